```python
import jax, jax.numpy as jnp
from jax import lax
import numpy as np

D_MODEL = 2048
BATCH = 4
SEQ = 2048
DEPTH = 1

CTX_LEN = 256
GRID_W = 64
MIX_W = D_MODEL
HEAD_SIZE = 128
ATTN_W = MIX_W // 2
N_HEADS = ATTN_W // HEAD_SIZE
QK_NOPE = 128
QK_ROPE = 64
V_DIM = HEAD_SIZE
Q_RANK = D_MODEL // 4
KV_RANK = D_MODEL // 8
CONV_W = MIX_W - ATTN_W
CONV_GROUPS = CONV_W // HEAD_SIZE
N_GROUPS = MIX_W // HEAD_SIZE
FFN_DIM = ((8 * D_MODEL // 3 + 127) // 128) * 128
IN_COLS = Q_RANK + KV_RANK + QK_ROPE + 3 * CONV_W
ROPE_BASE = 10000.0
Q_BLOCK = 128
EPS = 1e-6

kernel_name = "hybrid_mla_shortconv_dit_block"


def rms_norm(x, g):
    xf = x.astype(jnp.float32)
    y = xf * lax.rsqrt(jnp.mean(xf * xf, axis=-1, keepdims=True) + EPS)
    return (y * g.astype(jnp.float32)).astype(x.dtype)


def modulate(h, shift, scale):
    return h * (1 + scale) + shift


def axial_rope_tables(n_tok, dtype):
    rows = n_tok // GRID_W
    row = jnp.repeat(jnp.arange(rows), GRID_W).astype(jnp.float32)
    col = jnp.tile(jnp.arange(GRID_W), rows).astype(jnp.float32)
    axis_dim = QK_ROPE // 2
    inv = ROPE_BASE ** (-jnp.arange(0, axis_dim, 2, dtype=jnp.float32) / axis_dim)
    ang = jnp.concatenate([row[:, None] * inv, col[:, None] * inv], axis=-1)
    return jnp.cos(ang).astype(dtype), jnp.sin(ang).astype(dtype)


def apply_rope(x, cos, sin):
    xp = x.reshape(*x.shape[:-1], QK_ROPE // 2, 2)
    x1, x2 = xp[..., 0], xp[..., 1]
    return jnp.stack([x1 * cos - x2 * sin, x1 * sin + x2 * cos], axis=-1).reshape(x.shape)


def dwconv3(x, w, b):
    xp = jnp.pad(x, ((0, 0), (1, 1), (0, 0)))
    return xp[:, :-2] * w[0] + xp[:, 1:-1] * w[1] + xp[:, 2:] * w[2] + b


def split_projection(p):
    cuts = [Q_RANK, Q_RANK + KV_RANK, Q_RANK + KV_RANK + QK_ROPE,
            Q_RANK + KV_RANK + QK_ROPE + CONV_W, Q_RANK + KV_RANK + QK_ROPE + 2 * CONV_W]
    return jnp.split(p, cuts, axis=-1)


def mla_query(q_a, g_q_a, w_q_b, rope):
    b, l, _ = q_a.shape
    q = (rms_norm(q_a, g_q_a) @ w_q_b).reshape(b, l, N_HEADS, QK_NOPE + QK_ROPE).transpose(0, 2, 1, 3)
    q_nope, q_rope = q[..., :QK_NOPE], q[..., QK_NOPE:]
    if rope is not None:
        q_rope = apply_rope(q_rope, *rope)
    return jnp.concatenate([q_nope, q_rope], axis=-1)


def mla_key_value(kv_a, k_rope, g_kv_a, w_kv_b, rope):
    b, l, _ = kv_a.shape
    kv = (rms_norm(kv_a, g_kv_a) @ w_kv_b).reshape(b, l, N_HEADS, QK_NOPE + V_DIM).transpose(0, 2, 1, 3)
    k_nope, v = kv[..., :QK_NOPE], kv[..., QK_NOPE:]
    if rope is not None:
        k_rope = apply_rope(k_rope, *rope)
    k_rope = jnp.broadcast_to(k_rope[:, None], (b, N_HEADS, l, QK_ROPE))
    return jnp.concatenate([k_nope, k_rope], axis=-1), v


def block_attention(q, k, v):
    b, h, l, dqk = q.shape
    nb = l // Q_BLOCK
    scale = (QK_NOPE + QK_ROPE) ** -0.5
    qb = q.reshape(b, h, nb, Q_BLOCK, dqk).transpose(2, 0, 1, 3, 4)

    def one_block(qi):
        s = jnp.einsum("bhqd,bhkd->bhqk", qi, k).astype(jnp.float32) * scale
        p = jax.nn.softmax(s, axis=-1).astype(v.dtype)
        return jnp.einsum("bhqk,bhkd->bhqd", p, v)

    o = lax.map(one_block, qb)
    return o.transpose(1, 0, 3, 2, 4).reshape(b, l, h * V_DIM)


def short_conv(gate_b, gate_c, hc, w, bias):
    return gate_b * dwconv3(gate_c * hc, w, bias)


def merge_heads(att, conv, g_mix, w_out):
    b, l, _ = att.shape
    mix = jnp.concatenate([att, conv], axis=-1).reshape(b, l, N_GROUPS, HEAD_SIZE)
    mix = rms_norm(mix, g_mix.reshape(N_GROUPS, HEAD_SIZE)).reshape(b, l, MIX_W)
    return mix @ w_out


def conv_ffn(h, w_up, cw, cb, w_down):
    u = dwconv3(h @ w_up, cw, cb)
    a, g = jnp.split(u, 2, axis=-1)
    return (a * jax.nn.silu(g)) @ w_down


def setup_inputs(seed: int = 0) -> dict:
    key = jax.random.key(seed)
    ks = jax.random.split(key, 24)
    f32 = jnp.float32
    nrm = lambda k, shape, s: jax.random.normal(k, shape, f32) * s
    gain = lambda k, shape: 1.0 + 0.02 * jax.random.normal(k, shape, f32)
    return {
        "x": nrm(ks[0], (BATCH, SEQ, D_MODEL), 1.0),
        "c": nrm(ks[1], (BATCH, D_MODEL), 1.0),
        "ctx": nrm(ks[2], (BATCH, CTX_LEN, D_MODEL), 1.0),
        "c_ctx": nrm(ks[3], (D_MODEL,), 1.0),
        "w_ada": nrm(ks[4], (DEPTH, D_MODEL, 6 * D_MODEL), D_MODEL ** -0.5),
        "b_ada": nrm(ks[5], (DEPTH, 6 * D_MODEL), 0.01),
        "g_mix_norm": gain(ks[6], (DEPTH, D_MODEL)),
        "w_in": nrm(ks[7], (DEPTH, D_MODEL, IN_COLS), D_MODEL ** -0.5),
        "g_q_a": gain(ks[8], (DEPTH, Q_RANK)),
        "w_q_b": nrm(ks[9], (DEPTH, Q_RANK, N_HEADS * (QK_NOPE + QK_ROPE)), Q_RANK ** -0.5),
        "g_kv_a": gain(ks[10], (DEPTH, KV_RANK)),
        "w_kv_b": nrm(ks[11], (DEPTH, KV_RANK, N_HEADS * (QK_NOPE + V_DIM)), KV_RANK ** -0.5),
        "conv_w": nrm(ks[12], (DEPTH, 3, CONV_W), 3 ** -0.5),
        "conv_b": nrm(ks[13], (DEPTH, CONV_W), 0.01),
        "g_mix_out": gain(ks[14], (DEPTH, MIX_W)),
        "w_out": nrm(ks[15], (DEPTH, MIX_W, D_MODEL), MIX_W ** -0.5),
        "g_ffn_norm": gain(ks[16], (DEPTH, D_MODEL)),
        "w_up": nrm(ks[17], (DEPTH, D_MODEL, 2 * FFN_DIM), D_MODEL ** -0.5),
        "ffn_conv_w": nrm(ks[18], (DEPTH, 3, 2 * FFN_DIM), 3 ** -0.5),
        "ffn_conv_b": nrm(ks[19], (DEPTH, 2 * FFN_DIM), 0.01),
        "w_down": nrm(ks[20], (DEPTH, FFN_DIM, D_MODEL), FFN_DIM ** -0.5),
        "g_final": gain(ks[21], (D_MODEL,)),
    }


def reference(x, c, ctx, c_ctx, w_ada, b_ada, g_mix_norm, w_in, g_q_a, w_q_b, g_kv_a, w_kv_b,
              conv_w, conv_b, g_mix_out, w_out, g_ffn_norm, w_up, ffn_conv_w, ffn_conv_b,
              w_down, g_final):
    n_tok = x.shape[1]
    rope_lat = axial_rope_tables(n_tok, x.dtype)
    xc = ctx
    for l in range(DEPTH):
        last = l == DEPTH - 1
        mod_l = (jax.nn.silu(c) @ w_ada[l] + b_ada[l])[:, None, :]
        mod_c = (jax.nn.silu(c_ctx) @ w_ada[l] + b_ada[l])[None, None, :]
        sh_a, sc_a, gt_a, sh_f, sc_f, gt_f = jnp.split(mod_l, 6, axis=-1)
        csh_a, csc_a, cgt_a, csh_f, csc_f, cgt_f = jnp.split(mod_c, 6, axis=-1)

        h_l = modulate(rms_norm(x, g_mix_norm[l]), sh_a, sc_a)
        h_c = modulate(rms_norm(xc, g_mix_norm[l]), csh_a, csc_a)
        qa_l, kva_l, kr_l, cb_l, cc_l, ch_l = split_projection(h_l @ w_in[l])
        qa_c, kva_c, kr_c, cb_c, cc_c, ch_c = split_projection(h_c @ w_in[l])

        k_l, v_l = mla_key_value(kva_l, kr_l, g_kv_a[l], w_kv_b[l], rope_lat)
        k_c, v_c = mla_key_value(kva_c, kr_c, g_kv_a[l], w_kv_b[l], None)
        q_l = mla_query(qa_l, g_q_a[l], w_q_b[l], rope_lat)
        att_l = block_attention(q_l, jnp.concatenate([k_l, k_c], axis=2),
                                jnp.concatenate([v_l, v_c], axis=2))
        conv_l = short_conv(cb_l, cc_l, ch_l, conv_w[l], conv_b[l])
        x = x + gt_a * merge_heads(att_l, conv_l, g_mix_out[l], w_out[l])

        if not last:
            q_c = mla_query(qa_c, g_q_a[l], w_q_b[l], None)
            att_c = block_attention(q_c, k_c, v_c)
            conv_c = short_conv(cb_c, cc_c, ch_c, conv_w[l], conv_b[l])
            xc = xc + cgt_a * merge_heads(att_c, conv_c, g_mix_out[l], w_out[l])
            hf_c = modulate(rms_norm(xc, g_ffn_norm[l]), csh_f, csc_f)
            xc = xc + cgt_f * conv_ffn(hf_c, w_up[l], ffn_conv_w[l], ffn_conv_b[l], w_down[l])

        hf_l = modulate(rms_norm(x, g_ffn_norm[l]), sh_f, sc_f)
        x = x + gt_f * conv_ffn(hf_l, w_up[l], ffn_conv_w[l], ffn_conv_b[l], w_down[l])
    return rms_norm(x, g_final)
```

```python
import functools

import jax
import jax.numpy as jnp
from jax import lax
from jax.experimental import pallas as pl
from jax.experimental.pallas import tpu as pltpu

F32 = jnp.float32
BF16 = jnp.bfloat16

D_MODEL = 2048
N_HEADS = 8
HEAD = 128
QK_NOPE = 128
QK_ROPE = 64
QK_DIM = QK_NOPE + QK_ROPE
Q_RANK = 512
KV_RANK = 256
CONV_W = 1024
GRID_W = 64
ROPE_BASE = 10000.0
EPS = 1e-6
FFN_DIM = 5504

V7X_VMEM_BYTES = 64 * 1024 * 1024
VMEM_LIMIT = V7X_VMEM_BYTES - 8 * 1024 * 1024

TM_IN = 256
TQ = 512
TM_OUT = 512
TM_FFN = 512
TF = 512
HALO = 16
FFN_PAD = ((FFN_DIM + TF - 1) // TF) * TF


def _rms(x, g):
    return x * lax.rsqrt(jnp.mean(x * x, axis=-1, keepdims=True) + EPS) * g


def _silu(x):
    return x / (1.0 + jnp.exp(-x))


def _dot(a, b):
    return jnp.dot(a, b, preferred_element_type=F32)


def _rope(x, c, s):
    w = x.shape[-1]
    lane = lax.broadcasted_iota(jnp.int32, x.shape, 1)
    from_right = pltpu.roll(x, w - 1, 1)
    from_left = pltpu.roll(x, 1, 1)
    partner = jnp.where(lane % 2 == 0, from_right, from_left)
    return x * c + partner * s


def _ada_kernel(c_ref, w_ref, b_ref, o_ref):
    s = _silu(c_ref[...]).astype(BF16)
    o_ref[...] = _dot(s, w_ref[...].astype(BF16)) + b_ref[...]


def _ada(c8, w, b):
    n = w.shape[1]
    tn = 1024
    return pl.pallas_call(
        _ada_kernel,
        grid=(n // tn,),
        in_specs=[
            pl.BlockSpec((8, D_MODEL), lambda j: (0, 0)),
            pl.BlockSpec((D_MODEL, tn), lambda j: (0, j)),
            pl.BlockSpec((1, tn), lambda j: (0, j)),
        ],
        out_specs=pl.BlockSpec((8, tn), lambda j: (0, j)),
        out_shape=jax.ShapeDtypeStruct((8, n), F32),
        compiler_params=pltpu.CompilerParams(
            dimension_semantics=("parallel",), vmem_limit_bytes=VMEM_LIMIT),
        name="ada",
    )(c8, w, b)


def _in_lat_kernel(x_ref, mod_ref, gmix_ref, wqkv_ref, wconv_ref, gq_ref, wqb_ref, gkv_ref,
                   wkvb_ref, cq_ref, sq_ref, q_ref, k_ref, v_ref, cb_ref, z_ref):
    sh = mod_ref[0, 0:1, :]
    sc = mod_ref[0, 1:2, :]
    h = (_rms(x_ref[...], gmix_ref[...]) * (1.0 + sc) + sh).astype(BF16)

    p2 = _dot(h, wconv_ref[...])
    cb_ref[...] = p2[:, :CONV_W]
    z_ref[...] = p2[:, CONV_W:2 * CONV_W] * p2[:, 2 * CONV_W:]

    p1 = _dot(h, wqkv_ref[...])
    qa = p1[:, :Q_RANK]
    kva = p1[:, Q_RANK:Q_RANK + KV_RANK]
    kr = p1[:, Q_RANK + KV_RANK:]

    cq = cq_ref[...]
    sq = sq_ref[...]
    scale = QK_DIM ** -0.5
    q = _dot(_rms(qa, gq_ref[...]).astype(BF16), wqb_ref[...])
    qn = (q[:, :N_HEADS * QK_NOPE] * scale).astype(BF16)
    qr = (_rope(q[:, N_HEADS * QK_NOPE:], cq, sq) * scale).astype(BF16)

    kv = _dot(_rms(kva, gkv_ref[...]).astype(BF16), wkvb_ref[...]).astype(BF16)
    krr = _rope(kr, cq[:, :128], sq[:, :128])[:, :QK_ROPE].astype(BF16)
    for hd in range(N_HEADS):
        q_ref[0, hd, :, 0:QK_NOPE] = qn[:, hd * QK_NOPE:(hd + 1) * QK_NOPE]
        q_ref[0, hd, :, QK_NOPE:QK_DIM] = qr[:, hd * QK_ROPE:(hd + 1) * QK_ROPE]
        k_ref[0, hd, :, 0:QK_NOPE] = kv[:, hd * 2 * HEAD:hd * 2 * HEAD + QK_NOPE]
        k_ref[0, hd, :, QK_NOPE:QK_DIM] = krr
        v_ref[0, hd, :, :] = kv[:, hd * 2 * HEAD + QK_NOPE:(hd + 1) * 2 * HEAD]


def _in_lat(x2, mod_l, gmix, wqkv, wconv, gq, wqb, gkv, wkvb, cq, sq, batch, seq):
    tm = TM_IN
    tpb = seq // tm
    const = lambda b, t: (0, 0)
    one = pl.Buffered(1)
    tok = lambda b, t: (b * tpb + t, 0)
    hd4 = lambda b, t: (b, 0, t, 0)
    return pl.pallas_call(
        _in_lat_kernel,
        grid=(batch, tpb),
        in_specs=[
            pl.BlockSpec((tm, D_MODEL), tok),
            pl.BlockSpec((1, 6, D_MODEL), lambda b, t: (b, 0, 0)),
            pl.BlockSpec((1, D_MODEL), const),
            pl.BlockSpec(wqkv.shape, const, pipeline_mode=one),
            pl.BlockSpec(wconv.shape, const, pipeline_mode=one),
            pl.BlockSpec((1, Q_RANK), const),
            pl.BlockSpec(wqb.shape, const, pipeline_mode=one),
            pl.BlockSpec((1, KV_RANK), const),
            pl.BlockSpec(wkvb.shape, const, pipeline_mode=one),
            pl.BlockSpec((tm, N_HEADS * QK_ROPE), lambda b, t: (t, 0)),
            pl.BlockSpec((tm, N_HEADS * QK_ROPE), lambda b, t: (t, 0)),
        ],
        out_specs=[
            pl.BlockSpec((1, N_HEADS, tm, QK_DIM), hd4),
            pl.BlockSpec((1, N_HEADS, tm, QK_DIM), hd4),
            pl.BlockSpec((1, N_HEADS, tm, HEAD), hd4),
            pl.BlockSpec((tm, CONV_W), tok),
            pl.BlockSpec((tm, CONV_W), tok),
        ],
        out_shape=[
            jax.ShapeDtypeStruct((batch, N_HEADS, seq, QK_DIM), BF16),
            jax.ShapeDtypeStruct((batch, N_HEADS, seq, QK_DIM), BF16),
            jax.ShapeDtypeStruct((batch, N_HEADS, seq, HEAD), BF16),
            jax.ShapeDtypeStruct((batch * seq, CONV_W), F32),
            jax.ShapeDtypeStruct((batch * seq, CONV_W), F32),
        ],
        compiler_params=pltpu.CompilerParams(
            dimension_semantics=("parallel", "parallel"), vmem_limit_bytes=VMEM_LIMIT),
        name="in_lat",
    )(x2, mod_l, gmix, wqkv, wconv, gq, wqb, gkv, wkvb, cq, sq)


def _in_ctx_kernel(x_ref, mod_ref, gmix_ref, wkv_ref, gkv_ref, wkvb_ref, k_ref, v_ref):
    sh = mod_ref[0, 0:1, :]
    sc = mod_ref[0, 1:2, :]
    h = (_rms(x_ref[0], gmix_ref[...]) * (1.0 + sc) + sh).astype(BF16)
    p1 = _dot(h, wkv_ref[...])
    kva = p1[:, :KV_RANK]
    kr = p1[:, KV_RANK:KV_RANK + QK_ROPE].astype(BF16)
    kv = _dot(_rms(kva, gkv_ref[...]).astype(BF16), wkvb_ref[...]).astype(BF16)
    for hd in range(N_HEADS):
        k_ref[0, hd, :, 0:QK_NOPE] = kv[:, hd * 2 * HEAD:hd * 2 * HEAD + QK_NOPE]
        k_ref[0, hd, :, QK_NOPE:QK_DIM] = kr
        v_ref[0, hd, :, :] = kv[:, hd * 2 * HEAD + QK_NOPE:(hd + 1) * 2 * HEAD]


def _in_ctx(ctx, mod_c, gmix, wkv, gkv, wkvb):
    batch, n_ctx, _ = ctx.shape
    const = lambda b: (0, 0)
    return pl.pallas_call(
        _in_ctx_kernel,
        grid=(batch,),
        in_specs=[
            pl.BlockSpec((1, n_ctx, D_MODEL), lambda b: (b, 0, 0)),
            pl.BlockSpec((1, 6, D_MODEL), lambda b: (0, 0, 0)),
            pl.BlockSpec((1, D_MODEL), const),
            pl.BlockSpec(wkv.shape, const),
            pl.BlockSpec((1, KV_RANK), const),
            pl.BlockSpec(wkvb.shape, const),
        ],
        out_specs=[
            pl.BlockSpec((1, N_HEADS, n_ctx, QK_DIM), lambda b: (b, 0, 0, 0)),
            pl.BlockSpec((1, N_HEADS, n_ctx, HEAD), lambda b: (b, 0, 0, 0)),
        ],
        out_shape=[
            jax.ShapeDtypeStruct((batch, N_HEADS, n_ctx, QK_DIM), BF16),
            jax.ShapeDtypeStruct((batch, N_HEADS, n_ctx, HEAD), BF16),
        ],
        compiler_params=pltpu.CompilerParams(
            dimension_semantics=("parallel",), vmem_limit_bytes=VMEM_LIMIT),
        name="in_ctx",
    )(ctx, mod_c, gmix, wkv, gkv, wkvb)


def _attn_kernel(q_ref, kl_ref, vl_ref, kc_ref, vc_ref, o_ref):
    q = q_ref[0, 0]
    nt = (((1,), (1,)), ((), ()))
    s_l = lax.dot_general(q, kl_ref[0, 0], nt, preferred_element_type=F32)
    s_c = lax.dot_general(q, kc_ref[0, 0], nt, preferred_element_type=F32)
    m = jnp.maximum(jnp.max(s_l, axis=-1, keepdims=True), jnp.max(s_c, axis=-1, keepdims=True))
    p_l = jnp.exp(s_l - m)
    p_c = jnp.exp(s_c - m)
    denom = jnp.sum(p_l, axis=-1, keepdims=True) + jnp.sum(p_c, axis=-1, keepdims=True)
    o = _dot(p_l.astype(BF16), vl_ref[0, 0]) + _dot(p_c.astype(BF16), vc_ref[0, 0])
    o_ref[0] = o / denom


def _attn(q, k_l, v_l, k_c, v_c):
    batch, heads, seq, _ = q.shape
    n_ctx = k_c.shape[2]
    kv_map = lambda b, h, i: (b, h, 0, 0)
    return pl.pallas_call(
        _attn_kernel,
        grid=(batch, heads, seq // TQ),
        in_specs=[
            pl.BlockSpec((1, 1, TQ, QK_DIM), lambda b, h, i: (b, h, i, 0)),
            pl.BlockSpec((1, 1, seq, QK_DIM), kv_map),
            pl.BlockSpec((1, 1, seq, HEAD), kv_map),
            pl.BlockSpec((1, 1, n_ctx, QK_DIM), kv_map),
            pl.BlockSpec((1, 1, n_ctx, HEAD), kv_map),
        ],
        out_specs=pl.BlockSpec((1, TQ, HEAD), lambda b, h, i: (b, i, h)),
        out_shape=jax.ShapeDtypeStruct((batch, seq, heads * HEAD), F32),
        compiler_params=pltpu.CompilerParams(
            dimension_semantics=("parallel", "parallel", "parallel"),
            vmem_limit_bytes=VMEM_LIMIT),
        name="attn",
    )(q, k_l, v_l, k_c, v_c)


def _out_kernel(att_ref, cb_ref, z_ref, zp_ref, zn_ref, cw_ref, cbias_ref, gmo_ref, wout_ref,
                x_ref, mod_ref, gffn_ref, x1_ref, hf_ref, mix_ref):
    t = pl.program_id(1)
    tm = z_ref.shape[0]
    z = z_ref[...]
    row = lax.broadcasted_iota(jnp.int32, z.shape, 0)
    z_before = jnp.where(t == 0, 0.0, zp_ref[0, 7:8, :])
    z_after = jnp.where(t == pl.num_programs(1) - 1, 0.0, zn_ref[0, 0:1, :])
    z_prev = jnp.where(row == 0, z_before, pltpu.roll(z, 1, 0))
    z_next = jnp.where(row == tm - 1, z_after, pltpu.roll(z, tm - 1, 0))
    conv = cb_ref[...] * (z_prev * cw_ref[0:1, :] + z * cw_ref[1:2, :] + z_next * cw_ref[2:3, :]
                          + cbias_ref[...])

    att = att_ref[...]
    n_att = att.shape[1] // HEAD
    for g in range(D_MODEL // HEAD):
        blk = att[:, g * HEAD:(g + 1) * HEAD] if g < n_att else conv[:, (g - n_att) * HEAD:(g - n_att + 1) * HEAD]
        mix_ref[:, g * HEAD:(g + 1) * HEAD] = _rms(blk, gmo_ref[:, g * HEAD:(g + 1) * HEAD]).astype(BF16)

    y = _dot(mix_ref[...], wout_ref[...])
    gt_a = mod_ref[0, 2:3, :]
    x1 = x_ref[...] + gt_a * y
    x1_ref[...] = x1
    hf_ref[...] = (_rms(x1, gffn_ref[...]) * (1.0 + mod_ref[0, 4:5, :]) + mod_ref[0, 3:4, :]).astype(BF16)


def _out(att2, cb, z, cw, cbias, gmo, wout, x2, mod_l, gffn, batch, seq):
    tm = TM_OUT
    tpb = seq // tm
    n8 = batch * seq // 8
    z3 = z.reshape(n8, 8, CONV_W)
    const = lambda b, t: (0, 0)
    tok = lambda b, t: (b * tpb + t, 0)
    return pl.pallas_call(
        _out_kernel,
        grid=(batch, tpb),
        in_specs=[
            pl.BlockSpec((tm, att2.shape[1]), tok),
            pl.BlockSpec((tm, CONV_W), tok),
            pl.BlockSpec((tm, CONV_W), tok),
            pl.BlockSpec((1, 8, CONV_W), lambda b, t: (jnp.maximum((b * tpb + t) * (tm // 8) - 1, 0), 0, 0)),
            pl.BlockSpec((1, 8, CONV_W), lambda b, t: (jnp.minimum((b * tpb + t + 1) * (tm // 8), n8 - 1), 0, 0)),
            pl.BlockSpec((3, CONV_W), const),
            pl.BlockSpec((1, CONV_W), const),
            pl.BlockSpec((1, D_MODEL), const),
            pl.BlockSpec(wout.shape, const, pipeline_mode=pl.Buffered(1)),
            pl.BlockSpec((tm, D_MODEL), tok),
            pl.BlockSpec((1, 6, D_MODEL), lambda b, t: (b, 0, 0)),
            pl.BlockSpec((1, D_MODEL), const),
        ],
        out_specs=[
            pl.BlockSpec((tm, D_MODEL), tok),
            pl.BlockSpec((tm, D_MODEL), tok),
        ],
        out_shape=[
            jax.ShapeDtypeStruct((batch * seq, D_MODEL), F32),
            jax.ShapeDtypeStruct((batch * seq, D_MODEL), BF16),
        ],
        scratch_shapes=[pltpu.VMEM((tm, D_MODEL), BF16)],
        compiler_params=pltpu.CompilerParams(
            dimension_semantics=("parallel", "parallel"), vmem_limit_bytes=VMEM_LIMIT),
        name="out_proj",
    )(att2, cb, z, z3, z3, cw, cbias, gmo, wout, x2, mod_l, gffn)


def _ffn_kernel(h_ref, hp_ref, hn_ref, wup_ref, cw_ref, cbias_ref, wdn_ref, x1_ref, mod_ref, gfin_ref,
                o_ref, hbuf_ref, u_ref):
    t = pl.program_id(1)
    f = pl.program_id(2)
    tm = h_ref.shape[0]

    @pl.when(f == 0)
    def _():
        hbuf_ref[HALO:HALO + tm, :] = h_ref[...]
        hbuf_ref[0:HALO, :] = jnp.where(t == 0, jnp.zeros_like(hp_ref[0]), hp_ref[0])
        hbuf_ref[HALO + tm:, :] = jnp.where(t == pl.num_programs(1) - 1, jnp.zeros_like(hn_ref[0]), hn_ref[0])
        o_ref[...] = jnp.zeros_like(o_ref)

    u_ref[...] = _dot(hbuf_ref[...], wup_ref[0])
    u = (u_ref[pl.ds(HALO - 1, tm), :] * cw_ref[0, 0:1, :] + u_ref[pl.ds(HALO, tm), :] * cw_ref[0, 1:2, :]
         + u_ref[pl.ds(HALO + 1, tm), :] * cw_ref[0, 2:3, :] + cbias_ref[0])
    act = (u[:, :TF] * _silu(u[:, TF:])).astype(BF16)
    o_ref[...] += _dot(act, wdn_ref[...])

    @pl.when(f == pl.num_programs(2) - 1)
    def _():
        y = x1_ref[...] + mod_ref[0, 5:6, :] * o_ref[...]
        o_ref[...] = _rms(y, gfin_ref[...])


def _ffn(hf, wup_t, cw_t, cbias_t, wdn, x1, mod_l, gfin, batch, seq):
    tm = TM_FFN
    tpb = seq // tm
    nf = wup_t.shape[0]
    n_h = batch * seq // HALO
    hf3 = hf.reshape(n_h, HALO, D_MODEL)
    tok = lambda b, t, f: (b * tpb + t, 0)
    return pl.pallas_call(
        _ffn_kernel,
        grid=(batch, tpb, nf),
        in_specs=[
            pl.BlockSpec((tm, D_MODEL), tok),
            pl.BlockSpec((1, HALO, D_MODEL),
                         lambda b, t, f: (jnp.maximum((b * tpb + t) * (tm // HALO) - 1, 0), 0, 0)),
            pl.BlockSpec((1, HALO, D_MODEL),
                         lambda b, t, f: (jnp.minimum((b * tpb + t + 1) * (tm // HALO), n_h - 1), 0, 0)),
            pl.BlockSpec((1, D_MODEL, 2 * TF), lambda b, t, f: (f, 0, 0)),
            pl.BlockSpec((1, 3, 2 * TF), lambda b, t, f: (f, 0, 0)),
            pl.BlockSpec((1, 1, 2 * TF), lambda b, t, f: (f, 0, 0)),
            pl.BlockSpec((TF, D_MODEL), lambda b, t, f: (f, 0)),
            pl.BlockSpec((tm, D_MODEL), tok),
            pl.BlockSpec((1, 6, D_MODEL), lambda b, t, f: (b, 0, 0)),
            pl.BlockSpec((1, D_MODEL), lambda b, t, f: (0, 0)),
        ],
        out_specs=pl.BlockSpec((tm, D_MODEL), tok),
        out_shape=jax.ShapeDtypeStruct((batch * seq, D_MODEL), F32),
        scratch_shapes=[
            pltpu.VMEM((tm + 2 * HALO, D_MODEL), BF16),
            pltpu.VMEM((tm + 2 * HALO, 2 * TF), F32),
        ],
        compiler_params=pltpu.CompilerParams(
            dimension_semantics=("parallel", "parallel", "arbitrary"), vmem_limit_bytes=VMEM_LIMIT),
        name="ffn",
    )(hf, hf3, hf3, wup_t, cw_t, cbias_t, wdn, x1, mod_l, gfin)


def _rope_tables(seq):
    rows = seq // GRID_W
    row = jnp.repeat(jnp.arange(rows), GRID_W).astype(F32)
    col = jnp.tile(jnp.arange(GRID_W), rows).astype(F32)
    axis_dim = QK_ROPE // 2
    inv = ROPE_BASE ** (-jnp.arange(0, axis_dim, 2, dtype=F32) / axis_dim)
    ang = jnp.concatenate([row[:, None] * inv, col[:, None] * inv], axis=-1)
    cos, sin = jnp.cos(ang), jnp.sin(ang)
    c = jnp.repeat(cos, 2, axis=-1)
    s = jnp.stack([-sin, sin], axis=-1).reshape(seq, QK_ROPE)
    return jnp.tile(c, (1, N_HEADS)), jnp.tile(s, (1, N_HEADS))


def _tile_ffn_cols(a, nf):
    pad = [(0, 0)] * (a.ndim - 1) + [(0, FFN_PAD - FFN_DIM)]
    val = jnp.pad(a[..., :FFN_DIM], pad).reshape(*a.shape[:-1], nf, TF)
    gate = jnp.pad(a[..., FFN_DIM:], pad).reshape(*a.shape[:-1], nf, TF)
    return jnp.moveaxis(jnp.concatenate([val, gate], axis=-1), -2, 0)


def kernel(x, c, ctx, c_ctx, w_ada, b_ada, g_mix_norm, w_in, g_q_a, w_q_b, g_kv_a, w_kv_b, conv_w, conv_b, g_mix_out, w_out, g_ffn_norm, w_up, ffn_conv_w, ffn_conv_b, w_down, g_final):
    batch, seq, d = x.shape
    assert d == D_MODEL and w_ada.shape[0] == 1
    x2 = x.reshape(batch * seq, d)

    c8 = jnp.concatenate([c, c_ctx[None], jnp.zeros((8 - batch - 1, d), F32)], axis=0)
    mod = _ada(c8, w_ada[0], b_ada[0][None])
    mod_l = mod[:batch].reshape(batch, 6, d)
    mod_c = mod[batch:batch + 1].reshape(1, 6, d)

    w_in0 = w_in[0]
    n_qkv = Q_RANK + KV_RANK + QK_ROPE
    wqkv = jnp.pad(w_in0[:, :n_qkv], ((0, 0), (0, 128 - QK_ROPE))).astype(BF16)
    wkv_c = jnp.pad(w_in0[:, Q_RANK:n_qkv], ((0, 0), (0, 128 - QK_ROPE))).astype(BF16)
    wconv = w_in0[:, n_qkv:].astype(BF16)
    wqb3 = w_q_b[0].reshape(Q_RANK, N_HEADS, QK_DIM)
    wqb = jnp.concatenate([wqb3[:, :, :QK_NOPE].reshape(Q_RANK, -1),
                           wqb3[:, :, QK_NOPE:].reshape(Q_RANK, -1)], axis=-1).astype(BF16)
    wkvb = w_kv_b[0].astype(BF16)
    wout = w_out[0].astype(BF16)
    nf = FFN_PAD // TF
    wup_t = _tile_ffn_cols(w_up[0], nf).astype(BF16)
    cw_t = _tile_ffn_cols(ffn_conv_w[0], nf)
    cbias_t = _tile_ffn_cols(ffn_conv_b[0][None], nf)
    wdn = jnp.pad(w_down[0], ((0, FFN_PAD - FFN_DIM), (0, 0))).astype(BF16)

    cq, sq = _rope_tables(seq)

    q, k_l, v_l, cb, z = _in_lat(x2, mod_l, g_mix_norm, wqkv, wconv, g_q_a, wqb, g_kv_a, wkvb,
                                 cq, sq, batch, seq)
    k_c, v_c = _in_ctx(ctx, mod_c, g_mix_norm, wkv_c, g_kv_a, wkvb)
    att = _attn(q, k_l, v_l, k_c, v_c)
    x1, hf = _out(att.reshape(batch * seq, N_HEADS * HEAD), cb, z, conv_w[0], conv_b, g_mix_out,
                  wout, x2, mod_l, g_ffn_norm, batch, seq)
    y = _ffn(hf, wup_t, cw_t, cbias_t, wdn, x1, mod_l, g_final[None], batch, seq)
    return y.reshape(batch, seq, d)
```

```python
import functools

import jax
import jax.numpy as jnp
from jax import lax
from jax.experimental import pallas as pl
from jax.experimental.pallas import tpu as pltpu

F32 = jnp.float32
BF16 = jnp.bfloat16

D_MODEL = 2048
N_HEADS = 8
HEAD = 128
QK_NOPE = 128
QK_ROPE = 64
QK_DIM = QK_NOPE + QK_ROPE
Q_RANK = 512
KV_RANK = 256
CONV_W = 1024
GRID_W = 64
ROPE_BASE = 10000.0
EPS = 1e-6
LOG2E = 1.4426950408889634
FFN_DIM = 5504

V7X_VMEM_BYTES = 64 * 1024 * 1024
VMEM_LIMIT = V7X_VMEM_BYTES - 8 * 1024 * 1024

TM_IN = 256
TQ = 1024
TQ_SUB = 256
TM_OUT = 512
TM_FFN = 512
LANES = 128
TF_SUB = 512
FFN_SUBS = 1
HALO = 16
EW_ROWS = 32
BPS = TF_SUB // LANES
FFN_BLKS = FFN_DIM // LANES
N_SUB = (FFN_DIM + TF_SUB - 1) // TF_SUB
FFN_PAD = N_SUB * TF_SUB


def _rms(x, g):
    return x * lax.rsqrt(jnp.mean(x * x, axis=-1, keepdims=True) + EPS) * g


def _silu(x):
    return x / (1.0 + jnp.exp(-x))


def _dot(a, b):
    return jnp.dot(a, b, preferred_element_type=F32)


def _rope(x, c, s):
    w = x.shape[-1]
    lane = lax.broadcasted_iota(jnp.int32, x.shape, 1)
    from_right = pltpu.roll(x, w - 1, 1)
    from_left = pltpu.roll(x, 1, 1)
    partner = jnp.where(lane % 2 == 0, from_right, from_left)
    return x * c + partner * s


def _ada_kernel(c_ref, w_ref, b_ref, o_ref):
    s = _silu(c_ref[...]).astype(BF16)
    o_ref[...] = _dot(s, w_ref[...].astype(BF16)) + b_ref[...]


def _ada(c8, w, b):
    n = w.shape[1]
    tn = 1024
    return pl.pallas_call(
        _ada_kernel,
        grid=(n // tn,),
        in_specs=[
            pl.BlockSpec((8, D_MODEL), lambda j: (0, 0)),
            pl.BlockSpec((D_MODEL, tn), lambda j: (0, j)),
            pl.BlockSpec((1, tn), lambda j: (0, j)),
        ],
        out_specs=pl.BlockSpec((8, tn), lambda j: (0, j)),
        out_shape=jax.ShapeDtypeStruct((8, n), F32),
        compiler_params=pltpu.CompilerParams(
            dimension_semantics=("parallel",), vmem_limit_bytes=VMEM_LIMIT),
        name="ada",
    )(c8, w, b)


def _in_lat_kernel(x_ref, mod_ref, gmix_ref, wqkv_ref, wconv_ref, gq_ref, wqb_ref, gkv_ref,
                   wkvb_ref, cq_ref, sq_ref, q_ref, k_ref, v_ref, cb_ref, z_ref):
    sh = mod_ref[0, 0:1, :]
    sc = mod_ref[0, 1:2, :]
    h = (_rms(x_ref[...], gmix_ref[...]) * (1.0 + sc) + sh).astype(BF16)

    p2 = _dot(h, wconv_ref[...])
    cb_ref[...] = p2[:, :CONV_W]
    z_ref[...] = p2[:, CONV_W:2 * CONV_W] * p2[:, 2 * CONV_W:]

    p1 = _dot(h, wqkv_ref[...])
    qa = p1[:, :Q_RANK]
    kva = p1[:, Q_RANK:Q_RANK + KV_RANK]
    kr = p1[:, Q_RANK + KV_RANK:]

    cq = cq_ref[...]
    sq = sq_ref[...]
    scale = QK_DIM ** -0.5 * LOG2E
    q = _dot(_rms(qa, gq_ref[...]).astype(BF16), wqb_ref[...])
    qn = (q[:, :N_HEADS * QK_NOPE] * scale).astype(BF16)
    qr = (_rope(q[:, N_HEADS * QK_NOPE:], cq, sq) * scale).astype(BF16)

    kv = _dot(_rms(kva, gkv_ref[...]).astype(BF16), wkvb_ref[...]).astype(BF16)
    krr = _rope(kr, cq[:, :128], sq[:, :128])[:, :QK_ROPE].astype(BF16)
    for hd in range(N_HEADS):
        q_ref[0, hd, :, 0:QK_NOPE] = qn[:, hd * QK_NOPE:(hd + 1) * QK_NOPE]
        q_ref[0, hd, :, QK_NOPE:QK_DIM] = qr[:, hd * QK_ROPE:(hd + 1) * QK_ROPE]
        k_ref[0, hd, :, 0:QK_NOPE] = kv[:, hd * 2 * HEAD:hd * 2 * HEAD + QK_NOPE]
        k_ref[0, hd, :, QK_NOPE:QK_DIM] = krr
        v_ref[0, hd, :, :] = kv[:, hd * 2 * HEAD + QK_NOPE:(hd + 1) * 2 * HEAD]


def _in_lat(x2, mod_l, gmix, wqkv, wconv, gq, wqb, gkv, wkvb, cq, sq, batch, seq):
    tm = TM_IN
    tpb = seq // tm
    const = lambda b, t: (0, 0)
    one = pl.Buffered(1)
    tok = lambda b, t: (b * tpb + t, 0)
    hd4 = lambda b, t: (b, 0, t, 0)
    return pl.pallas_call(
        _in_lat_kernel,
        grid=(batch, tpb),
        in_specs=[
            pl.BlockSpec((tm, D_MODEL), tok),
            pl.BlockSpec((1, 6, D_MODEL), lambda b, t: (b, 0, 0)),
            pl.BlockSpec((1, D_MODEL), const),
            pl.BlockSpec(wqkv.shape, const, pipeline_mode=one),
            pl.BlockSpec(wconv.shape, const, pipeline_mode=one),
            pl.BlockSpec((1, Q_RANK), const),
            pl.BlockSpec(wqb.shape, const, pipeline_mode=one),
            pl.BlockSpec((1, KV_RANK), const),
            pl.BlockSpec(wkvb.shape, const, pipeline_mode=one),
            pl.BlockSpec((tm, N_HEADS * QK_ROPE), lambda b, t: (t, 0)),
            pl.BlockSpec((tm, N_HEADS * QK_ROPE), lambda b, t: (t, 0)),
        ],
        out_specs=[
            pl.BlockSpec((1, N_HEADS, tm, QK_DIM), hd4),
            pl.BlockSpec((1, N_HEADS, tm, QK_DIM), hd4),
            pl.BlockSpec((1, N_HEADS, tm, HEAD), hd4),
            pl.BlockSpec((tm, CONV_W), tok),
            pl.BlockSpec((tm, CONV_W), tok),
        ],
        out_shape=[
            jax.ShapeDtypeStruct((batch, N_HEADS, seq, QK_DIM), BF16),
            jax.ShapeDtypeStruct((batch, N_HEADS, seq, QK_DIM), BF16),
            jax.ShapeDtypeStruct((batch, N_HEADS, seq, HEAD), BF16),
            jax.ShapeDtypeStruct((batch * seq, CONV_W), F32),
            jax.ShapeDtypeStruct((batch * seq, CONV_W), F32),
        ],
        compiler_params=pltpu.CompilerParams(
            dimension_semantics=("parallel", "parallel"), vmem_limit_bytes=VMEM_LIMIT),
        name="in_lat",
    )(x2, mod_l, gmix, wqkv, wconv, gq, wqb, gkv, wkvb, cq, sq)


def _in_ctx_kernel(x_ref, mod_ref, gmix_ref, wkv_ref, gkv_ref, wkvb_ref, k_ref, v_ref):
    sh = mod_ref[0, 0:1, :]
    sc = mod_ref[0, 1:2, :]
    h = (_rms(x_ref[0], gmix_ref[...]) * (1.0 + sc) + sh).astype(BF16)
    p1 = _dot(h, wkv_ref[:, Q_RANK:])
    kva = p1[:, :KV_RANK]
    kr = p1[:, KV_RANK:KV_RANK + QK_ROPE].astype(BF16)
    kv = _dot(_rms(kva, gkv_ref[...]).astype(BF16), wkvb_ref[...]).astype(BF16)
    for hd in range(N_HEADS):
        k_ref[0, hd, :, 0:QK_NOPE] = kv[:, hd * 2 * HEAD:hd * 2 * HEAD + QK_NOPE]
        k_ref[0, hd, :, QK_NOPE:QK_DIM] = kr
        v_ref[0, hd, :, :] = kv[:, hd * 2 * HEAD + QK_NOPE:(hd + 1) * 2 * HEAD]


def _in_ctx(ctx, mod_c, gmix, wkv, gkv, wkvb):
    batch, n_ctx, _ = ctx.shape
    const = lambda b: (0, 0)
    return pl.pallas_call(
        _in_ctx_kernel,
        grid=(batch,),
        in_specs=[
            pl.BlockSpec((1, n_ctx, D_MODEL), lambda b: (b, 0, 0)),
            pl.BlockSpec((1, 6, D_MODEL), lambda b: (0, 0, 0)),
            pl.BlockSpec((1, D_MODEL), const),
            pl.BlockSpec(wkv.shape, const),
            pl.BlockSpec((1, KV_RANK), const),
            pl.BlockSpec(wkvb.shape, const),
        ],
        out_specs=[
            pl.BlockSpec((1, N_HEADS, n_ctx, QK_DIM), lambda b: (b, 0, 0, 0)),
            pl.BlockSpec((1, N_HEADS, n_ctx, HEAD), lambda b: (b, 0, 0, 0)),
        ],
        out_shape=[
            jax.ShapeDtypeStruct((batch, N_HEADS, n_ctx, QK_DIM), BF16),
            jax.ShapeDtypeStruct((batch, N_HEADS, n_ctx, HEAD), BF16),
        ],
        compiler_params=pltpu.CompilerParams(
            dimension_semantics=("parallel",), vmem_limit_bytes=VMEM_LIMIT),
        name="in_ctx",
    )(ctx, mod_c, gmix, wkv, gkv, wkvb)


def _attn_kernel(q_ref, kl_ref, vl_ref, kc_ref, vc_ref, o_ref):
    nt = (((1,), (1,)), ((), ()))
    for j in range(TQ // TQ_SUB):
        rows = slice(j * TQ_SUB, (j + 1) * TQ_SUB)
        q = q_ref[0, 0, rows, :]
        s_l = lax.dot_general(q, kl_ref[0, 0], nt, preferred_element_type=F32)
        s_c = lax.dot_general(q, kc_ref[0, 0], nt, preferred_element_type=F32)
        m = jnp.maximum(jnp.max(s_l, axis=-1, keepdims=True), jnp.max(s_c, axis=-1, keepdims=True))
        p_l = jnp.exp2(s_l - m)
        p_c = jnp.exp2(s_c - m)
        denom = jnp.sum(p_l, axis=-1, keepdims=True) + jnp.sum(p_c, axis=-1, keepdims=True)
        o = _dot(p_l.astype(BF16), vl_ref[0, 0]) + _dot(p_c.astype(BF16), vc_ref[0, 0])
        o_ref[0, rows, :] = o / denom


def _attn(q, k_l, v_l, k_c, v_c):
    batch, heads, seq, _ = q.shape
    n_ctx = k_c.shape[2]
    kv_map = lambda b, h, i: (b, h, 0, 0)
    return pl.pallas_call(
        _attn_kernel,
        grid=(batch, heads, seq // TQ),
        in_specs=[
            pl.BlockSpec((1, 1, TQ, QK_DIM), lambda b, h, i: (b, h, i, 0)),
            pl.BlockSpec((1, 1, seq, QK_DIM), kv_map),
            pl.BlockSpec((1, 1, seq, HEAD), kv_map),
            pl.BlockSpec((1, 1, n_ctx, QK_DIM), kv_map),
            pl.BlockSpec((1, 1, n_ctx, HEAD), kv_map),
        ],
        out_specs=pl.BlockSpec((1, TQ, HEAD), lambda b, h, i: (b, i, h)),
        out_shape=jax.ShapeDtypeStruct((batch, seq, heads * HEAD), F32),
        compiler_params=pltpu.CompilerParams(
            dimension_semantics=("parallel", "parallel", "parallel"),
            vmem_limit_bytes=VMEM_LIMIT),
        name="attn",
    )(q, k_l, v_l, k_c, v_c)


def _out_kernel(att_ref, cb_ref, z_ref, zp_ref, zn_ref, cw_ref, cbias_ref, gmo_ref, wout_ref,
                x_ref, mod_ref, gffn_ref, x1_ref, hf_ref, mix_ref):
    t = pl.program_id(1)
    tm = z_ref.shape[0]
    z = z_ref[...]
    row = lax.broadcasted_iota(jnp.int32, z.shape, 0)
    z_before = jnp.where(t == 0, 0.0, zp_ref[0, 7:8, :])
    z_after = jnp.where(t == pl.num_programs(1) - 1, 0.0, zn_ref[0, 0:1, :])
    z_prev = jnp.where(row == 0, z_before, pltpu.roll(z, 1, 0))
    z_next = jnp.where(row == tm - 1, z_after, pltpu.roll(z, tm - 1, 0))
    conv = cb_ref[...] * (z_prev * cw_ref[0:1, :] + z * cw_ref[1:2, :] + z_next * cw_ref[2:3, :]
                          + cbias_ref[...])

    att = att_ref[...]
    n_att = att.shape[1] // HEAD
    for g in range(D_MODEL // HEAD):
        blk = att[:, g * HEAD:(g + 1) * HEAD] if g < n_att else conv[:, (g - n_att) * HEAD:(g - n_att + 1) * HEAD]
        mix_ref[:, g * HEAD:(g + 1) * HEAD] = _rms(blk, gmo_ref[:, g * HEAD:(g + 1) * HEAD]).astype(BF16)

    y = _dot(mix_ref[...], wout_ref[...])
    gt_a = mod_ref[0, 2:3, :]
    x1 = x_ref[...] + gt_a * y
    x1_ref[...] = x1
    hf_ref[...] = (_rms(x1, gffn_ref[...]) * (1.0 + mod_ref[0, 4:5, :]) + mod_ref[0, 3:4, :]).astype(BF16)


def _out(att2, cb, z, cw, cbias, gmo, wout, x2, mod_l, gffn, batch, seq):
    tm = TM_OUT
    tpb = seq // tm
    n8 = batch * seq // 8
    z3 = z.reshape(n8, 8, CONV_W)
    const = lambda b, t: (0, 0)
    tok = lambda b, t: (b * tpb + t, 0)
    return pl.pallas_call(
        _out_kernel,
        grid=(batch, tpb),
        in_specs=[
            pl.BlockSpec((tm, att2.shape[1]), tok),
            pl.BlockSpec((tm, CONV_W), tok),
            pl.BlockSpec((tm, CONV_W), tok),
            pl.BlockSpec((1, 8, CONV_W), lambda b, t: (jnp.maximum((b * tpb + t) * (tm // 8) - 1, 0), 0, 0)),
            pl.BlockSpec((1, 8, CONV_W), lambda b, t: (jnp.minimum((b * tpb + t + 1) * (tm // 8), n8 - 1), 0, 0)),
            pl.BlockSpec((3, CONV_W), const),
            pl.BlockSpec((1, CONV_W), const),
            pl.BlockSpec((1, D_MODEL), const),
            pl.BlockSpec(wout.shape, const, pipeline_mode=pl.Buffered(1)),
            pl.BlockSpec((tm, D_MODEL), tok),
            pl.BlockSpec((1, 6, D_MODEL), lambda b, t: (b, 0, 0)),
            pl.BlockSpec((1, D_MODEL), const),
        ],
        out_specs=[
            pl.BlockSpec((tm, D_MODEL), tok),
            pl.BlockSpec((tm, D_MODEL), tok),
        ],
        out_shape=[
            jax.ShapeDtypeStruct((batch * seq, D_MODEL), F32),
            jax.ShapeDtypeStruct((batch * seq, D_MODEL), BF16),
        ],
        scratch_shapes=[pltpu.VMEM((tm, D_MODEL), BF16)],
        compiler_params=pltpu.CompilerParams(
            dimension_semantics=("parallel", "parallel"), vmem_limit_bytes=VMEM_LIMIT),
        name="out_proj",
    )(att2, cb, z, z3, z3, cw, cbias, gmo, wout, x2, mod_l, gffn)


def _ffn_kernel(h_ref, hp_ref, hn_ref, wup_ref, cw_ref, cbias_ref, wdn_ref, x1_ref, mod_ref, gfin_ref,
                o_ref, hbuf_ref, u_ref, act_ref):
    t = pl.program_id(1)
    f = pl.program_id(2)
    tm = h_ref.shape[0]
    rows = tm + 2 * HALO

    @pl.when(f == 0)
    def _():
        hbuf_ref[HALO:HALO + tm, :] = h_ref[...]
        hbuf_ref[0:HALO, :] = jnp.where(t == 0, jnp.zeros_like(hp_ref[0]), hp_ref[0])
        hbuf_ref[HALO + tm:, :] = jnp.where(t == pl.num_programs(1) - 1, jnp.zeros_like(hn_ref[0]), hn_ref[0])
        o_ref[...] = jnp.zeros_like(o_ref)

    hb = hbuf_ref[...]
    for c in range(FFN_SUBS):
        u_ref[c] = _dot(hb, wup_ref[c])
    for c in range(FFN_SUBS):
        for j in range(BPS):
            ys = []
            for cols in (slice(2 * j * LANES, (2 * j + 1) * LANES), slice((2 * j + 1) * LANES, (2 * j + 2) * LANES)):
                ys.append(u_ref[c, pl.ds(HALO - 1, tm), cols] * cw_ref[c, 0:1, cols]
                          + u_ref[c, pl.ds(HALO, tm), cols] * cw_ref[c, 1:2, cols]
                          + u_ref[c, pl.ds(HALO + 1, tm), cols] * cw_ref[c, 2:3, cols] + cbias_ref[c, :, cols])
            act_ref[:, c * TF_SUB + j * LANES:c * TF_SUB + (j + 1) * LANES] = (ys[0] * _silu(ys[1])).astype(BF16)
    o_ref[...] += _dot(act_ref[...], wdn_ref[...])

    @pl.when(f == pl.num_programs(2) - 1)
    def _():
        y = x1_ref[...] + mod_ref[0, 5:6, :] * o_ref[...]
        o_ref[...] = _rms(y, gfin_ref[...])


def _ffn(hf, wup_t, cw_t, cbias_t, wdn, x1, mod_l, gfin, batch, seq):
    tm = TM_FFN
    tpb = seq // tm
    nf = N_SUB // FFN_SUBS
    tf = FFN_SUBS * TF_SUB
    n_h = batch * seq // HALO
    hf3 = hf.reshape(n_h, HALO, D_MODEL)
    tok = lambda b, t, f: (b * tpb + t, 0)
    return pl.pallas_call(
        _ffn_kernel,
        grid=(batch, tpb, nf),
        in_specs=[
            pl.BlockSpec((tm, D_MODEL), tok),
            pl.BlockSpec((1, HALO, D_MODEL),
                         lambda b, t, f: (jnp.maximum((b * tpb + t) * (tm // HALO) - 1, 0), 0, 0)),
            pl.BlockSpec((1, HALO, D_MODEL),
                         lambda b, t, f: (jnp.minimum((b * tpb + t + 1) * (tm // HALO), n_h - 1), 0, 0)),
            pl.BlockSpec((FFN_SUBS, D_MODEL, 2 * TF_SUB), lambda b, t, f: (f, 0, 0)),
            pl.BlockSpec((FFN_SUBS, 3, 2 * TF_SUB), lambda b, t, f: (f, 0, 0)),
            pl.BlockSpec((FFN_SUBS, 1, 2 * TF_SUB), lambda b, t, f: (f, 0, 0)),
            pl.BlockSpec((tf, D_MODEL), lambda b, t, f: (f, 0)),
            pl.BlockSpec((tm, D_MODEL), tok),
            pl.BlockSpec((1, 6, D_MODEL), lambda b, t, f: (b, 0, 0)),
            pl.BlockSpec((1, D_MODEL), lambda b, t, f: (0, 0)),
        ],
        out_specs=pl.BlockSpec((tm, D_MODEL), tok),
        out_shape=jax.ShapeDtypeStruct((batch * seq, D_MODEL), F32),
        scratch_shapes=[
            pltpu.VMEM((tm + 2 * HALO, D_MODEL), BF16),
            pltpu.VMEM((FFN_SUBS, tm + 2 * HALO, 2 * TF_SUB), F32),
            pltpu.VMEM((tm, tf), BF16),
        ],
        compiler_params=pltpu.CompilerParams(
            dimension_semantics=("parallel", "parallel", "arbitrary"), vmem_limit_bytes=VMEM_LIMIT),
        name="ffn",
    )(hf, hf3, hf3, wup_t, cw_t, cbias_t, wdn, x1, mod_l, gfin)


def _rope_tables(seq):
    rows = seq // GRID_W
    row = jnp.repeat(jnp.arange(rows), GRID_W).astype(F32)
    col = jnp.tile(jnp.arange(GRID_W), rows).astype(F32)
    axis_dim = QK_ROPE // 2
    inv = ROPE_BASE ** (-jnp.arange(0, axis_dim, 2, dtype=F32) / axis_dim)
    ang = jnp.concatenate([row[:, None] * inv, col[:, None] * inv], axis=-1)
    cos, sin = jnp.cos(ang), jnp.sin(ang)
    c = jnp.repeat(cos, 2, axis=-1)
    s = jnp.stack([-sin, sin], axis=-1).reshape(seq, QK_ROPE)
    return jnp.tile(c, (1, N_HEADS)), jnp.tile(s, (1, N_HEADS))


def _tile_ffn_cols(a):
    pad = ((0, 0), (0, FFN_PAD - FFN_DIM))
    val = jnp.pad(a[:, :FFN_DIM], pad).reshape(a.shape[0], N_SUB, BPS, 1, LANES)
    gate = jnp.pad(a[:, FFN_DIM:], pad).reshape(a.shape[0], N_SUB, BPS, 1, LANES)
    both = jnp.concatenate([val, gate], axis=3).reshape(a.shape[0], N_SUB, 2 * TF_SUB)
    return jnp.moveaxis(both, 1, 0)


def _blk_index(s, j):
    return jnp.minimum(s * BPS + j, FFN_BLKS - 1)


def _prep_up_kernel(*refs):
    o_ref = refs[-1]
    s = pl.program_id(0)
    for i in range(2 * BPS):
        real = s * BPS + i // 2 < FFN_BLKS
        o_ref[0, :, i * LANES:(i + 1) * LANES] = jnp.where(real, refs[i][...], 0.0).astype(BF16)


def _prep_up(w_up):
    blk = lambda fn: pl.BlockSpec((D_MODEL, LANES), fn)
    specs = []
    for j in range(BPS):
        specs.append(blk(functools.partial(lambda s, j: (0, _blk_index(s, j)), j=j)))
        specs.append(blk(functools.partial(lambda s, j: (0, FFN_BLKS + _blk_index(s, j)), j=j)))
    return pl.pallas_call(
        _prep_up_kernel,
        grid=(N_SUB,),
        in_specs=specs,
        out_specs=pl.BlockSpec((1, D_MODEL, 2 * TF_SUB), lambda s: (s, 0, 0)),
        out_shape=jax.ShapeDtypeStruct((N_SUB, D_MODEL, 2 * TF_SUB), BF16),
        compiler_params=pltpu.CompilerParams(
            dimension_semantics=("parallel",), vmem_limit_bytes=VMEM_LIMIT),
        name="prep_up",
    )(*([w_up] * (2 * BPS)))


def _prep_down_kernel(*refs):
    o_ref = refs[-1]
    s = pl.program_id(0)
    for j in range(BPS):
        real = s * BPS + j < FFN_BLKS
        o_ref[j * LANES:(j + 1) * LANES, :] = jnp.where(real, refs[j][...], 0.0).astype(BF16)


def _prep_down(w_down):
    return pl.pallas_call(
        _prep_down_kernel,
        grid=(N_SUB,),
        in_specs=[pl.BlockSpec((LANES, D_MODEL), functools.partial(lambda s, j: (_blk_index(s, j), 0), j=j))
                  for j in range(BPS)],
        out_specs=pl.BlockSpec((TF_SUB, D_MODEL), lambda s: (s, 0)),
        out_shape=jax.ShapeDtypeStruct((FFN_PAD, D_MODEL), BF16),
        compiler_params=pltpu.CompilerParams(
            dimension_semantics=("parallel",), vmem_limit_bytes=VMEM_LIMIT),
        name="prep_down",
    )(*([w_down] * BPS))


def _cast_kernel(w_ref, o_ref):
    o_ref[...] = w_ref[...].astype(BF16)


def _prep_qkv(w_in):
    n = Q_RANK + KV_RANK + LANES
    return pl.pallas_call(
        _cast_kernel,
        grid=(1,),
        in_specs=[pl.BlockSpec((D_MODEL, n), lambda i: (0, 0))],
        out_specs=pl.BlockSpec((D_MODEL, n), lambda i: (0, 0)),
        out_shape=jax.ShapeDtypeStruct((D_MODEL, n), BF16),
        compiler_params=pltpu.CompilerParams(vmem_limit_bytes=VMEM_LIMIT),
        name="prep_qkv",
    )(w_in)


def _prep_conv_kernel(lo_ref, hi_ref, o_ref):
    half = LANES // 2
    o_ref[...] = jnp.concatenate([lo_ref[:, half:], hi_ref[:, :half]], axis=1).astype(BF16)


def _prep_conv(w_in):
    first = (Q_RANK + KV_RANK + QK_ROPE) // LANES
    return pl.pallas_call(
        _prep_conv_kernel,
        grid=(3 * CONV_W // LANES,),
        in_specs=[
            pl.BlockSpec((D_MODEL, LANES), lambda j: (0, first + j)),
            pl.BlockSpec((D_MODEL, LANES), lambda j: (0, first + j + 1)),
        ],
        out_specs=pl.BlockSpec((D_MODEL, LANES), lambda j: (0, j)),
        out_shape=jax.ShapeDtypeStruct((D_MODEL, 3 * CONV_W), BF16),
        compiler_params=pltpu.CompilerParams(
            dimension_semantics=("parallel",), vmem_limit_bytes=VMEM_LIMIT),
        name="prep_conv",
    )(w_in, w_in)


def kernel(x, c, ctx, c_ctx, w_ada, b_ada, g_mix_norm, w_in, g_q_a, w_q_b, g_kv_a, w_kv_b, conv_w, conv_b, g_mix_out, w_out, g_ffn_norm, w_up, ffn_conv_w, ffn_conv_b, w_down, g_final):
    batch, seq, d = x.shape
    assert d == D_MODEL and w_ada.shape[0] == 1
    x2 = x.reshape(batch * seq, d)

    c8 = jnp.concatenate([c, c_ctx[None], jnp.zeros((8 - batch - 1, d), F32)], axis=0)
    mod = _ada(c8, w_ada[0], b_ada[0][None])
    mod_l = mod[:batch].reshape(batch, 6, d)
    mod_c = mod[batch:batch + 1].reshape(1, 6, d)

    wqkv = _prep_qkv(w_in[0])
    wconv = _prep_conv(w_in[0])
    wqb3 = w_q_b[0].reshape(Q_RANK, N_HEADS, QK_DIM)
    wqb = jnp.concatenate([wqb3[:, :, :QK_NOPE].reshape(Q_RANK, -1),
                           wqb3[:, :, QK_NOPE:].reshape(Q_RANK, -1)], axis=-1).astype(BF16)
    wkvb = w_kv_b[0].astype(BF16)
    wout = w_out[0].astype(BF16)
    wup_t = _prep_up(w_up[0])
    cw_t = _tile_ffn_cols(ffn_conv_w[0])
    cbias_t = _tile_ffn_cols(ffn_conv_b)
    wdn = _prep_down(w_down[0])

    cq, sq = _rope_tables(seq)

    q, k_l, v_l, cb, z = _in_lat(x2, mod_l, g_mix_norm, wqkv, wconv, g_q_a, wqb, g_kv_a, wkvb,
                                 cq, sq, batch, seq)
    k_c, v_c = _in_ctx(ctx, mod_c, g_mix_norm, wqkv, g_kv_a, wkvb)
    att = _attn(q, k_l, v_l, k_c, v_c)
    x1, hf = _out(att.reshape(batch * seq, N_HEADS * HEAD), cb, z, conv_w[0], conv_b, g_mix_out,
                  wout, x2, mod_l, g_ffn_norm, batch, seq)
    y = _ffn(hf, wup_t, cw_t, cbias_t, wdn, x1, mod_l, g_final[None], batch, seq)
    return y.reshape(batch, seq, d)
```

```python
import functools

import jax
import jax.numpy as jnp
from jax import lax
from jax.experimental import pallas as pl
from jax.experimental.pallas import tpu as pltpu

F32 = jnp.float32
BF16 = jnp.bfloat16

D_MODEL = 2048
N_HEADS = 8
HEAD = 128
QK_NOPE = 128
QK_ROPE = 64
QK_DIM = QK_NOPE + QK_ROPE
Q_RANK = 512
KV_RANK = 256
CONV_W = 1024
GRID_W = 64
ROPE_BASE = 10000.0
EPS = 1e-6
LOG2E = 1.4426950408889634
FFN_DIM = 5504

V7X_VMEM_BYTES = 64 * 1024 * 1024
VMEM_LIMIT = V7X_VMEM_BYTES - 8 * 1024 * 1024

TM_IN = 256
TQ = 1024
TQ_SUB = 256
TM_OUT = 512
OUT_SUB = 256
TM_FFN = 512
LANES = 128
TF_SUB = 512
FFN_SUBS = 1
HALO = 16
BPS = TF_SUB // LANES
FFN_BLKS = FFN_DIM // LANES
N_SUB = (FFN_DIM + TF_SUB - 1) // TF_SUB
FFN_PAD = N_SUB * TF_SUB


def _rms(x, g):
    return x * lax.rsqrt(jnp.mean(x * x, axis=-1, keepdims=True) + EPS) * g


def _silu(x):
    return x / (1.0 + jnp.exp(-x))


def _dot(a, b):
    return jnp.dot(a, b, preferred_element_type=F32)


def _rope(x, c, s):
    w = x.shape[-1]
    lane = lax.broadcasted_iota(jnp.int32, x.shape, 1)
    from_right = pltpu.roll(x, w - 1, 1)
    from_left = pltpu.roll(x, 1, 1)
    partner = jnp.where(lane % 2 == 0, from_right, from_left)
    return x * c + partner * s


def _ada_kernel(c_ref, w_ref, b_ref, o_ref):
    s = _silu(c_ref[...]).astype(BF16)
    o_ref[...] = _dot(s, w_ref[...].astype(BF16)) + b_ref[...]


def _ada(c8, w, b):
    n = w.shape[1]
    tn = 1024
    return pl.pallas_call(
        _ada_kernel,
        grid=(n // tn,),
        in_specs=[
            pl.BlockSpec((8, D_MODEL), lambda j: (0, 0)),
            pl.BlockSpec((D_MODEL, tn), lambda j: (0, j)),
            pl.BlockSpec((1, tn), lambda j: (0, j)),
        ],
        out_specs=pl.BlockSpec((8, tn), lambda j: (0, j)),
        out_shape=jax.ShapeDtypeStruct((8, n), F32),
        compiler_params=pltpu.CompilerParams(
            dimension_semantics=("parallel",), vmem_limit_bytes=VMEM_LIMIT),
        name="ada",
    )(c8, w, b)


def _in_lat_kernel(x_ref, mod_ref, gmix_ref, wqkv_ref, wconv_ref, gq_ref, wqb_ref, gkv_ref,
                   wkvb_ref, cq_ref, sq_ref, q_ref, k_ref, v_ref, cb_ref, z_ref):
    sh = mod_ref[0, 0:1, :]
    sc = mod_ref[0, 1:2, :]
    h = (_rms(x_ref[...], gmix_ref[...]) * (1.0 + sc) + sh).astype(BF16)

    p2 = _dot(h, wconv_ref[...])
    cb_ref[...] = p2[:, :CONV_W]
    z_ref[...] = p2[:, CONV_W:2 * CONV_W] * p2[:, 2 * CONV_W:]

    p1 = _dot(h, wqkv_ref[...])
    qa = p1[:, :Q_RANK]
    kva = p1[:, Q_RANK:Q_RANK + KV_RANK]
    kr = p1[:, Q_RANK + KV_RANK:]

    cq = cq_ref[...]
    sq = sq_ref[...]
    scale = QK_DIM ** -0.5 * LOG2E
    q = _dot(_rms(qa, gq_ref[...]).astype(BF16), wqb_ref[...])
    qn = (q[:, :N_HEADS * QK_NOPE] * scale).astype(BF16)
    qr = (_rope(q[:, N_HEADS * QK_NOPE:], cq, sq) * scale).astype(BF16)

    kv = _dot(_rms(kva, gkv_ref[...]).astype(BF16), wkvb_ref[...]).astype(BF16)
    krr = _rope(kr, cq[:, :128], sq[:, :128])[:, :QK_ROPE].astype(BF16)
    for hd in range(N_HEADS):
        q_ref[0, hd, :, 0:QK_NOPE] = qn[:, hd * QK_NOPE:(hd + 1) * QK_NOPE]
        q_ref[0, hd, :, QK_NOPE:QK_DIM] = qr[:, hd * QK_ROPE:(hd + 1) * QK_ROPE]
        k_ref[0, hd, :, 0:QK_NOPE] = kv[:, hd * 2 * HEAD:hd * 2 * HEAD + QK_NOPE]
        k_ref[0, hd, :, QK_NOPE:QK_DIM] = krr
        v_ref[0, hd, :, :] = kv[:, hd * 2 * HEAD + QK_NOPE:(hd + 1) * 2 * HEAD]


def _in_lat(x2, mod_l, gmix, wqkv, wconv, gq, wqb, gkv, wkvb, cq, sq, batch, seq):
    tm = TM_IN
    tpb = seq // tm
    const = lambda b, t: (0, 0)
    one = pl.Buffered(1)
    tok = lambda b, t: (b * tpb + t, 0)
    hd4 = lambda b, t: (b, 0, t, 0)
    return pl.pallas_call(
        _in_lat_kernel,
        grid=(batch, tpb),
        in_specs=[
            pl.BlockSpec((tm, D_MODEL), tok),
            pl.BlockSpec((1, 6, D_MODEL), lambda b, t: (b, 0, 0)),
            pl.BlockSpec((1, D_MODEL), const),
            pl.BlockSpec(wqkv.shape, const, pipeline_mode=one),
            pl.BlockSpec(wconv.shape, const, pipeline_mode=one),
            pl.BlockSpec((1, Q_RANK), const),
            pl.BlockSpec(wqb.shape, const, pipeline_mode=one),
            pl.BlockSpec((1, KV_RANK), const),
            pl.BlockSpec(wkvb.shape, const, pipeline_mode=one),
            pl.BlockSpec((tm, N_HEADS * QK_ROPE), lambda b, t: (t, 0)),
            pl.BlockSpec((tm, N_HEADS * QK_ROPE), lambda b, t: (t, 0)),
        ],
        out_specs=[
            pl.BlockSpec((1, N_HEADS, tm, QK_DIM), hd4),
            pl.BlockSpec((1, N_HEADS, tm, QK_DIM), hd4),
            pl.BlockSpec((1, N_HEADS, tm, HEAD), hd4),
            pl.BlockSpec((tm, CONV_W), tok),
            pl.BlockSpec((tm, CONV_W), tok),
        ],
        out_shape=[
            jax.ShapeDtypeStruct((batch, N_HEADS, seq, QK_DIM), BF16),
            jax.ShapeDtypeStruct((batch, N_HEADS, seq, QK_DIM), BF16),
            jax.ShapeDtypeStruct((batch, N_HEADS, seq, HEAD), BF16),
            jax.ShapeDtypeStruct((batch * seq, CONV_W), F32),
            jax.ShapeDtypeStruct((batch * seq, CONV_W), F32),
        ],
        compiler_params=pltpu.CompilerParams(
            dimension_semantics=("parallel", "parallel"), vmem_limit_bytes=VMEM_LIMIT),
        name="in_lat",
    )(x2, mod_l, gmix, wqkv, wconv, gq, wqb, gkv, wkvb, cq, sq)


def _in_ctx_kernel(x_ref, mod_ref, gmix_ref, wkv_ref, gkv_ref, wkvb_ref, k_ref, v_ref):
    sh = mod_ref[0, 0:1, :]
    sc = mod_ref[0, 1:2, :]
    h = (_rms(x_ref[0], gmix_ref[...]) * (1.0 + sc) + sh).astype(BF16)
    p1 = _dot(h, wkv_ref[:, Q_RANK:])
    kva = p1[:, :KV_RANK]
    kr = p1[:, KV_RANK:KV_RANK + QK_ROPE].astype(BF16)
    kv = _dot(_rms(kva, gkv_ref[...]).astype(BF16), wkvb_ref[...]).astype(BF16)
    for hd in range(N_HEADS):
        k_ref[0, hd, :, 0:QK_NOPE] = kv[:, hd * 2 * HEAD:hd * 2 * HEAD + QK_NOPE]
        k_ref[0, hd, :, QK_NOPE:QK_DIM] = kr
        v_ref[0, hd, :, :] = kv[:, hd * 2 * HEAD + QK_NOPE:(hd + 1) * 2 * HEAD]


def _in_ctx(ctx, mod_c, gmix, wkv, gkv, wkvb):
    batch, n_ctx, _ = ctx.shape
    const = lambda b: (0, 0)
    return pl.pallas_call(
        _in_ctx_kernel,
        grid=(batch,),
        in_specs=[
            pl.BlockSpec((1, n_ctx, D_MODEL), lambda b: (b, 0, 0)),
            pl.BlockSpec((1, 6, D_MODEL), lambda b: (0, 0, 0)),
            pl.BlockSpec((1, D_MODEL), const),
            pl.BlockSpec(wkv.shape, const),
            pl.BlockSpec((1, KV_RANK), const),
            pl.BlockSpec(wkvb.shape, const),
        ],
        out_specs=[
            pl.BlockSpec((1, N_HEADS, n_ctx, QK_DIM), lambda b: (b, 0, 0, 0)),
            pl.BlockSpec((1, N_HEADS, n_ctx, HEAD), lambda b: (b, 0, 0, 0)),
        ],
        out_shape=[
            jax.ShapeDtypeStruct((batch, N_HEADS, n_ctx, QK_DIM), BF16),
            jax.ShapeDtypeStruct((batch, N_HEADS, n_ctx, HEAD), BF16),
        ],
        compiler_params=pltpu.CompilerParams(
            dimension_semantics=("parallel",), vmem_limit_bytes=VMEM_LIMIT),
        name="in_ctx",
    )(ctx, mod_c, gmix, wkv, gkv, wkvb)


def _attn_kernel(q_ref, kl_ref, vl_ref, kc_ref, vc_ref, o_ref):
    nt = (((1,), (1,)), ((), ()))
    for j in range(TQ // TQ_SUB):
        rows = slice(j * TQ_SUB, (j + 1) * TQ_SUB)
        q = q_ref[0, 0, rows, :]
        s_l = lax.dot_general(q, kl_ref[0, 0], nt, preferred_element_type=F32)
        s_c = lax.dot_general(q, kc_ref[0, 0], nt, preferred_element_type=F32)
        m = jnp.maximum(jnp.max(s_l, axis=-1, keepdims=True), jnp.max(s_c, axis=-1, keepdims=True))
        p_l = jnp.exp2(s_l - m)
        p_c = jnp.exp2(s_c - m)
        denom = jnp.sum(p_l, axis=-1, keepdims=True) + jnp.sum(p_c, axis=-1, keepdims=True)
        o = _dot(p_l.astype(BF16), vl_ref[0, 0]) + _dot(p_c.astype(BF16), vc_ref[0, 0])
        o_ref[0, rows, :] = o / denom


def _attn(q, k_l, v_l, k_c, v_c):
    batch, heads, seq, _ = q.shape
    n_ctx = k_c.shape[2]
    kv_map = lambda b, h, i: (b, h, 0, 0)
    return pl.pallas_call(
        _attn_kernel,
        grid=(batch, heads, seq // TQ),
        in_specs=[
            pl.BlockSpec((1, 1, TQ, QK_DIM), lambda b, h, i: (b, h, i, 0)),
            pl.BlockSpec((1, 1, seq, QK_DIM), kv_map),
            pl.BlockSpec((1, 1, seq, HEAD), kv_map),
            pl.BlockSpec((1, 1, n_ctx, QK_DIM), kv_map),
            pl.BlockSpec((1, 1, n_ctx, HEAD), kv_map),
        ],
        out_specs=pl.BlockSpec((1, TQ, HEAD), lambda b, h, i: (b, i, h)),
        out_shape=jax.ShapeDtypeStruct((batch, seq, heads * HEAD), F32),
        compiler_params=pltpu.CompilerParams(
            dimension_semantics=("parallel", "parallel", "parallel"),
            vmem_limit_bytes=VMEM_LIMIT),
        name="attn",
    )(q, k_l, v_l, k_c, v_c)


def _out_kernel(att_ref, cb_ref, z_ref, zp_ref, zn_ref, cw_ref, cbias_ref, gmo_ref, wout_ref,
                x_ref, mod_ref, gffn_ref, x1_ref, hf_ref, mix_ref):
    t = pl.program_id(1)
    tm = z_ref.shape[0]
    z = z_ref[...]
    row = lax.broadcasted_iota(jnp.int32, z.shape, 0)
    z_before = jnp.where(t == 0, 0.0, zp_ref[0, 7:8, :])
    z_after = jnp.where(t == pl.num_programs(1) - 1, 0.0, zn_ref[0, 0:1, :])
    z_prev = jnp.where(row == 0, z_before, pltpu.roll(z, 1, 0))
    z_next = jnp.where(row == tm - 1, z_after, pltpu.roll(z, tm - 1, 0))
    conv = cb_ref[...] * (z_prev * cw_ref[0:1, :] + z * cw_ref[1:2, :] + z_next * cw_ref[2:3, :]
                          + cbias_ref[...])

    n_att = att_ref.shape[1] // HEAD
    gt_a = mod_ref[0, 2:3, :]
    ffn_gain = gffn_ref[...] * (1.0 + mod_ref[0, 4:5, :])
    ffn_shift = mod_ref[0, 3:4, :]
    for g in range(D_MODEL // HEAD):
        cols = slice(g * HEAD, (g + 1) * HEAD)
        blk = att_ref[:, cols] if g < n_att else conv[:, (g - n_att) * HEAD:(g - n_att + 1) * HEAD]
        mix_ref[:, cols] = _rms(blk, gmo_ref[:, cols]).astype(BF16)
    for s in range(tm // OUT_SUB):
        rows = slice(s * OUT_SUB, (s + 1) * OUT_SUB)
        x1 = x_ref[rows, :] + gt_a * _dot(mix_ref[rows, :], wout_ref[...])
        x1_ref[rows, :] = x1
        hf_ref[rows, :] = (_rms(x1, ffn_gain) + ffn_shift).astype(BF16)


def _out(att2, cb, z, cw, cbias, gmo, wout, x2, mod_l, gffn, batch, seq):
    tm = TM_OUT
    tpb = seq // tm
    n8 = batch * seq // 8
    z3 = z.reshape(n8, 8, CONV_W)
    const = lambda b, t: (0, 0)
    tok = lambda b, t: (b * tpb + t, 0)
    return pl.pallas_call(
        _out_kernel,
        grid=(batch, tpb),
        in_specs=[
            pl.BlockSpec((tm, att2.shape[1]), tok),
            pl.BlockSpec((tm, CONV_W), tok),
            pl.BlockSpec((tm, CONV_W), tok),
            pl.BlockSpec((1, 8, CONV_W), lambda b, t: (jnp.maximum((b * tpb + t) * (tm // 8) - 1, 0), 0, 0)),
            pl.BlockSpec((1, 8, CONV_W), lambda b, t: (jnp.minimum((b * tpb + t + 1) * (tm // 8), n8 - 1), 0, 0)),
            pl.BlockSpec((3, CONV_W), const),
            pl.BlockSpec((1, CONV_W), const),
            pl.BlockSpec((1, D_MODEL), const),
            pl.BlockSpec(wout.shape, const, pipeline_mode=pl.Buffered(1)),
            pl.BlockSpec((tm, D_MODEL), tok),
            pl.BlockSpec((1, 6, D_MODEL), lambda b, t: (b, 0, 0)),
            pl.BlockSpec((1, D_MODEL), const),
        ],
        out_specs=[
            pl.BlockSpec((tm, D_MODEL), tok),
            pl.BlockSpec((tm, D_MODEL), tok),
        ],
        out_shape=[
            jax.ShapeDtypeStruct((batch * seq, D_MODEL), F32),
            jax.ShapeDtypeStruct((batch * seq, D_MODEL), BF16),
        ],
        scratch_shapes=[pltpu.VMEM((tm, D_MODEL), BF16)],
        compiler_params=pltpu.CompilerParams(
            dimension_semantics=("parallel", "parallel"), vmem_limit_bytes=VMEM_LIMIT),
        name="out_proj",
    )(att2, cb, z, z3, z3, cw, cbias, gmo, wout, x2, mod_l, gffn)


def _ffn_kernel(h_ref, hp_ref, hn_ref, wup_ref, cw_ref, cbias_ref, wdn_ref, x1_ref, mod_ref, gfin_ref,
                o_ref, hbuf_ref, u_ref):
    t = pl.program_id(1)
    f = pl.program_id(2)
    tm = h_ref.shape[0]

    @pl.when(f == 0)
    def _():
        hbuf_ref[HALO:HALO + tm, :] = h_ref[...]
        hbuf_ref[0:HALO, :] = jnp.where(t == 0, jnp.zeros_like(hp_ref[0]), hp_ref[0])
        hbuf_ref[HALO + tm:, :] = jnp.where(t == pl.num_programs(1) - 1, jnp.zeros_like(hn_ref[0]), hn_ref[0])
        o_ref[...] = jnp.zeros_like(o_ref)

    u_ref[...] = _dot(hbuf_ref[...], wup_ref[0])
    y = (u_ref[pl.ds(HALO - 1, tm), :] * cw_ref[0, 0:1, :] + u_ref[pl.ds(HALO, tm), :] * cw_ref[0, 1:2, :]
         + u_ref[pl.ds(HALO + 1, tm), :] * cw_ref[0, 2:3, :] + cbias_ref[0])
    act = (y[:, :TF_SUB] * _silu(y[:, TF_SUB:])).astype(BF16)
    o_ref[...] += _dot(act, wdn_ref[...])

    @pl.when(f == pl.num_programs(2) - 1)
    def _():
        y = x1_ref[...] + mod_ref[0, 5:6, :] * o_ref[...]
        o_ref[...] = _rms(y, gfin_ref[...])


def _ffn(hf, wup_t, cw_t, cbias_t, wdn, x1, mod_l, gfin, batch, seq):
    tm = TM_FFN
    tpb = seq // tm
    nf = N_SUB // FFN_SUBS
    tf = FFN_SUBS * TF_SUB
    n_h = batch * seq // HALO
    hf3 = hf.reshape(n_h, HALO, D_MODEL)
    tok = lambda b, t, f: (b * tpb + t, 0)
    return pl.pallas_call(
        _ffn_kernel,
        grid=(batch, tpb, nf),
        in_specs=[
            pl.BlockSpec((tm, D_MODEL), tok),
            pl.BlockSpec((1, HALO, D_MODEL),
                         lambda b, t, f: (jnp.maximum((b * tpb + t) * (tm // HALO) - 1, 0), 0, 0)),
            pl.BlockSpec((1, HALO, D_MODEL),
                         lambda b, t, f: (jnp.minimum((b * tpb + t + 1) * (tm // HALO), n_h - 1), 0, 0)),
            pl.BlockSpec((FFN_SUBS, D_MODEL, 2 * TF_SUB), lambda b, t, f: (f, 0, 0)),
            pl.BlockSpec((FFN_SUBS, 3, 2 * TF_SUB), lambda b, t, f: (f, 0, 0)),
            pl.BlockSpec((FFN_SUBS, 1, 2 * TF_SUB), lambda b, t, f: (f, 0, 0)),
            pl.BlockSpec((tf, D_MODEL), lambda b, t, f: (f, 0)),
            pl.BlockSpec((tm, D_MODEL), tok),
            pl.BlockSpec((1, 6, D_MODEL), lambda b, t, f: (b, 0, 0)),
            pl.BlockSpec((1, D_MODEL), lambda b, t, f: (0, 0)),
        ],
        out_specs=pl.BlockSpec((tm, D_MODEL), tok),
        out_shape=jax.ShapeDtypeStruct((batch * seq, D_MODEL), F32),
        scratch_shapes=[
            pltpu.VMEM((tm + 2 * HALO, D_MODEL), BF16),
            pltpu.VMEM((tm + 2 * HALO, 2 * TF_SUB), F32),
        ],
        compiler_params=pltpu.CompilerParams(
            dimension_semantics=("parallel", "parallel", "arbitrary"), vmem_limit_bytes=VMEM_LIMIT),
        name="ffn",
    )(hf, hf3, hf3, wup_t, cw_t, cbias_t, wdn, x1, mod_l, gfin)


def _rope_tables(seq):
    rows = seq // GRID_W
    row = jnp.repeat(jnp.arange(rows), GRID_W).astype(F32)
    col = jnp.tile(jnp.arange(GRID_W), rows).astype(F32)
    axis_dim = QK_ROPE // 2
    inv = ROPE_BASE ** (-jnp.arange(0, axis_dim, 2, dtype=F32) / axis_dim)
    ang = jnp.concatenate([row[:, None] * inv, col[:, None] * inv], axis=-1)
    cos, sin = jnp.cos(ang), jnp.sin(ang)
    c = jnp.repeat(cos, 2, axis=-1)
    s = jnp.stack([-sin, sin], axis=-1).reshape(seq, QK_ROPE)
    return jnp.tile(c, (1, N_HEADS)), jnp.tile(s, (1, N_HEADS))


def _tile_ffn_cols(a):
    pad = ((0, 0), (0, FFN_PAD - FFN_DIM))
    val = jnp.pad(a[:, :FFN_DIM], pad).reshape(a.shape[0], N_SUB, TF_SUB)
    gate = jnp.pad(a[:, FFN_DIM:], pad).reshape(a.shape[0], N_SUB, TF_SUB)
    return jnp.moveaxis(jnp.concatenate([val, gate], axis=-1), 1, 0)


def _blk_index(s, j):
    return jnp.minimum(s * BPS + j, FFN_BLKS - 1)


def _prep_up_kernel(*refs):
    o_ref = refs[-1]
    s = pl.program_id(0)
    for i in range(2 * BPS):
        real = s * BPS + i % BPS < FFN_BLKS
        o_ref[0, :, i * LANES:(i + 1) * LANES] = jnp.where(real, refs[i][...], 0.0).astype(BF16)


def _prep_up(w_up):
    blk = lambda fn: pl.BlockSpec((D_MODEL, LANES), fn)
    val = [blk(functools.partial(lambda s, j: (0, _blk_index(s, j)), j=j)) for j in range(BPS)]
    gate = [blk(functools.partial(lambda s, j: (0, FFN_BLKS + _blk_index(s, j)), j=j)) for j in range(BPS)]
    return pl.pallas_call(
        _prep_up_kernel,
        grid=(N_SUB,),
        in_specs=val + gate,
        out_specs=pl.BlockSpec((1, D_MODEL, 2 * TF_SUB), lambda s: (s, 0, 0)),
        out_shape=jax.ShapeDtypeStruct((N_SUB, D_MODEL, 2 * TF_SUB), BF16),
        compiler_params=pltpu.CompilerParams(
            dimension_semantics=("parallel",), vmem_limit_bytes=VMEM_LIMIT),
        name="prep_up",
    )(*([w_up] * (2 * BPS)))


def _prep_down_kernel(*refs):
    o_ref = refs[-1]
    s = pl.program_id(0)
    for j in range(BPS):
        real = s * BPS + j < FFN_BLKS
        o_ref[j * LANES:(j + 1) * LANES, :] = jnp.where(real, refs[j][...], 0.0).astype(BF16)


def _prep_down(w_down):
    return pl.pallas_call(
        _prep_down_kernel,
        grid=(N_SUB,),
        in_specs=[pl.BlockSpec((LANES, D_MODEL), functools.partial(lambda s, j: (_blk_index(s, j), 0), j=j))
                  for j in range(BPS)],
        out_specs=pl.BlockSpec((TF_SUB, D_MODEL), lambda s: (s, 0)),
        out_shape=jax.ShapeDtypeStruct((FFN_PAD, D_MODEL), BF16),
        compiler_params=pltpu.CompilerParams(
            dimension_semantics=("parallel",), vmem_limit_bytes=VMEM_LIMIT),
        name="prep_down",
    )(*([w_down] * BPS))


N_QKV = Q_RANK + KV_RANK + QK_ROPE
N_QKV_PAD = Q_RANK + KV_RANK + LANES


PREP_ROWS = 512


def _prep_qkv_kernel(w_ref, o_ref):
    o_ref[...] = w_ref[...].T.astype(BF16)


def _prep_conv_kernel(lo_ref, hi_ref, o_ref):
    off = N_QKV % PREP_ROWS
    o_ref[...] = jnp.concatenate([lo_ref[off:, :], hi_ref[:off, :]], axis=0).T.astype(BF16)


def _prep_in(w_in_t):
    wqkv = pl.pallas_call(
        _prep_qkv_kernel,
        grid=(N_QKV_PAD // LANES,),
        in_specs=[pl.BlockSpec((LANES, D_MODEL), lambda j: (j, 0))],
        out_specs=pl.BlockSpec((D_MODEL, LANES), lambda j: (0, j)),
        out_shape=jax.ShapeDtypeStruct((D_MODEL, N_QKV_PAD), BF16),
        compiler_params=pltpu.CompilerParams(
            dimension_semantics=("parallel",), vmem_limit_bytes=VMEM_LIMIT),
        name="prep_qkv",
    )(w_in_t)
    first = N_QKV // PREP_ROWS
    wconv = pl.pallas_call(
        _prep_conv_kernel,
        grid=(3 * CONV_W // PREP_ROWS,),
        in_specs=[
            pl.BlockSpec((PREP_ROWS, D_MODEL), lambda j: (first + j, 0)),
            pl.BlockSpec((PREP_ROWS, D_MODEL), lambda j: (first + j + 1, 0)),
        ],
        out_specs=pl.BlockSpec((D_MODEL, PREP_ROWS), lambda j: (0, j)),
        out_shape=jax.ShapeDtypeStruct((D_MODEL, 3 * CONV_W), BF16),
        compiler_params=pltpu.CompilerParams(
            dimension_semantics=("parallel",), vmem_limit_bytes=VMEM_LIMIT),
        name="prep_conv",
    )(w_in_t, w_in_t)
    return wqkv, wconv


def kernel(x, c, ctx, c_ctx, w_ada, b_ada, g_mix_norm, w_in, g_q_a, w_q_b, g_kv_a, w_kv_b, conv_w, conv_b, g_mix_out, w_out, g_ffn_norm, w_up, ffn_conv_w, ffn_conv_b, w_down, g_final):
    batch, seq, d = x.shape
    assert d == D_MODEL and w_ada.shape[0] == 1
    x2 = x.reshape(batch * seq, d)

    c8 = jnp.concatenate([c, c_ctx[None], jnp.zeros((8 - batch - 1, d), F32)], axis=0)
    mod = _ada(c8, w_ada[0], b_ada[0][None])
    mod_l = mod[:batch].reshape(batch, 6, d)
    mod_c = mod[batch:batch + 1].reshape(1, 6, d)

    wqkv, wconv = _prep_in(w_in[0].T)
    wqb3 = w_q_b[0].reshape(Q_RANK, N_HEADS, QK_DIM)
    wqb = jnp.concatenate([wqb3[:, :, :QK_NOPE].reshape(Q_RANK, -1),
                           wqb3[:, :, QK_NOPE:].reshape(Q_RANK, -1)], axis=-1).astype(BF16)
    wkvb = w_kv_b[0].astype(BF16)
    wout = w_out[0].astype(BF16)
    wup_t = _prep_up(w_up[0])
    cw_t = _tile_ffn_cols(ffn_conv_w[0])
    cbias_t = _tile_ffn_cols(ffn_conv_b)
    wdn = _prep_down(w_down[0])

    cq, sq = _rope_tables(seq)

    q, k_l, v_l, cb, z = _in_lat(x2, mod_l, g_mix_norm, wqkv, wconv, g_q_a, wqb, g_kv_a, wkvb,
                                 cq, sq, batch, seq)
    k_c, v_c = _in_ctx(ctx, mod_c, g_mix_norm, wqkv, g_kv_a, wkvb)
    att = _attn(q, k_l, v_l, k_c, v_c)
    x1, hf = _out(att.reshape(batch * seq, N_HEADS * HEAD), cb, z, conv_w[0], conv_b, g_mix_out,
                  wout, x2, mod_l, g_ffn_norm, batch, seq)
    y = _ffn(hf, wup_t, cw_t, cbias_t, wdn, x1, mod_l, g_final[None], batch, seq)
    return y.reshape(batch, seq, d)
```

```python
import functools

import jax
import jax.numpy as jnp
from jax import lax
from jax.experimental import pallas as pl
from jax.experimental.pallas import tpu as pltpu

F32 = jnp.float32
BF16 = jnp.bfloat16

D_MODEL = 2048
N_HEADS = 8
HEAD = 128
QK_NOPE = 128
QK_ROPE = 64
QK_DIM = QK_NOPE + QK_ROPE
Q_RANK = 512
KV_RANK = 256
CONV_W = 1024
GRID_W = 64
ROPE_BASE = 10000.0
EPS = 1e-6
LOG2E = 1.4426950408889634
FFN_DIM = 5504

V7X_VMEM_BYTES = 64 * 1024 * 1024
VMEM_LIMIT = V7X_VMEM_BYTES - 8 * 1024 * 1024

TM_IN = 512
TQ = 2048
TQ_SUB = 256
TM_OUT = 512
TM_FFN = 512
LANES = 128
TF_SUB = 512
FFN_SUBS = 1
HALO = 16
BPS = TF_SUB // LANES
FFN_BLKS = FFN_DIM // LANES
N_SUB = (FFN_DIM + TF_SUB - 1) // TF_SUB
FFN_PAD = N_SUB * TF_SUB


def _rms(x, g):
    return x * lax.rsqrt(jnp.mean(x * x, axis=-1, keepdims=True) + EPS) * g


def _silu(x):
    return x / (1.0 + jnp.exp(-x))


def _dot(a, b):
    return jnp.dot(a, b, preferred_element_type=F32)


def _rope(x, c, s):
    w = x.shape[-1]
    lane = lax.broadcasted_iota(jnp.int32, x.shape, 1)
    from_right = pltpu.roll(x, w - 1, 1)
    from_left = pltpu.roll(x, 1, 1)
    partner = jnp.where(lane % 2 == 0, from_right, from_left)
    return x * c + partner * s


def _ada_kernel(c_ref, w_ref, b_ref, o_ref):
    s = _silu(c_ref[...]).astype(BF16)
    o_ref[...] = _dot(s, w_ref[...].astype(BF16)) + b_ref[...]


def _ada(c8, w, b):
    n = w.shape[1]
    tn = 1024
    return pl.pallas_call(
        _ada_kernel,
        grid=(n // tn,),
        in_specs=[
            pl.BlockSpec((8, D_MODEL), lambda j: (0, 0)),
            pl.BlockSpec((D_MODEL, tn), lambda j: (0, j)),
            pl.BlockSpec((1, tn), lambda j: (0, j)),
        ],
        out_specs=pl.BlockSpec((8, tn), lambda j: (0, j)),
        out_shape=jax.ShapeDtypeStruct((8, n), F32),
        compiler_params=pltpu.CompilerParams(
            dimension_semantics=("parallel",), vmem_limit_bytes=VMEM_LIMIT),
        name="ada",
    )(c8, w, b)


def _in_lat_kernel(x_ref, mod_ref, gmix_ref, wqkv_ref, wconv_ref, gq_ref, wqb_ref, gkv_ref,
                   wkvb_ref, cq_ref, sq_ref, q_ref, k_ref, v_ref, cb_ref, z_ref):
    sh = mod_ref[0, 0:1, :]
    sc = mod_ref[0, 1:2, :]
    h = (_rms(x_ref[...], gmix_ref[...]) * (1.0 + sc) + sh).astype(BF16)

    p2 = _dot(h, wconv_ref[...])
    cb_ref[...] = p2[:, :CONV_W]
    z_ref[...] = p2[:, CONV_W:2 * CONV_W] * p2[:, 2 * CONV_W:]

    p1 = _dot(h, wqkv_ref[...])
    qa = p1[:, :Q_RANK]
    kva = p1[:, Q_RANK:Q_RANK + KV_RANK]
    kr = p1[:, Q_RANK + KV_RANK:]

    cq = cq_ref[...]
    sq = sq_ref[...]
    scale = QK_DIM ** -0.5 * LOG2E
    q = _dot(_rms(qa, gq_ref[...]).astype(BF16), wqb_ref[...])
    qn = (q[:, :N_HEADS * QK_NOPE] * scale).astype(BF16)
    qr = (_rope(q[:, N_HEADS * QK_NOPE:], cq, sq) * scale).astype(BF16)

    kv = _dot(_rms(kva, gkv_ref[...]).astype(BF16), wkvb_ref[...]).astype(BF16)
    krr = _rope(kr, cq[:, :128], sq[:, :128])[:, :QK_ROPE].astype(BF16)
    for hd in range(N_HEADS):
        q_ref[0, hd, :, 0:QK_NOPE] = qn[:, hd * QK_NOPE:(hd + 1) * QK_NOPE]
        q_ref[0, hd, :, QK_NOPE:QK_DIM] = qr[:, hd * QK_ROPE:(hd + 1) * QK_ROPE]
        k_ref[0, hd, :, 0:QK_NOPE] = kv[:, hd * 2 * HEAD:hd * 2 * HEAD + QK_NOPE]
        k_ref[0, hd, :, QK_NOPE:QK_DIM] = krr
        v_ref[0, hd, :, :] = kv[:, hd * 2 * HEAD + QK_NOPE:(hd + 1) * 2 * HEAD]


def _in_lat(x2, mod_l, gmix, wqkv, wconv, gq, wqb, gkv, wkvb, cq, sq, batch, seq):
    tm = TM_IN
    tpb = seq // tm
    const = lambda b, t: (0, 0)
    one = pl.Buffered(1)
    tok = lambda b, t: (b * tpb + t, 0)
    hd4 = lambda b, t: (b, 0, t, 0)
    return pl.pallas_call(
        _in_lat_kernel,
        grid=(batch, tpb),
        in_specs=[
            pl.BlockSpec((tm, D_MODEL), tok),
            pl.BlockSpec((1, 6, D_MODEL), lambda b, t: (b, 0, 0)),
            pl.BlockSpec((1, D_MODEL), const),
            pl.BlockSpec(wqkv.shape, const, pipeline_mode=one),
            pl.BlockSpec(wconv.shape, const, pipeline_mode=one),
            pl.BlockSpec((1, Q_RANK), const),
            pl.BlockSpec(wqb.shape, const, pipeline_mode=one),
            pl.BlockSpec((1, KV_RANK), const),
            pl.BlockSpec(wkvb.shape, const, pipeline_mode=one),
            pl.BlockSpec((tm, N_HEADS * QK_ROPE), lambda b, t: (t, 0)),
            pl.BlockSpec((tm, N_HEADS * QK_ROPE), lambda b, t: (t, 0)),
        ],
        out_specs=[
            pl.BlockSpec((1, N_HEADS, tm, QK_DIM), hd4),
            pl.BlockSpec((1, N_HEADS, tm, QK_DIM), hd4),
            pl.BlockSpec((1, N_HEADS, tm, HEAD), hd4),
            pl.BlockSpec((tm, CONV_W), tok),
            pl.BlockSpec((tm, CONV_W), tok),
        ],
        out_shape=[
            jax.ShapeDtypeStruct((batch, N_HEADS, seq, QK_DIM), BF16),
            jax.ShapeDtypeStruct((batch, N_HEADS, seq, QK_DIM), BF16),
            jax.ShapeDtypeStruct((batch, N_HEADS, seq, HEAD), BF16),
            jax.ShapeDtypeStruct((batch * seq, CONV_W), F32),
            jax.ShapeDtypeStruct((batch * seq, CONV_W), F32),
        ],
        compiler_params=pltpu.CompilerParams(
            dimension_semantics=("parallel", "parallel"), vmem_limit_bytes=VMEM_LIMIT),
        name="in_lat",
    )(x2, mod_l, gmix, wqkv, wconv, gq, wqb, gkv, wkvb, cq, sq)


def _in_ctx_kernel(x_ref, mod_ref, gmix_ref, wkv_ref, gkv_ref, wkvb_ref, k_ref, v_ref):
    sh = mod_ref[0, 0:1, :]
    sc = mod_ref[0, 1:2, :]
    h = (_rms(x_ref[0], gmix_ref[...]) * (1.0 + sc) + sh).astype(BF16)
    p1 = _dot(h, wkv_ref[:, Q_RANK:])
    kva = p1[:, :KV_RANK]
    kr = p1[:, KV_RANK:KV_RANK + QK_ROPE].astype(BF16)
    kv = _dot(_rms(kva, gkv_ref[...]).astype(BF16), wkvb_ref[...]).astype(BF16)
    for hd in range(N_HEADS):
        k_ref[0, hd, :, 0:QK_NOPE] = kv[:, hd * 2 * HEAD:hd * 2 * HEAD + QK_NOPE]
        k_ref[0, hd, :, QK_NOPE:QK_DIM] = kr
        v_ref[0, hd, :, :] = kv[:, hd * 2 * HEAD + QK_NOPE:(hd + 1) * 2 * HEAD]


def _in_ctx(ctx, mod_c, gmix, wkv, gkv, wkvb):
    batch, n_ctx, _ = ctx.shape
    const = lambda b: (0, 0)
    return pl.pallas_call(
        _in_ctx_kernel,
        grid=(batch,),
        in_specs=[
            pl.BlockSpec((1, n_ctx, D_MODEL), lambda b: (b, 0, 0)),
            pl.BlockSpec((1, 6, D_MODEL), lambda b: (0, 0, 0)),
            pl.BlockSpec((1, D_MODEL), const),
            pl.BlockSpec(wkv.shape, const),
            pl.BlockSpec((1, KV_RANK), const),
            pl.BlockSpec(wkvb.shape, const),
        ],
        out_specs=[
            pl.BlockSpec((1, N_HEADS, n_ctx, QK_DIM), lambda b: (b, 0, 0, 0)),
            pl.BlockSpec((1, N_HEADS, n_ctx, HEAD), lambda b: (b, 0, 0, 0)),
        ],
        out_shape=[
            jax.ShapeDtypeStruct((batch, N_HEADS, n_ctx, QK_DIM), BF16),
            jax.ShapeDtypeStruct((batch, N_HEADS, n_ctx, HEAD), BF16),
        ],
        compiler_params=pltpu.CompilerParams(
            dimension_semantics=("parallel",), vmem_limit_bytes=VMEM_LIMIT),
        name="in_ctx",
    )(ctx, mod_c, gmix, wkv, gkv, wkvb)


def _attn_kernel(q_ref, kl_ref, vl_ref, kc_ref, vc_ref, o_ref):
    nt = (((1,), (1,)), ((), ()))
    for j in range(TQ // TQ_SUB):
        rows = slice(j * TQ_SUB, (j + 1) * TQ_SUB)
        q = q_ref[0, 0, rows, :]
        s_l = lax.dot_general(q, kl_ref[0, 0], nt, preferred_element_type=F32)
        s_c = lax.dot_general(q, kc_ref[0, 0], nt, preferred_element_type=F32)
        m = jnp.maximum(jnp.max(s_l, axis=-1, keepdims=True), jnp.max(s_c, axis=-1, keepdims=True))
        p_l = jnp.exp2(s_l - m)
        p_c = jnp.exp2(s_c - m)
        denom = jnp.sum(p_l, axis=-1, keepdims=True) + jnp.sum(p_c, axis=-1, keepdims=True)
        o = _dot(p_l.astype(BF16), vl_ref[0, 0]) + _dot(p_c.astype(BF16), vc_ref[0, 0])
        o_ref[0, rows, :] = o / denom


def _attn(q, k_l, v_l, k_c, v_c):
    batch, heads, seq, _ = q.shape
    n_ctx = k_c.shape[2]
    kv_map = lambda b, h, i: (b, h, 0, 0)
    return pl.pallas_call(
        _attn_kernel,
        grid=(batch, heads, seq // TQ),
        in_specs=[
            pl.BlockSpec((1, 1, TQ, QK_DIM), lambda b, h, i: (b, h, i, 0)),
            pl.BlockSpec((1, 1, seq, QK_DIM), kv_map),
            pl.BlockSpec((1, 1, seq, HEAD), kv_map),
            pl.BlockSpec((1, 1, n_ctx, QK_DIM), kv_map),
            pl.BlockSpec((1, 1, n_ctx, HEAD), kv_map),
        ],
        out_specs=pl.BlockSpec((1, TQ, HEAD), lambda b, h, i: (b, i, h)),
        out_shape=jax.ShapeDtypeStruct((batch, seq, heads * HEAD), F32),
        compiler_params=pltpu.CompilerParams(
            dimension_semantics=("parallel", "parallel", "parallel"),
            vmem_limit_bytes=VMEM_LIMIT),
        name="attn",
    )(q, k_l, v_l, k_c, v_c)


def _out_kernel(att_ref, cb_ref, z_ref, zp_ref, zn_ref, cw_ref, cbias_ref, gmo_ref, wout_ref,
                x_ref, mod_ref, gffn_ref, x1_ref, hf_ref, mix_ref):
    t = pl.program_id(1)
    tm = z_ref.shape[0]
    z = z_ref[...]
    row = lax.broadcasted_iota(jnp.int32, z.shape, 0)
    z_before = jnp.where(t == 0, 0.0, zp_ref[0, 7:8, :])
    z_after = jnp.where(t == pl.num_programs(1) - 1, 0.0, zn_ref[0, 0:1, :])
    z_prev = jnp.where(row == 0, z_before, pltpu.roll(z, 1, 0))
    z_next = jnp.where(row == tm - 1, z_after, pltpu.roll(z, tm - 1, 0))
    conv = cb_ref[...] * (z_prev * cw_ref[0:1, :] + z * cw_ref[1:2, :] + z_next * cw_ref[2:3, :]
                          + cbias_ref[...])

    n_att = att_ref.shape[1] // HEAD
    gt_a = mod_ref[0, 2:3, :]
    ffn_gain = gffn_ref[...] * (1.0 + mod_ref[0, 4:5, :])
    ffn_shift = mod_ref[0, 3:4, :]
    for g in range(D_MODEL // HEAD):
        cols = slice(g * HEAD, (g + 1) * HEAD)
        blk = att_ref[:, cols] if g < n_att else conv[:, (g - n_att) * HEAD:(g - n_att + 1) * HEAD]
        mix_ref[:, cols] = _rms(blk, gmo_ref[:, cols]).astype(BF16)
    x1 = x_ref[...] + gt_a * _dot(mix_ref[...], wout_ref[...])
    x1_ref[...] = x1
    hf_ref[...] = (_rms(x1, ffn_gain) + ffn_shift).astype(BF16)


def _out(att2, cb, z, cw, cbias, gmo, wout, x2, mod_l, gffn, batch, seq):
    tm = TM_OUT
    tpb = seq // tm
    n8 = batch * seq // 8
    z3 = z.reshape(n8, 8, CONV_W)
    const = lambda b, t: (0, 0)
    tok = lambda b, t: (b * tpb + t, 0)
    return pl.pallas_call(
        _out_kernel,
        grid=(batch, tpb),
        in_specs=[
            pl.BlockSpec((tm, att2.shape[1]), tok),
            pl.BlockSpec((tm, CONV_W), tok),
            pl.BlockSpec((tm, CONV_W), tok),
            pl.BlockSpec((1, 8, CONV_W), lambda b, t: (jnp.maximum((b * tpb + t) * (tm // 8) - 1, 0), 0, 0)),
            pl.BlockSpec((1, 8, CONV_W), lambda b, t: (jnp.minimum((b * tpb + t + 1) * (tm // 8), n8 - 1), 0, 0)),
            pl.BlockSpec((3, CONV_W), const),
            pl.BlockSpec((1, CONV_W), const),
            pl.BlockSpec((1, D_MODEL), const),
            pl.BlockSpec(wout.shape, const, pipeline_mode=pl.Buffered(1)),
            pl.BlockSpec((tm, D_MODEL), tok),
            pl.BlockSpec((1, 6, D_MODEL), lambda b, t: (b, 0, 0)),
            pl.BlockSpec((1, D_MODEL), const),
        ],
        out_specs=[
            pl.BlockSpec((tm, D_MODEL), tok),
            pl.BlockSpec((tm, D_MODEL), tok),
        ],
        out_shape=[
            jax.ShapeDtypeStruct((batch * seq, D_MODEL), F32),
            jax.ShapeDtypeStruct((batch * seq, D_MODEL), BF16),
        ],
        scratch_shapes=[pltpu.VMEM((tm, D_MODEL), BF16)],
        compiler_params=pltpu.CompilerParams(
            dimension_semantics=("parallel", "parallel"), vmem_limit_bytes=VMEM_LIMIT),
        name="out_proj",
    )(att2, cb, z, z3, z3, cw, cbias, gmo, wout, x2, mod_l, gffn)


def _ffn_kernel(h_ref, hp_ref, hn_ref, wup_ref, cw_ref, cbias_ref, wdn_ref, x1_ref, mod_ref, gfin_ref,
                o_ref, hbuf_ref, u_ref):
    t = pl.program_id(1)
    f = pl.program_id(2)
    tm = h_ref.shape[0]

    @pl.when(f == 0)
    def _():
        hbuf_ref[HALO:HALO + tm, :] = h_ref[...]
        hbuf_ref[0:HALO, :] = jnp.where(t == 0, jnp.zeros_like(hp_ref[0]), hp_ref[0])
        hbuf_ref[HALO + tm:, :] = jnp.where(t == pl.num_programs(1) - 1, jnp.zeros_like(hn_ref[0]), hn_ref[0])
        o_ref[...] = jnp.zeros_like(o_ref)

    u_ref[...] = _dot(hbuf_ref[...], wup_ref[0])
    cw = cw_ref[f]
    y = (u_ref[pl.ds(HALO - 1, tm), :] * cw[0:1, :] + u_ref[pl.ds(HALO, tm), :] * cw[1:2, :]
         + u_ref[pl.ds(HALO + 1, tm), :] * cw[2:3, :] + cbias_ref[f])
    act = (y[:, :TF_SUB] * _silu(y[:, TF_SUB:])).astype(BF16)
    o_ref[...] += _dot(act, wdn_ref[...])

    @pl.when(f == pl.num_programs(2) - 1)
    def _():
        y = x1_ref[...] + mod_ref[0, 5:6, :] * o_ref[...]
        o_ref[...] = _rms(y, gfin_ref[...])


def _ffn(hf, wup_t, cw_t, cbias_t, wdn, x1, mod_l, gfin, batch, seq):
    tm = TM_FFN
    tpb = seq // tm
    nf = N_SUB // FFN_SUBS
    tf = FFN_SUBS * TF_SUB
    n_h = batch * seq // HALO
    hf3 = hf.reshape(n_h, HALO, D_MODEL)
    tok = lambda b, t, f: (b * tpb + t, 0)
    return pl.pallas_call(
        _ffn_kernel,
        grid=(batch, tpb, nf),
        in_specs=[
            pl.BlockSpec((tm, D_MODEL), tok),
            pl.BlockSpec((1, HALO, D_MODEL),
                         lambda b, t, f: (jnp.maximum((b * tpb + t) * (tm // HALO) - 1, 0), 0, 0)),
            pl.BlockSpec((1, HALO, D_MODEL),
                         lambda b, t, f: (jnp.minimum((b * tpb + t + 1) * (tm // HALO), n_h - 1), 0, 0)),
            pl.BlockSpec((FFN_SUBS, D_MODEL, 2 * TF_SUB), lambda b, t, f: (f, 0, 0)),
            pl.BlockSpec((nf, 3, 2 * TF_SUB), lambda b, t, f: (0, 0, 0)),
            pl.BlockSpec((nf, 1, 2 * TF_SUB), lambda b, t, f: (0, 0, 0)),
            pl.BlockSpec((tf, D_MODEL), lambda b, t, f: (f, 0)),
            pl.BlockSpec((tm, D_MODEL), tok),
            pl.BlockSpec((1, 6, D_MODEL), lambda b, t, f: (b, 0, 0)),
            pl.BlockSpec((1, D_MODEL), lambda b, t, f: (0, 0)),
        ],
        out_specs=pl.BlockSpec((tm, D_MODEL), tok),
        out_shape=jax.ShapeDtypeStruct((batch * seq, D_MODEL), F32),
        scratch_shapes=[
            pltpu.VMEM((tm + 2 * HALO, D_MODEL), BF16),
            pltpu.VMEM((tm + 2 * HALO, 2 * TF_SUB), F32),
        ],
        compiler_params=pltpu.CompilerParams(
            dimension_semantics=("parallel", "parallel", "arbitrary"), vmem_limit_bytes=VMEM_LIMIT),
        name="ffn",
    )(hf, hf3, hf3, wup_t, cw_t, cbias_t, wdn, x1, mod_l, gfin)


def _rope_tables(seq):
    rows = seq // GRID_W
    row = jnp.repeat(jnp.arange(rows), GRID_W).astype(F32)
    col = jnp.tile(jnp.arange(GRID_W), rows).astype(F32)
    axis_dim = QK_ROPE // 2
    inv = ROPE_BASE ** (-jnp.arange(0, axis_dim, 2, dtype=F32) / axis_dim)
    ang = jnp.concatenate([row[:, None] * inv, col[:, None] * inv], axis=-1)
    cos, sin = jnp.cos(ang), jnp.sin(ang)
    c = jnp.repeat(cos, 2, axis=-1)
    s = jnp.stack([-sin, sin], axis=-1).reshape(seq, QK_ROPE)
    return jnp.tile(c, (1, N_HEADS)), jnp.tile(s, (1, N_HEADS))


def _tile_ffn_cols(a):
    pad = ((0, 0), (0, FFN_PAD - FFN_DIM))
    val = jnp.pad(a[:, :FFN_DIM], pad).reshape(a.shape[0], N_SUB, TF_SUB)
    gate = jnp.pad(a[:, FFN_DIM:], pad).reshape(a.shape[0], N_SUB, TF_SUB)
    return jnp.moveaxis(jnp.concatenate([val, gate], axis=-1), 1, 0)


def _blk_index(s, j):
    return jnp.minimum(s * BPS + j, FFN_BLKS - 1)


def _prep_up_kernel(*refs):
    o_ref = refs[-1]
    s = pl.program_id(0)
    for i in range(2 * BPS):
        real = s * BPS + i % BPS < FFN_BLKS
        o_ref[0, :, i * LANES:(i + 1) * LANES] = jnp.where(real, refs[i][...], 0.0).astype(BF16)


def _prep_up(w_up):
    blk = lambda fn: pl.BlockSpec((D_MODEL, LANES), fn)
    val = [blk(functools.partial(lambda s, j: (0, _blk_index(s, j)), j=j)) for j in range(BPS)]
    gate = [blk(functools.partial(lambda s, j: (0, FFN_BLKS + _blk_index(s, j)), j=j)) for j in range(BPS)]
    return pl.pallas_call(
        _prep_up_kernel,
        grid=(N_SUB,),
        in_specs=val + gate,
        out_specs=pl.BlockSpec((1, D_MODEL, 2 * TF_SUB), lambda s: (s, 0, 0)),
        out_shape=jax.ShapeDtypeStruct((N_SUB, D_MODEL, 2 * TF_SUB), BF16),
        compiler_params=pltpu.CompilerParams(
            dimension_semantics=("parallel",), vmem_limit_bytes=VMEM_LIMIT),
        name="prep_up",
    )(*([w_up] * (2 * BPS)))


def _prep_down_kernel(*refs):
    o_ref = refs[-1]
    s = pl.program_id(0)
    for j in range(BPS):
        real = s * BPS + j < FFN_BLKS
        o_ref[j * LANES:(j + 1) * LANES, :] = jnp.where(real, refs[j][...], 0.0).astype(BF16)


def _prep_down(w_down):
    return pl.pallas_call(
        _prep_down_kernel,
        grid=(N_SUB,),
        in_specs=[pl.BlockSpec((LANES, D_MODEL), functools.partial(lambda s, j: (_blk_index(s, j), 0), j=j))
                  for j in range(BPS)],
        out_specs=pl.BlockSpec((TF_SUB, D_MODEL), lambda s: (s, 0)),
        out_shape=jax.ShapeDtypeStruct((FFN_PAD, D_MODEL), BF16),
        compiler_params=pltpu.CompilerParams(
            dimension_semantics=("parallel",), vmem_limit_bytes=VMEM_LIMIT),
        name="prep_down",
    )(*([w_down] * BPS))


N_QKV = Q_RANK + KV_RANK + QK_ROPE
N_QKV_PAD = Q_RANK + KV_RANK + LANES


PREP_ROWS = 512


def _prep_qkv_kernel(w_ref, o_ref):
    o_ref[...] = w_ref[...].T.astype(BF16)


def _prep_conv_kernel(lo_ref, hi_ref, o_ref):
    off = N_QKV % PREP_ROWS
    o_ref[...] = jnp.concatenate([lo_ref[off:, :], hi_ref[:off, :]], axis=0).T.astype(BF16)


def _prep_in(w_in_t):
    wqkv = pl.pallas_call(
        _prep_qkv_kernel,
        grid=(N_QKV_PAD // LANES,),
        in_specs=[pl.BlockSpec((LANES, D_MODEL), lambda j: (j, 0))],
        out_specs=pl.BlockSpec((D_MODEL, LANES), lambda j: (0, j)),
        out_shape=jax.ShapeDtypeStruct((D_MODEL, N_QKV_PAD), BF16),
        compiler_params=pltpu.CompilerParams(
            dimension_semantics=("parallel",), vmem_limit_bytes=VMEM_LIMIT),
        name="prep_qkv",
    )(w_in_t)
    first = N_QKV // PREP_ROWS
    wconv = pl.pallas_call(
        _prep_conv_kernel,
        grid=(3 * CONV_W // PREP_ROWS,),
        in_specs=[
            pl.BlockSpec((PREP_ROWS, D_MODEL), lambda j: (first + j, 0)),
            pl.BlockSpec((PREP_ROWS, D_MODEL), lambda j: (first + j + 1, 0)),
        ],
        out_specs=pl.BlockSpec((D_MODEL, PREP_ROWS), lambda j: (0, j)),
        out_shape=jax.ShapeDtypeStruct((D_MODEL, 3 * CONV_W), BF16),
        compiler_params=pltpu.CompilerParams(
            dimension_semantics=("parallel",), vmem_limit_bytes=VMEM_LIMIT),
        name="prep_conv",
    )(w_in_t, w_in_t)
    return wqkv, wconv


def kernel(x, c, ctx, c_ctx, w_ada, b_ada, g_mix_norm, w_in, g_q_a, w_q_b, g_kv_a, w_kv_b, conv_w, conv_b, g_mix_out, w_out, g_ffn_norm, w_up, ffn_conv_w, ffn_conv_b, w_down, g_final):
    batch, seq, d = x.shape
    assert d == D_MODEL and w_ada.shape[0] == 1
    x2 = x.reshape(batch * seq, d)

    c8 = jnp.concatenate([c, c_ctx[None], jnp.zeros((8 - batch - 1, d), F32)], axis=0)
    mod = _ada(c8, w_ada[0], b_ada[0][None])
    mod_l = mod[:batch].reshape(batch, 6, d)
    mod_c = mod[batch:batch + 1].reshape(1, 6, d)

    wqkv, wconv = _prep_in(w_in[0].T)
    wqb3 = w_q_b[0].reshape(Q_RANK, N_HEADS, QK_DIM)
    wqb = jnp.concatenate([wqb3[:, :, :QK_NOPE].reshape(Q_RANK, -1),
                           wqb3[:, :, QK_NOPE:].reshape(Q_RANK, -1)], axis=-1).astype(BF16)
    wkvb = w_kv_b[0].astype(BF16)
    wout = w_out[0].astype(BF16)
    wup_t = _prep_up(w_up[0])
    cw_t = _tile_ffn_cols(ffn_conv_w[0])
    cbias_t = _tile_ffn_cols(ffn_conv_b)
    wdn = _prep_down(w_down[0])

    cq, sq = _rope_tables(seq)

    q, k_l, v_l, cb, z = _in_lat(x2, mod_l, g_mix_norm, wqkv, wconv, g_q_a, wqb, g_kv_a, wkvb,
                                 cq, sq, batch, seq)
    k_c, v_c = _in_ctx(ctx, mod_c, g_mix_norm, wqkv, g_kv_a, wkvb)
    att = _attn(q, k_l, v_l, k_c, v_c)
    x1, hf = _out(att.reshape(batch * seq, N_HEADS * HEAD), cb, z, conv_w[0], conv_b, g_mix_out,
                  wout, x2, mod_l, g_ffn_norm, batch, seq)
    y = _ffn(hf, wup_t, cw_t, cbias_t, wdn, x1, mod_l, g_final[None], batch, seq)
    return y.reshape(batch, seq, d)
```

```python
import functools

import jax
import jax.numpy as jnp
from jax import lax
from jax.experimental import pallas as pl
from jax.experimental.pallas import tpu as pltpu

F32 = jnp.float32
BF16 = jnp.bfloat16

D_MODEL = 2048
N_HEADS = 8
HEAD = 128
QK_NOPE = 128
QK_ROPE = 64
QK_DIM = QK_NOPE + QK_ROPE
Q_RANK = 512
KV_RANK = 256
CONV_W = 1024
GRID_W = 64
ROPE_BASE = 10000.0
EPS = 1e-6
LOG2E = 1.4426950408889634
FFN_DIM = 5504

V7X_VMEM_BYTES = 64 * 1024 * 1024
VMEM_LIMIT = V7X_VMEM_BYTES - 8 * 1024 * 1024

TM_IN = 512
TQ = 2048
TQ_SUB = 256
TM_OUT = 512
NORM_ROWS = 64
TM_FFN = 512
LANES = 128
TF_SUB = 512
FFN_SUBS = 1
HALO = 16
BPS = TF_SUB // LANES
FFN_BLKS = FFN_DIM // LANES
N_SUB = (FFN_DIM + TF_SUB - 1) // TF_SUB
FFN_PAD = N_SUB * TF_SUB


def _rms(x, g):
    return x * lax.rsqrt(jnp.mean(x * x, axis=-1, keepdims=True) + EPS) * g


def _silu(x):
    h = 0.5 * x
    return h * (1.0 + jnp.tanh(h))


def _dot(a, b):
    return jnp.dot(a, b, preferred_element_type=F32)


def _rope(x, c, s):
    w = x.shape[-1]
    lane = lax.broadcasted_iota(jnp.int32, x.shape, 1)
    from_right = pltpu.roll(x, w - 1, 1)
    from_left = pltpu.roll(x, 1, 1)
    partner = jnp.where(lane % 2 == 0, from_right, from_left)
    return x * c + partner * s


def _ada_kernel(c_ref, w_ref, b_ref, o_ref):
    s = _silu(c_ref[...]).astype(BF16)
    o_ref[...] = _dot(s, w_ref[...].astype(BF16)) + b_ref[...]


def _ada(c8, w, b):
    n = w.shape[1]
    tn = 1024
    return pl.pallas_call(
        _ada_kernel,
        grid=(n // tn,),
        in_specs=[
            pl.BlockSpec((8, D_MODEL), lambda j: (0, 0)),
            pl.BlockSpec((D_MODEL, tn), lambda j: (0, j)),
            pl.BlockSpec((1, tn), lambda j: (0, j)),
        ],
        out_specs=pl.BlockSpec((8, tn), lambda j: (0, j)),
        out_shape=jax.ShapeDtypeStruct((8, n), F32),
        compiler_params=pltpu.CompilerParams(
            dimension_semantics=("parallel",), vmem_limit_bytes=VMEM_LIMIT),
        name="ada",
    )(c8, w, b)


def _in_lat_kernel(x_ref, mod_ref, gmix_ref, wqkv_ref, wconv_ref, gq_ref, wqb_ref, gkv_ref,
                   wkvb_ref, cq_ref, sq_ref, q_ref, k_ref, v_ref, cb_ref, z_ref):
    sh = mod_ref[0, 0:1, :]
    sc = mod_ref[0, 1:2, :]
    h = (_rms(x_ref[...], gmix_ref[...]) * (1.0 + sc) + sh).astype(BF16)

    p2 = _dot(h, wconv_ref[...])
    cb_ref[...] = p2[:, :CONV_W]
    z_ref[...] = p2[:, CONV_W:2 * CONV_W] * p2[:, 2 * CONV_W:]

    p1 = _dot(h, wqkv_ref[...])
    qa = p1[:, :Q_RANK]
    kva = p1[:, Q_RANK:Q_RANK + KV_RANK]
    kr = p1[:, Q_RANK + KV_RANK:]

    c1 = cq_ref[...]
    s1 = sq_ref[...]
    reps = N_HEADS * QK_ROPE // LANES
    cq = jnp.concatenate([c1] * reps, axis=1)
    sq = jnp.concatenate([s1] * reps, axis=1)
    scale = QK_DIM ** -0.5 * LOG2E
    q = _dot(_rms(qa, gq_ref[...]).astype(BF16), wqb_ref[...])
    qn = (q[:, :N_HEADS * QK_NOPE] * scale).astype(BF16)
    qr = (_rope(q[:, N_HEADS * QK_NOPE:], cq, sq) * scale).astype(BF16)

    kv = _dot(_rms(kva, gkv_ref[...]).astype(BF16), wkvb_ref[...]).astype(BF16)
    krr = _rope(kr, c1, s1)[:, :QK_ROPE].astype(BF16)
    for hd in range(N_HEADS):
        q_ref[0, hd, :, 0:QK_NOPE] = qn[:, hd * QK_NOPE:(hd + 1) * QK_NOPE]
        q_ref[0, hd, :, QK_NOPE:QK_DIM] = qr[:, hd * QK_ROPE:(hd + 1) * QK_ROPE]
        k_ref[0, hd, :, 0:QK_NOPE] = kv[:, hd * 2 * HEAD:hd * 2 * HEAD + QK_NOPE]
        k_ref[0, hd, :, QK_NOPE:QK_DIM] = krr
        v_ref[0, hd, :, :] = kv[:, hd * 2 * HEAD + QK_NOPE:(hd + 1) * 2 * HEAD]


def _in_lat(x2, mod_l, gmix, wqkv, wconv, gq, wqb, gkv, wkvb, cq, sq, batch, seq):
    tm = TM_IN
    tpb = seq // tm
    const = lambda b, t: (0, 0)
    one = pl.Buffered(1)
    tok = lambda b, t: (b * tpb + t, 0)
    hd4 = lambda b, t: (b, 0, t, 0)
    return pl.pallas_call(
        _in_lat_kernel,
        grid=(batch, tpb),
        in_specs=[
            pl.BlockSpec((tm, D_MODEL), tok),
            pl.BlockSpec((1, 6, D_MODEL), lambda b, t: (b, 0, 0)),
            pl.BlockSpec((1, D_MODEL), const),
            pl.BlockSpec(wqkv.shape, const, pipeline_mode=one),
            pl.BlockSpec(wconv.shape, const, pipeline_mode=one),
            pl.BlockSpec((1, Q_RANK), const),
            pl.BlockSpec(wqb.shape, const, pipeline_mode=one),
            pl.BlockSpec((1, KV_RANK), const),
            pl.BlockSpec(wkvb.shape, const, pipeline_mode=one),
            pl.BlockSpec((tm, LANES), lambda b, t: (t, 0)),
            pl.BlockSpec((tm, LANES), lambda b, t: (t, 0)),
        ],
        out_specs=[
            pl.BlockSpec((1, N_HEADS, tm, QK_DIM), hd4),
            pl.BlockSpec((1, N_HEADS, tm, QK_DIM), hd4),
            pl.BlockSpec((1, N_HEADS, tm, HEAD), hd4),
            pl.BlockSpec((tm, CONV_W), tok),
            pl.BlockSpec((tm, CONV_W), tok),
        ],
        out_shape=[
            jax.ShapeDtypeStruct((batch, N_HEADS, seq, QK_DIM), BF16),
            jax.ShapeDtypeStruct((batch, N_HEADS, seq, QK_DIM), BF16),
            jax.ShapeDtypeStruct((batch, N_HEADS, seq, HEAD), BF16),
            jax.ShapeDtypeStruct((batch * seq, CONV_W), F32),
            jax.ShapeDtypeStruct((batch * seq, CONV_W), F32),
        ],
        compiler_params=pltpu.CompilerParams(
            dimension_semantics=("parallel", "parallel"), vmem_limit_bytes=VMEM_LIMIT),
        name="in_lat",
    )(x2, mod_l, gmix, wqkv, wconv, gq, wqb, gkv, wkvb, cq, sq)


def _in_ctx_kernel(x_ref, mod_ref, gmix_ref, wkv_ref, gkv_ref, wkvb_ref, k_ref, v_ref):
    sh = mod_ref[0, 0:1, :]
    sc = mod_ref[0, 1:2, :]
    h = (_rms(x_ref[0], gmix_ref[...]) * (1.0 + sc) + sh).astype(BF16)
    p1 = _dot(h, wkv_ref[:, Q_RANK:])
    kva = p1[:, :KV_RANK]
    kr = p1[:, KV_RANK:KV_RANK + QK_ROPE].astype(BF16)
    kv = _dot(_rms(kva, gkv_ref[...]).astype(BF16), wkvb_ref[...]).astype(BF16)
    for hd in range(N_HEADS):
        k_ref[0, hd, :, 0:QK_NOPE] = kv[:, hd * 2 * HEAD:hd * 2 * HEAD + QK_NOPE]
        k_ref[0, hd, :, QK_NOPE:QK_DIM] = kr
        v_ref[0, hd, :, :] = kv[:, hd * 2 * HEAD + QK_NOPE:(hd + 1) * 2 * HEAD]


def _in_ctx(ctx, mod_c, gmix, wkv, gkv, wkvb):
    batch, n_ctx, _ = ctx.shape
    const = lambda b: (0, 0)
    return pl.pallas_call(
        _in_ctx_kernel,
        grid=(batch,),
        in_specs=[
            pl.BlockSpec((1, n_ctx, D_MODEL), lambda b: (b, 0, 0)),
            pl.BlockSpec((1, 6, D_MODEL), lambda b: (0, 0, 0)),
            pl.BlockSpec((1, D_MODEL), const),
            pl.BlockSpec(wkv.shape, const),
            pl.BlockSpec((1, KV_RANK), const),
            pl.BlockSpec(wkvb.shape, const),
        ],
        out_specs=[
            pl.BlockSpec((1, N_HEADS, n_ctx, QK_DIM), lambda b: (b, 0, 0, 0)),
            pl.BlockSpec((1, N_HEADS, n_ctx, HEAD), lambda b: (b, 0, 0, 0)),
        ],
        out_shape=[
            jax.ShapeDtypeStruct((batch, N_HEADS, n_ctx, QK_DIM), BF16),
            jax.ShapeDtypeStruct((batch, N_HEADS, n_ctx, HEAD), BF16),
        ],
        compiler_params=pltpu.CompilerParams(
            dimension_semantics=("parallel",), vmem_limit_bytes=VMEM_LIMIT),
        name="in_ctx",
    )(ctx, mod_c, gmix, wkv, gkv, wkvb)


def _attn_kernel(q_ref, kl_ref, vl_ref, kc_ref, vc_ref, o_ref):
    nt = (((1,), (1,)), ((), ()))
    for j in range(TQ // TQ_SUB):
        rows = slice(j * TQ_SUB, (j + 1) * TQ_SUB)
        q = q_ref[0, 0, rows, :]
        s_l = lax.dot_general(q, kl_ref[0, 0], nt, preferred_element_type=F32)
        s_c = lax.dot_general(q, kc_ref[0, 0], nt, preferred_element_type=F32)
        m = jnp.maximum(jnp.max(s_l, axis=-1, keepdims=True), jnp.max(s_c, axis=-1, keepdims=True))
        p_l = jnp.exp2(s_l - m)
        p_c = jnp.exp2(s_c - m)
        denom = jnp.sum(p_l, axis=-1, keepdims=True) + jnp.sum(p_c, axis=-1, keepdims=True)
        o = _dot(p_l.astype(BF16), vl_ref[0, 0]) + _dot(p_c.astype(BF16), vc_ref[0, 0])
        o_ref[0, rows, :] = o / denom


def _attn(q, k_l, v_l, k_c, v_c):
    batch, heads, seq, _ = q.shape
    n_ctx = k_c.shape[2]
    kv_map = lambda b, h, i: (b, h, 0, 0)
    return pl.pallas_call(
        _attn_kernel,
        grid=(batch, heads, seq // TQ),
        in_specs=[
            pl.BlockSpec((1, 1, TQ, QK_DIM), lambda b, h, i: (b, h, i, 0)),
            pl.BlockSpec((1, 1, seq, QK_DIM), kv_map),
            pl.BlockSpec((1, 1, seq, HEAD), kv_map),
            pl.BlockSpec((1, 1, n_ctx, QK_DIM), kv_map),
            pl.BlockSpec((1, 1, n_ctx, HEAD), kv_map),
        ],
        out_specs=pl.BlockSpec((1, TQ, HEAD), lambda b, h, i: (b, i, h)),
        out_shape=jax.ShapeDtypeStruct((batch, seq, heads * HEAD), F32),
        compiler_params=pltpu.CompilerParams(
            dimension_semantics=("parallel", "parallel", "parallel"),
            vmem_limit_bytes=VMEM_LIMIT),
        name="attn",
    )(q, k_l, v_l, k_c, v_c)


def _out_kernel(att_ref, cb_ref, z_ref, zp_ref, zn_ref, cw_ref, cbias_ref, gmo_ref, wout_ref,
                x_ref, mod_ref, gffn_ref, x1_ref, hf_ref, mix_ref):
    t = pl.program_id(1)
    tm = z_ref.shape[0]
    z = z_ref[...]
    row = lax.broadcasted_iota(jnp.int32, z.shape, 0)
    z_before = jnp.where(t == 0, 0.0, zp_ref[0, 7:8, :])
    z_after = jnp.where(t == pl.num_programs(1) - 1, 0.0, zn_ref[0, 0:1, :])
    z_prev = jnp.where(row == 0, z_before, pltpu.roll(z, 1, 0))
    z_next = jnp.where(row == tm - 1, z_after, pltpu.roll(z, tm - 1, 0))
    conv = cb_ref[...] * (z_prev * cw_ref[0:1, :] + z * cw_ref[1:2, :] + z_next * cw_ref[2:3, :]
                          + cbias_ref[...])

    n_att = att_ref.shape[1] // HEAD
    gt_a = mod_ref[0, 2:3, :]
    ffn_gain = gffn_ref[...] * (1.0 + mod_ref[0, 4:5, :])
    ffn_shift = mod_ref[0, 3:4, :]
    for r in range(0, tm, NORM_ROWS):
        rows = slice(r, r + NORM_ROWS)
        for g in range(D_MODEL // HEAD):
            cols = slice(g * HEAD, (g + 1) * HEAD)
            blk = att_ref[rows, cols] if g < n_att else conv[rows, (g - n_att) * HEAD:(g - n_att + 1) * HEAD]
            mix_ref[rows, cols] = _rms(blk, gmo_ref[:, cols]).astype(BF16)
    x1 = x_ref[...] + gt_a * _dot(mix_ref[...], wout_ref[...])
    x1_ref[...] = x1
    hf_ref[...] = (_rms(x1, ffn_gain) + ffn_shift).astype(BF16)


def _out(att2, cb, z, cw, cbias, gmo, wout, x2, mod_l, gffn, batch, seq):
    tm = TM_OUT
    tpb = seq // tm
    n8 = batch * seq // 8
    z3 = z.reshape(n8, 8, CONV_W)
    const = lambda b, t: (0, 0)
    tok = lambda b, t: (b * tpb + t, 0)
    return pl.pallas_call(
        _out_kernel,
        grid=(batch, tpb),
        in_specs=[
            pl.BlockSpec((tm, att2.shape[1]), tok),
            pl.BlockSpec((tm, CONV_W), tok),
            pl.BlockSpec((tm, CONV_W), tok),
            pl.BlockSpec((1, 8, CONV_W), lambda b, t: (jnp.maximum((b * tpb + t) * (tm // 8) - 1, 0), 0, 0)),
            pl.BlockSpec((1, 8, CONV_W), lambda b, t: (jnp.minimum((b * tpb + t + 1) * (tm // 8), n8 - 1), 0, 0)),
            pl.BlockSpec((3, CONV_W), const),
            pl.BlockSpec((1, CONV_W), const),
            pl.BlockSpec((1, D_MODEL), const),
            pl.BlockSpec(wout.shape, const, pipeline_mode=pl.Buffered(1)),
            pl.BlockSpec((tm, D_MODEL), tok),
            pl.BlockSpec((1, 6, D_MODEL), lambda b, t: (b, 0, 0)),
            pl.BlockSpec((1, D_MODEL), const),
        ],
        out_specs=[
            pl.BlockSpec((tm, D_MODEL), tok),
            pl.BlockSpec((tm, D_MODEL), tok),
        ],
        out_shape=[
            jax.ShapeDtypeStruct((batch * seq, D_MODEL), F32),
            jax.ShapeDtypeStruct((batch * seq, D_MODEL), BF16),
        ],
        scratch_shapes=[pltpu.VMEM((tm, D_MODEL), BF16)],
        compiler_params=pltpu.CompilerParams(
            dimension_semantics=("parallel", "parallel"), vmem_limit_bytes=VMEM_LIMIT),
        name="out_proj",
    )(att2, cb, z, z3, z3, cw, cbias, gmo, wout, x2, mod_l, gffn)


def _ffn_kernel(h_ref, hp_ref, hn_ref, wup_ref, cw_ref, cbias_ref, wdn_ref, x1_ref, mod_ref, gfin_ref,
                o_ref, hbuf_ref, u_ref):
    t = pl.program_id(1)
    f = pl.program_id(2)
    tm = h_ref.shape[0]

    @pl.when(f == 0)
    def _():
        hbuf_ref[HALO:HALO + tm, :] = h_ref[...]
        hbuf_ref[0:HALO, :] = jnp.where(t == 0, jnp.zeros_like(hp_ref[0]), hp_ref[0])
        hbuf_ref[HALO + tm:, :] = jnp.where(t == pl.num_programs(1) - 1, jnp.zeros_like(hn_ref[0]), hn_ref[0])
        o_ref[...] = jnp.zeros_like(o_ref)

    u_ref[...] = _dot(hbuf_ref[...], wup_ref[0])
    cw = cw_ref[f]
    y = (u_ref[pl.ds(HALO - 1, tm), :] * cw[0:1, :] + u_ref[pl.ds(HALO, tm), :] * cw[1:2, :]
         + u_ref[pl.ds(HALO + 1, tm), :] * cw[2:3, :] + cbias_ref[f])
    act = jnp.concatenate(
        [y[:, 2 * j * LANES:(2 * j + 1) * LANES] * _silu(y[:, (2 * j + 1) * LANES:(2 * j + 2) * LANES])
         for j in range(BPS)], axis=1).astype(BF16)
    o_ref[...] += _dot(act, wdn_ref[...])

    @pl.when(f == pl.num_programs(2) - 1)
    def _():
        y = x1_ref[...] + mod_ref[0, 5:6, :] * o_ref[...]
        o_ref[...] = _rms(y, gfin_ref[...])


def _ffn(hf, wup_t, cw_t, cbias_t, wdn, x1, mod_l, gfin, batch, seq):
    tm = TM_FFN
    tpb = seq // tm
    nf = N_SUB // FFN_SUBS
    tf = FFN_SUBS * TF_SUB
    n_h = batch * seq // HALO
    hf3 = hf.reshape(n_h, HALO, D_MODEL)
    tok = lambda b, t, f: (b * tpb + t, 0)
    return pl.pallas_call(
        _ffn_kernel,
        grid=(batch, tpb, nf),
        in_specs=[
            pl.BlockSpec((tm, D_MODEL), tok),
            pl.BlockSpec((1, HALO, D_MODEL),
                         lambda b, t, f: (jnp.maximum((b * tpb + t) * (tm // HALO) - 1, 0), 0, 0)),
            pl.BlockSpec((1, HALO, D_MODEL),
                         lambda b, t, f: (jnp.minimum((b * tpb + t + 1) * (tm // HALO), n_h - 1), 0, 0)),
            pl.BlockSpec((FFN_SUBS, D_MODEL, 2 * TF_SUB), lambda b, t, f: (f, 0, 0)),
            pl.BlockSpec((nf, 3, 2 * TF_SUB), lambda b, t, f: (0, 0, 0)),
            pl.BlockSpec((nf, 1, 2 * TF_SUB), lambda b, t, f: (0, 0, 0)),
            pl.BlockSpec((tf, D_MODEL), lambda b, t, f: (f, 0)),
            pl.BlockSpec((tm, D_MODEL), tok),
            pl.BlockSpec((1, 6, D_MODEL), lambda b, t, f: (b, 0, 0)),
            pl.BlockSpec((1, D_MODEL), lambda b, t, f: (0, 0)),
        ],
        out_specs=pl.BlockSpec((tm, D_MODEL), tok),
        out_shape=jax.ShapeDtypeStruct((batch * seq, D_MODEL), F32),
        scratch_shapes=[
            pltpu.VMEM((tm + 2 * HALO, D_MODEL), BF16),
            pltpu.VMEM((tm + 2 * HALO, 2 * TF_SUB), F32),
        ],
        compiler_params=pltpu.CompilerParams(
            dimension_semantics=("parallel", "parallel", "arbitrary"), vmem_limit_bytes=VMEM_LIMIT),
        name="ffn",
    )(hf, hf3, hf3, wup_t, cw_t, cbias_t, wdn, x1, mod_l, gfin)


def _rope_tables(seq):
    rows = seq // GRID_W
    row = jnp.repeat(jnp.arange(rows), GRID_W).astype(F32)
    col = jnp.tile(jnp.arange(GRID_W), rows).astype(F32)
    axis_dim = QK_ROPE // 2
    inv = ROPE_BASE ** (-jnp.arange(0, axis_dim, 2, dtype=F32) / axis_dim)
    ang = jnp.concatenate([row[:, None] * inv, col[:, None] * inv], axis=-1)
    cos, sin = jnp.cos(ang), jnp.sin(ang)
    c = jnp.repeat(cos, 2, axis=-1)
    s = jnp.stack([-sin, sin], axis=-1).reshape(seq, QK_ROPE)
    return jnp.tile(c, (1, LANES // QK_ROPE)), jnp.tile(s, (1, LANES // QK_ROPE))


def _tile_ffn_cols(a):
    pad = ((0, 0), (0, FFN_PAD - FFN_DIM))
    val = jnp.pad(a[:, :FFN_DIM], pad).reshape(a.shape[0], N_SUB, BPS, 1, LANES)
    gate = jnp.pad(a[:, FFN_DIM:], pad).reshape(a.shape[0], N_SUB, BPS, 1, LANES)
    both = jnp.concatenate([val, gate], axis=3).reshape(a.shape[0], N_SUB, 2 * TF_SUB)
    return jnp.moveaxis(both, 1, 0)


def _blk_index(s, j):
    return jnp.minimum(s * BPS + j, FFN_BLKS - 1)


def _prep_up_kernel(*refs):
    o_ref = refs[-1]
    s = pl.program_id(0)
    for i in range(2 * BPS):
        real = s * BPS + i // 2 < FFN_BLKS
        o_ref[0, :, i * LANES:(i + 1) * LANES] = jnp.where(real, refs[i][...], 0.0).astype(BF16)


def _prep_up(w_up):
    blk = lambda fn: pl.BlockSpec((D_MODEL, LANES), fn)
    specs = []
    for j in range(BPS):
        specs.append(blk(functools.partial(lambda s, j: (0, _blk_index(s, j)), j=j)))
        specs.append(blk(functools.partial(lambda s, j: (0, FFN_BLKS + _blk_index(s, j)), j=j)))
    return pl.pallas_call(
        _prep_up_kernel,
        grid=(N_SUB,),
        in_specs=specs,
        out_specs=pl.BlockSpec((1, D_MODEL, 2 * TF_SUB), lambda s: (s, 0, 0)),
        out_shape=jax.ShapeDtypeStruct((N_SUB, D_MODEL, 2 * TF_SUB), BF16),
        compiler_params=pltpu.CompilerParams(
            dimension_semantics=("parallel",), vmem_limit_bytes=VMEM_LIMIT),
        name="prep_up",
    )(*([w_up] * (2 * BPS)))


def _prep_down_kernel(*refs):
    o_ref = refs[-1]
    s = pl.program_id(0)
    for j in range(BPS):
        real = s * BPS + j < FFN_BLKS
        o_ref[j * LANES:(j + 1) * LANES, :] = jnp.where(real, refs[j][...], 0.0).astype(BF16)


def _prep_down(w_down):
    return pl.pallas_call(
        _prep_down_kernel,
        grid=(N_SUB,),
        in_specs=[pl.BlockSpec((LANES, D_MODEL), functools.partial(lambda s, j: (_blk_index(s, j), 0), j=j))
                  for j in range(BPS)],
        out_specs=pl.BlockSpec((TF_SUB, D_MODEL), lambda s: (s, 0)),
        out_shape=jax.ShapeDtypeStruct((FFN_PAD, D_MODEL), BF16),
        compiler_params=pltpu.CompilerParams(
            dimension_semantics=("parallel",), vmem_limit_bytes=VMEM_LIMIT),
        name="prep_down",
    )(*([w_down] * BPS))


N_QKV = Q_RANK + KV_RANK + QK_ROPE
N_QKV_PAD = Q_RANK + KV_RANK + LANES


PREP_ROWS = 512


def _prep_qkv_kernel(w_ref, o_ref):
    o_ref[...] = w_ref[...].T.astype(BF16)


def _prep_conv_kernel(lo_ref, hi_ref, o_ref):
    off = N_QKV % PREP_ROWS
    o_ref[...] = jnp.concatenate([lo_ref[off:, :], hi_ref[:off, :]], axis=0).T.astype(BF16)


def _prep_in(w_in_t):
    wqkv = pl.pallas_call(
        _prep_qkv_kernel,
        grid=(N_QKV_PAD // LANES,),
        in_specs=[pl.BlockSpec((LANES, D_MODEL), lambda j: (j, 0))],
        out_specs=pl.BlockSpec((D_MODEL, LANES), lambda j: (0, j)),
        out_shape=jax.ShapeDtypeStruct((D_MODEL, N_QKV_PAD), BF16),
        compiler_params=pltpu.CompilerParams(
            dimension_semantics=("parallel",), vmem_limit_bytes=VMEM_LIMIT),
        name="prep_qkv",
    )(w_in_t)
    first = N_QKV // PREP_ROWS
    wconv = pl.pallas_call(
        _prep_conv_kernel,
        grid=(3 * CONV_W // PREP_ROWS,),
        in_specs=[
            pl.BlockSpec((PREP_ROWS, D_MODEL), lambda j: (first + j, 0)),
            pl.BlockSpec((PREP_ROWS, D_MODEL), lambda j: (first + j + 1, 0)),
        ],
        out_specs=pl.BlockSpec((D_MODEL, PREP_ROWS), lambda j: (0, j)),
        out_shape=jax.ShapeDtypeStruct((D_MODEL, 3 * CONV_W), BF16),
        compiler_params=pltpu.CompilerParams(
            dimension_semantics=("parallel",), vmem_limit_bytes=VMEM_LIMIT),
        name="prep_conv",
    )(w_in_t, w_in_t)
    return wqkv, wconv


def kernel(x, c, ctx, c_ctx, w_ada, b_ada, g_mix_norm, w_in, g_q_a, w_q_b, g_kv_a, w_kv_b, conv_w, conv_b, g_mix_out, w_out, g_ffn_norm, w_up, ffn_conv_w, ffn_conv_b, w_down, g_final):
    batch, seq, d = x.shape
    assert d == D_MODEL and w_ada.shape[0] == 1
    x2 = x.reshape(batch * seq, d)

    c8 = jnp.concatenate([c, c_ctx[None], jnp.zeros((8 - batch - 1, d), F32)], axis=0)
    mod = _ada(c8, w_ada[0], b_ada[0][None])
    mod_l = mod[:batch].reshape(batch, 6, d)
    mod_c = mod[batch:batch + 1].reshape(1, 6, d)

    wqkv, wconv = _prep_in(w_in[0].T)
    wqb3 = w_q_b[0].reshape(Q_RANK, N_HEADS, QK_DIM)
    wqb = jnp.concatenate([wqb3[:, :, :QK_NOPE].reshape(Q_RANK, -1),
                           wqb3[:, :, QK_NOPE:].reshape(Q_RANK, -1)], axis=-1).astype(BF16)
    wkvb = w_kv_b[0].astype(BF16)
    wout = w_out[0].astype(BF16)
    wup_t = _prep_up(w_up[0])
    cw_t = _tile_ffn_cols(ffn_conv_w[0])
    cbias_t = _tile_ffn_cols(ffn_conv_b)
    wdn = _prep_down(w_down[0])

    cq, sq = _rope_tables(seq)

    q, k_l, v_l, cb, z = _in_lat(x2, mod_l, g_mix_norm, wqkv, wconv, g_q_a, wqb, g_kv_a, wkvb,
                                 cq, sq, batch, seq)
    k_c, v_c = _in_ctx(ctx, mod_c, g_mix_norm, wqkv, g_kv_a, wkvb)
    att = _attn(q, k_l, v_l, k_c, v_c)
    x1, hf = _out(att.reshape(batch * seq, N_HEADS * HEAD), cb, z, conv_w[0], conv_b, g_mix_out,
                  wout, x2, mod_l, g_ffn_norm, batch, seq)
    y = _ffn(hf, wup_t, cw_t, cbias_t, wdn, x1, mod_l, g_final[None], batch, seq)
    return y.reshape(batch, seq, d)
```

```python
import functools

import jax
import jax.numpy as jnp
from jax import lax
from jax.experimental import pallas as pl
from jax.experimental.pallas import tpu as pltpu

F32 = jnp.float32
BF16 = jnp.bfloat16

D_MODEL = 2048
N_HEADS = 8
HEAD = 128
QK_NOPE = 128
QK_ROPE = 64
QK_DIM = QK_NOPE + QK_ROPE
Q_RANK = 512
KV_RANK = 256
CONV_W = 1024
GRID_W = 64
ROPE_BASE = 10000.0
EPS = 1e-6
LOG2E = 1.4426950408889634
FFN_DIM = 5504

V7X_VMEM_BYTES = 64 * 1024 * 1024
VMEM_LIMIT = V7X_VMEM_BYTES - 8 * 1024 * 1024

TM_IN = 512
TQ = 2048
TQ_SUB = 256
TM_OUT = 512
NORM_ROWS = 64
TM_FFN = 512
LANES = 128
TF_SUB = 512
FFN_SUBS = 1
HALO = 16
BPS = TF_SUB // LANES
FFN_BLKS = FFN_DIM // LANES
N_SUB = (FFN_DIM + TF_SUB - 1) // TF_SUB
FFN_PAD = N_SUB * TF_SUB


def _rms(x, g):
    return x * lax.rsqrt(jnp.mean(x * x, axis=-1, keepdims=True) + EPS) * g


def _silu(x):
    h = 0.5 * x
    return h * (1.0 + jnp.tanh(h))


def _dot(a, b):
    return jnp.dot(a, b, preferred_element_type=F32)


def _rope(x, c, s):
    w = x.shape[-1]
    lane = lax.broadcasted_iota(jnp.int32, x.shape, 1)
    from_right = pltpu.roll(x, w - 1, 1)
    from_left = pltpu.roll(x, 1, 1)
    partner = jnp.where(lane % 2 == 0, from_right, from_left)
    return x * c + partner * s


def _ada_kernel(c_ref, w_ref, b_ref, o_ref):
    s = _silu(c_ref[...]).astype(BF16)
    o_ref[...] = _dot(s, w_ref[...].astype(BF16)) + b_ref[...]


def _ada(c8, w, b):
    n = w.shape[1]
    tn = 1024
    return pl.pallas_call(
        _ada_kernel,
        grid=(n // tn,),
        in_specs=[
            pl.BlockSpec((8, D_MODEL), lambda j: (0, 0)),
            pl.BlockSpec((D_MODEL, tn), lambda j: (0, j)),
            pl.BlockSpec((1, tn), lambda j: (0, j)),
        ],
        out_specs=pl.BlockSpec((8, tn), lambda j: (0, j)),
        out_shape=jax.ShapeDtypeStruct((8, n), F32),
        compiler_params=pltpu.CompilerParams(
            dimension_semantics=("parallel",), vmem_limit_bytes=VMEM_LIMIT),
        name="ada",
    )(c8, w, b)


def _in_lat_kernel(x_ref, mod_ref, gmix_ref, wqkv_ref, wconv_ref, gq_ref, wqb_ref, gkv_ref,
                   wkvb_ref, cq_ref, sq_ref, q_ref, k_ref, v_ref, cb_ref, z_ref):
    sh = mod_ref[0, 0:1, :]
    sc = mod_ref[0, 1:2, :]
    h = (_rms(x_ref[...], gmix_ref[...]) * (1.0 + sc) + sh).astype(BF16)

    p2 = _dot(h, wconv_ref[...])
    cb_ref[...] = p2[:, :CONV_W]
    z_ref[...] = p2[:, CONV_W:2 * CONV_W] * p2[:, 2 * CONV_W:]

    p1 = _dot(h, wqkv_ref[...])
    qa = p1[:, :Q_RANK]
    kva = p1[:, Q_RANK:Q_RANK + KV_RANK]
    kr = p1[:, Q_RANK + KV_RANK:]

    c1 = cq_ref[...]
    s1 = sq_ref[...]
    reps = N_HEADS * QK_ROPE // LANES
    cq = jnp.concatenate([c1] * reps, axis=1)
    sq = jnp.concatenate([s1] * reps, axis=1)
    scale = QK_DIM ** -0.5 * LOG2E
    q = _dot(_rms(qa, gq_ref[...]).astype(BF16), wqb_ref[...])
    qn = (q[:, :N_HEADS * QK_NOPE] * scale).astype(BF16)
    qr = (_rope(q[:, N_HEADS * QK_NOPE:], cq, sq) * scale).astype(BF16)

    kv = _dot(_rms(kva, gkv_ref[...]).astype(BF16), wkvb_ref[...]).astype(BF16)
    krr = _rope(kr, c1, s1)[:, :QK_ROPE].astype(BF16)
    for hd in range(N_HEADS):
        q_ref[0, hd, :, 0:QK_NOPE] = qn[:, hd * QK_NOPE:(hd + 1) * QK_NOPE]
        q_ref[0, hd, :, QK_NOPE:QK_DIM] = qr[:, hd * QK_ROPE:(hd + 1) * QK_ROPE]
        k_ref[0, hd, :, 0:QK_NOPE] = kv[:, hd * 2 * HEAD:hd * 2 * HEAD + QK_NOPE]
        k_ref[0, hd, :, QK_NOPE:QK_DIM] = krr
        v_ref[0, hd, :, :] = kv[:, hd * 2 * HEAD + QK_NOPE:(hd + 1) * 2 * HEAD]


def _in_lat(x2, mod_l, gmix, wqkv, wconv, gq, wqb, gkv, wkvb, cq, sq, batch, seq):
    tm = TM_IN
    tpb = seq // tm
    const = lambda b, t: (0, 0)
    one = pl.Buffered(1)
    tok = lambda b, t: (b * tpb + t, 0)
    hd4 = lambda b, t: (b, 0, t, 0)
    return pl.pallas_call(
        _in_lat_kernel,
        grid=(batch, tpb),
        in_specs=[
            pl.BlockSpec((tm, D_MODEL), tok),
            pl.BlockSpec((1, 6, D_MODEL), lambda b, t: (b, 0, 0)),
            pl.BlockSpec((1, D_MODEL), const),
            pl.BlockSpec(wqkv.shape, const, pipeline_mode=one),
            pl.BlockSpec(wconv.shape, const, pipeline_mode=one),
            pl.BlockSpec((1, Q_RANK), const),
            pl.BlockSpec(wqb.shape, const, pipeline_mode=one),
            pl.BlockSpec((1, KV_RANK), const),
            pl.BlockSpec(wkvb.shape, const, pipeline_mode=one),
            pl.BlockSpec((tm, LANES), lambda b, t: (t, 0)),
            pl.BlockSpec((tm, LANES), lambda b, t: (t, 0)),
        ],
        out_specs=[
            pl.BlockSpec((1, N_HEADS, tm, QK_DIM), hd4),
            pl.BlockSpec((1, N_HEADS, tm, QK_DIM), hd4),
            pl.BlockSpec((1, N_HEADS, tm, HEAD), hd4),
            pl.BlockSpec((tm, CONV_W), tok),
            pl.BlockSpec((tm, CONV_W), tok),
        ],
        out_shape=[
            jax.ShapeDtypeStruct((batch, N_HEADS, seq, QK_DIM), BF16),
            jax.ShapeDtypeStruct((batch, N_HEADS, seq, QK_DIM), BF16),
            jax.ShapeDtypeStruct((batch, N_HEADS, seq, HEAD), BF16),
            jax.ShapeDtypeStruct((batch * seq, CONV_W), F32),
            jax.ShapeDtypeStruct((batch * seq, CONV_W), F32),
        ],
        compiler_params=pltpu.CompilerParams(
            dimension_semantics=("parallel", "parallel"), vmem_limit_bytes=VMEM_LIMIT),
        name="in_lat",
    )(x2, mod_l, gmix, wqkv, wconv, gq, wqb, gkv, wkvb, cq, sq)


def _in_ctx_kernel(x_ref, mod_ref, gmix_ref, wkv_ref, gkv_ref, wkvb_ref, k_ref, v_ref):
    sh = mod_ref[0, 0:1, :]
    sc = mod_ref[0, 1:2, :]
    h = (_rms(x_ref[0], gmix_ref[...]) * (1.0 + sc) + sh).astype(BF16)
    p1 = _dot(h, wkv_ref[:, Q_RANK:])
    kva = p1[:, :KV_RANK]
    kr = p1[:, KV_RANK:KV_RANK + QK_ROPE].astype(BF16)
    kv = _dot(_rms(kva, gkv_ref[...]).astype(BF16), wkvb_ref[...]).astype(BF16)
    for hd in range(N_HEADS):
        k_ref[0, hd, :, 0:QK_NOPE] = kv[:, hd * 2 * HEAD:hd * 2 * HEAD + QK_NOPE]
        k_ref[0, hd, :, QK_NOPE:QK_DIM] = kr
        v_ref[0, hd, :, :] = kv[:, hd * 2 * HEAD + QK_NOPE:(hd + 1) * 2 * HEAD]


def _in_ctx(ctx, mod_c, gmix, wkv, gkv, wkvb):
    batch, n_ctx, _ = ctx.shape
    const = lambda b: (0, 0)
    return pl.pallas_call(
        _in_ctx_kernel,
        grid=(batch,),
        in_specs=[
            pl.BlockSpec((1, n_ctx, D_MODEL), lambda b: (b, 0, 0)),
            pl.BlockSpec((1, 6, D_MODEL), lambda b: (0, 0, 0)),
            pl.BlockSpec((1, D_MODEL), const),
            pl.BlockSpec(wkv.shape, const),
            pl.BlockSpec((1, KV_RANK), const),
            pl.BlockSpec(wkvb.shape, const),
        ],
        out_specs=[
            pl.BlockSpec((1, N_HEADS, n_ctx, QK_DIM), lambda b: (b, 0, 0, 0)),
            pl.BlockSpec((1, N_HEADS, n_ctx, HEAD), lambda b: (b, 0, 0, 0)),
        ],
        out_shape=[
            jax.ShapeDtypeStruct((batch, N_HEADS, n_ctx, QK_DIM), BF16),
            jax.ShapeDtypeStruct((batch, N_HEADS, n_ctx, HEAD), BF16),
        ],
        compiler_params=pltpu.CompilerParams(
            dimension_semantics=("parallel",), vmem_limit_bytes=VMEM_LIMIT),
        name="in_ctx",
    )(ctx, mod_c, gmix, wkv, gkv, wkvb)


UP_LANE_BLKS = N_SUB * 2 * BPS
DN_ROW_BLKS = N_SUB * BPS


def _up_src(c):
    c_in = jnp.minimum(c, UP_LANE_BLKS - 1)
    blk = (c_in // (2 * BPS)) * BPS + (c_in % (2 * BPS)) // 2
    src = (c_in % 2) * FFN_BLKS + jnp.minimum(blk, FFN_BLKS - 1)
    return src, (c < UP_LANE_BLKS) & (blk < FFN_BLKS)


def _attn_kernel(*refs, n_up, n_dn):
    q_ref, kl_ref, vl_ref, kc_ref, vc_ref = refs[:5]
    wu_refs = refs[5:5 + n_up]
    wd_refs = refs[5 + n_up:5 + n_up + n_dn]
    wo_ref = refs[5 + n_up + n_dn]
    o_ref, wup_ref, wdn_ref, wout_ref = refs[6 + n_up + n_dn:]

    nt = (((1,), (1,)), ((), ()))
    for j in range(TQ // TQ_SUB):
        rows = slice(j * TQ_SUB, (j + 1) * TQ_SUB)
        q = q_ref[0, 0, rows, :]
        s_l = lax.dot_general(q, kl_ref[0, 0], nt, preferred_element_type=F32)
        s_c = lax.dot_general(q, kc_ref[0, 0], nt, preferred_element_type=F32)
        m = jnp.maximum(jnp.max(s_l, axis=-1, keepdims=True), jnp.max(s_c, axis=-1, keepdims=True))
        p_l = jnp.exp2(s_l - m)
        p_c = jnp.exp2(s_c - m)
        denom = jnp.sum(p_l, axis=-1, keepdims=True) + jnp.sum(p_c, axis=-1, keepdims=True)
        o = _dot(p_l.astype(BF16), vl_ref[0, 0]) + _dot(p_c.astype(BF16), vc_ref[0, 0])
        o_ref[0, rows, :] = o / denom

    step = ((pl.program_id(0) * pl.num_programs(1) + pl.program_id(1)) * pl.num_programs(2)
            + pl.program_id(2))
    for i in range(n_up):
        _, real = _up_src(step * n_up + i)
        wup_ref[:, i * LANES:(i + 1) * LANES] = jnp.where(real, wu_refs[i][...], 0.0).astype(BF16)
    dn_blk = jnp.minimum(step, DN_ROW_BLKS // n_dn - 1) * n_dn
    for i in range(n_dn):
        wdn_ref[i * LANES:(i + 1) * LANES, :] = jnp.where(dn_blk + i < FFN_BLKS, wd_refs[i][...], 0.0).astype(BF16)
    wout_ref[...] = wo_ref[...].astype(BF16)


def _attn(q, k_l, v_l, k_c, v_c, w_up, w_down, w_out):
    batch, heads, seq, _ = q.shape
    n_ctx = k_c.shape[2]
    nq = seq // TQ
    n_steps = batch * heads * nq
    n_up = -(-UP_LANE_BLKS // n_steps)
    n_dn = -(-DN_ROW_BLKS // n_steps)
    assert DN_ROW_BLKS % n_dn == 0 and D_MODEL % n_steps == 0
    wo_rows = D_MODEL // n_steps
    step = lambda b, h, i: (b * heads + h) * nq + i
    kv_map = lambda b, h, i: (b, h, 0, 0)
    dn_out = lambda b, h, i: jnp.minimum(step(b, h, i), DN_ROW_BLKS // n_dn - 1)
    up_specs = [pl.BlockSpec((D_MODEL, LANES),
                             functools.partial(lambda b, h, i, k: (0, _up_src(step(b, h, i) * n_up + k)[0]), k=k))
                for k in range(n_up)]
    dn_specs = [pl.BlockSpec((LANES, D_MODEL),
                             functools.partial(
                                 lambda b, h, i, k: (jnp.minimum(dn_out(b, h, i) * n_dn + k, FFN_BLKS - 1), 0), k=k))
                for k in range(n_dn)]
    return pl.pallas_call(
        functools.partial(_attn_kernel, n_up=n_up, n_dn=n_dn),
        grid=(batch, heads, nq),
        in_specs=[
            pl.BlockSpec((1, 1, TQ, QK_DIM), lambda b, h, i: (b, h, i, 0)),
            pl.BlockSpec((1, 1, seq, QK_DIM), kv_map),
            pl.BlockSpec((1, 1, seq, HEAD), kv_map),
            pl.BlockSpec((1, 1, n_ctx, QK_DIM), kv_map),
            pl.BlockSpec((1, 1, n_ctx, HEAD), kv_map),
            *up_specs,
            *dn_specs,
            pl.BlockSpec((wo_rows, D_MODEL), lambda b, h, i: (step(b, h, i), 0)),
        ],
        out_specs=[
            pl.BlockSpec((1, TQ, HEAD), lambda b, h, i: (b, i, h)),
            pl.BlockSpec((D_MODEL, n_up * LANES), lambda b, h, i: (0, step(b, h, i))),
            pl.BlockSpec((n_dn * LANES, D_MODEL), lambda b, h, i: (dn_out(b, h, i), 0)),
            pl.BlockSpec((wo_rows, D_MODEL), lambda b, h, i: (step(b, h, i), 0)),
        ],
        out_shape=[
            jax.ShapeDtypeStruct((batch, seq, heads * HEAD), F32),
            jax.ShapeDtypeStruct((D_MODEL, n_steps * n_up * LANES), BF16),
            jax.ShapeDtypeStruct((FFN_PAD, D_MODEL), BF16),
            jax.ShapeDtypeStruct((D_MODEL, D_MODEL), BF16),
        ],
        compiler_params=pltpu.CompilerParams(
            dimension_semantics=("arbitrary", "arbitrary", "arbitrary"),
            vmem_limit_bytes=VMEM_LIMIT),
        name="attn",
    )(q, k_l, v_l, k_c, v_c, *([w_up] * n_up), *([w_down] * n_dn), w_out)


def _out_kernel(att_ref, cb_ref, z_ref, zp_ref, zn_ref, cw_ref, cbias_ref, gmo_ref, wout_ref,
                x_ref, mod_ref, gffn_ref, x1_ref, hf_ref, mix_ref):
    t = pl.program_id(1)
    tm = z_ref.shape[0]
    z = z_ref[...]
    row = lax.broadcasted_iota(jnp.int32, z.shape, 0)
    z_before = jnp.where(t == 0, 0.0, zp_ref[0, 7:8, :])
    z_after = jnp.where(t == pl.num_programs(1) - 1, 0.0, zn_ref[0, 0:1, :])
    z_prev = jnp.where(row == 0, z_before, pltpu.roll(z, 1, 0))
    z_next = jnp.where(row == tm - 1, z_after, pltpu.roll(z, tm - 1, 0))
    conv = cb_ref[...] * (z_prev * cw_ref[0:1, :] + z * cw_ref[1:2, :] + z_next * cw_ref[2:3, :]
                          + cbias_ref[...])

    n_att = att_ref.shape[1] // HEAD
    gt_a = mod_ref[0, 2:3, :]
    ffn_gain = gffn_ref[...] * (1.0 + mod_ref[0, 4:5, :])
    ffn_shift = mod_ref[0, 3:4, :]
    for r in range(0, tm, NORM_ROWS):
        rows = slice(r, r + NORM_ROWS)
        for g in range(D_MODEL // HEAD):
            cols = slice(g * HEAD, (g + 1) * HEAD)
            blk = att_ref[rows, cols] if g < n_att else conv[rows, (g - n_att) * HEAD:(g - n_att + 1) * HEAD]
            mix_ref[rows, cols] = _rms(blk, gmo_ref[:, cols]).astype(BF16)
    x1 = x_ref[...] + gt_a * _dot(mix_ref[...], wout_ref[...])
    x1_ref[...] = x1
    hf_ref[...] = (_rms(x1, ffn_gain) + ffn_shift).astype(BF16)


def _out(att2, cb, z, cw, cbias, gmo, wout, x2, mod_l, gffn, batch, seq):
    tm = TM_OUT
    tpb = seq // tm
    n8 = batch * seq // 8
    z3 = z.reshape(n8, 8, CONV_W)
    const = lambda b, t: (0, 0)
    tok = lambda b, t: (b * tpb + t, 0)
    return pl.pallas_call(
        _out_kernel,
        grid=(batch, tpb),
        in_specs=[
            pl.BlockSpec((tm, att2.shape[1]), tok),
            pl.BlockSpec((tm, CONV_W), tok),
            pl.BlockSpec((tm, CONV_W), tok),
            pl.BlockSpec((1, 8, CONV_W), lambda b, t: (jnp.maximum((b * tpb + t) * (tm // 8) - 1, 0), 0, 0)),
            pl.BlockSpec((1, 8, CONV_W), lambda b, t: (jnp.minimum((b * tpb + t + 1) * (tm // 8), n8 - 1), 0, 0)),
            pl.BlockSpec((3, CONV_W), const),
            pl.BlockSpec((1, CONV_W), const),
            pl.BlockSpec((1, D_MODEL), const),
            pl.BlockSpec(wout.shape, const, pipeline_mode=pl.Buffered(1)),
            pl.BlockSpec((tm, D_MODEL), tok),
            pl.BlockSpec((1, 6, D_MODEL), lambda b, t: (b, 0, 0)),
            pl.BlockSpec((1, D_MODEL), const),
        ],
        out_specs=[
            pl.BlockSpec((tm, D_MODEL), tok),
            pl.BlockSpec((tm, D_MODEL), tok),
        ],
        out_shape=[
            jax.ShapeDtypeStruct((batch * seq, D_MODEL), F32),
            jax.ShapeDtypeStruct((batch * seq, D_MODEL), BF16),
        ],
        scratch_shapes=[pltpu.VMEM((tm, D_MODEL), BF16)],
        compiler_params=pltpu.CompilerParams(
            dimension_semantics=("parallel", "parallel"), vmem_limit_bytes=VMEM_LIMIT),
        name="out_proj",
    )(att2, cb, z, z3, z3, cw, cbias, gmo, wout, x2, mod_l, gffn)


def _ffn_kernel(h_ref, hp_ref, hn_ref, wup_ref, cw_ref, cbias_ref, wdn_ref, x1_ref, mod_ref, gfin_ref,
                o_ref, hbuf_ref, u_ref):
    t = pl.program_id(1)
    f = pl.program_id(2)
    tm = h_ref.shape[0]

    @pl.when(f == 0)
    def _():
        hbuf_ref[HALO:HALO + tm, :] = h_ref[...]
        hbuf_ref[0:HALO, :] = jnp.where(t == 0, jnp.zeros_like(hp_ref[0]), hp_ref[0])
        hbuf_ref[HALO + tm:, :] = jnp.where(t == pl.num_programs(1) - 1, jnp.zeros_like(hn_ref[0]), hn_ref[0])
        o_ref[...] = jnp.zeros_like(o_ref)

    u_ref[...] = _dot(hbuf_ref[...], wup_ref[...])
    cw = cw_ref[f]
    y = (u_ref[pl.ds(HALO - 1, tm), :] * cw[0:1, :] + u_ref[pl.ds(HALO, tm), :] * cw[1:2, :]
         + u_ref[pl.ds(HALO + 1, tm), :] * cw[2:3, :] + cbias_ref[f])
    act = jnp.concatenate(
        [y[:, 2 * j * LANES:(2 * j + 1) * LANES] * _silu(y[:, (2 * j + 1) * LANES:(2 * j + 2) * LANES])
         for j in range(BPS)], axis=1).astype(BF16)
    o_ref[...] += _dot(act, wdn_ref[...])

    @pl.when(f == pl.num_programs(2) - 1)
    def _():
        y = x1_ref[...] + mod_ref[0, 5:6, :] * o_ref[...]
        o_ref[...] = _rms(y, gfin_ref[...])


def _ffn(hf, wup_t, cw_t, cbias_t, wdn, x1, mod_l, gfin, batch, seq):
    tm = TM_FFN
    tpb = seq // tm
    nf = N_SUB // FFN_SUBS
    tf = FFN_SUBS * TF_SUB
    n_h = batch * seq // HALO
    hf3 = hf.reshape(n_h, HALO, D_MODEL)
    tok = lambda b, t, f: (b * tpb + t, 0)
    return pl.pallas_call(
        _ffn_kernel,
        grid=(batch, tpb, nf),
        in_specs=[
            pl.BlockSpec((tm, D_MODEL), tok),
            pl.BlockSpec((1, HALO, D_MODEL),
                         lambda b, t, f: (jnp.maximum((b * tpb + t) * (tm // HALO) - 1, 0), 0, 0)),
            pl.BlockSpec((1, HALO, D_MODEL),
                         lambda b, t, f: (jnp.minimum((b * tpb + t + 1) * (tm // HALO), n_h - 1), 0, 0)),
            pl.BlockSpec((D_MODEL, 2 * TF_SUB), lambda b, t, f: (0, f)),
            pl.BlockSpec((nf, 3, 2 * TF_SUB), lambda b, t, f: (0, 0, 0)),
            pl.BlockSpec((nf, 1, 2 * TF_SUB), lambda b, t, f: (0, 0, 0)),
            pl.BlockSpec((tf, D_MODEL), lambda b, t, f: (f, 0)),
            pl.BlockSpec((tm, D_MODEL), tok),
            pl.BlockSpec((1, 6, D_MODEL), lambda b, t, f: (b, 0, 0)),
            pl.BlockSpec((1, D_MODEL), lambda b, t, f: (0, 0)),
        ],
        out_specs=pl.BlockSpec((tm, D_MODEL), tok),
        out_shape=jax.ShapeDtypeStruct((batch * seq, D_MODEL), F32),
        scratch_shapes=[
            pltpu.VMEM((tm + 2 * HALO, D_MODEL), BF16),
            pltpu.VMEM((tm + 2 * HALO, 2 * TF_SUB), F32),
        ],
        compiler_params=pltpu.CompilerParams(
            dimension_semantics=("parallel", "parallel", "arbitrary"), vmem_limit_bytes=VMEM_LIMIT),
        name="ffn",
    )(hf, hf3, hf3, wup_t, cw_t, cbias_t, wdn, x1, mod_l, gfin)


def _rope_tables(seq):
    rows = seq // GRID_W
    row = jnp.repeat(jnp.arange(rows), GRID_W).astype(F32)
    col = jnp.tile(jnp.arange(GRID_W), rows).astype(F32)
    axis_dim = QK_ROPE // 2
    inv = ROPE_BASE ** (-jnp.arange(0, axis_dim, 2, dtype=F32) / axis_dim)
    ang = jnp.concatenate([row[:, None] * inv, col[:, None] * inv], axis=-1)
    cos, sin = jnp.cos(ang), jnp.sin(ang)
    c = jnp.repeat(cos, 2, axis=-1)
    s = jnp.stack([-sin, sin], axis=-1).reshape(seq, QK_ROPE)
    return jnp.tile(c, (1, LANES // QK_ROPE)), jnp.tile(s, (1, LANES // QK_ROPE))


def _tile_ffn_cols(a):
    pad = ((0, 0), (0, FFN_PAD - FFN_DIM))
    val = jnp.pad(a[:, :FFN_DIM], pad).reshape(a.shape[0], N_SUB, BPS, 1, LANES)
    gate = jnp.pad(a[:, FFN_DIM:], pad).reshape(a.shape[0], N_SUB, BPS, 1, LANES)
    both = jnp.concatenate([val, gate], axis=3).reshape(a.shape[0], N_SUB, 2 * TF_SUB)
    return jnp.moveaxis(both, 1, 0)


N_QKV = Q_RANK + KV_RANK + QK_ROPE
N_QKV_PAD = Q_RANK + KV_RANK + LANES


PREP_ROWS = 512


def _prep_qkv_kernel(w_ref, o_ref):
    o_ref[...] = w_ref[...].T.astype(BF16)


def _prep_conv_kernel(lo_ref, hi_ref, o_ref):
    off = N_QKV % PREP_ROWS
    o_ref[...] = jnp.concatenate([lo_ref[off:, :], hi_ref[:off, :]], axis=0).T.astype(BF16)


def _prep_in(w_in_t):
    wqkv = pl.pallas_call(
        _prep_qkv_kernel,
        grid=(N_QKV_PAD // LANES,),
        in_specs=[pl.BlockSpec((LANES, D_MODEL), lambda j: (j, 0))],
        out_specs=pl.BlockSpec((D_MODEL, LANES), lambda j: (0, j)),
        out_shape=jax.ShapeDtypeStruct((D_MODEL, N_QKV_PAD), BF16),
        compiler_params=pltpu.CompilerParams(
            dimension_semantics=("parallel",), vmem_limit_bytes=VMEM_LIMIT),
        name="prep_qkv",
    )(w_in_t)
    first = N_QKV // PREP_ROWS
    wconv = pl.pallas_call(
        _prep_conv_kernel,
        grid=(3 * CONV_W // PREP_ROWS,),
        in_specs=[
            pl.BlockSpec((PREP_ROWS, D_MODEL), lambda j: (first + j, 0)),
            pl.BlockSpec((PREP_ROWS, D_MODEL), lambda j: (first + j + 1, 0)),
        ],
        out_specs=pl.BlockSpec((D_MODEL, PREP_ROWS), lambda j: (0, j)),
        out_shape=jax.ShapeDtypeStruct((D_MODEL, 3 * CONV_W), BF16),
        compiler_params=pltpu.CompilerParams(
            dimension_semantics=("parallel",), vmem_limit_bytes=VMEM_LIMIT),
        name="prep_conv",
    )(w_in_t, w_in_t)
    return wqkv, wconv


def kernel(x, c, ctx, c_ctx, w_ada, b_ada, g_mix_norm, w_in, g_q_a, w_q_b, g_kv_a, w_kv_b, conv_w, conv_b, g_mix_out, w_out, g_ffn_norm, w_up, ffn_conv_w, ffn_conv_b, w_down, g_final):
    batch, seq, d = x.shape
    assert d == D_MODEL and w_ada.shape[0] == 1
    x2 = x.reshape(batch * seq, d)

    c8 = jnp.concatenate([c, c_ctx[None], jnp.zeros((8 - batch - 1, d), F32)], axis=0)
    mod = _ada(c8, w_ada[0], b_ada[0][None])
    mod_l = mod[:batch].reshape(batch, 6, d)
    mod_c = mod[batch:batch + 1].reshape(1, 6, d)

    wqkv, wconv = _prep_in(w_in[0].T)
    wqb3 = w_q_b[0].reshape(Q_RANK, N_HEADS, QK_DIM)
    wqb = jnp.concatenate([wqb3[:, :, :QK_NOPE].reshape(Q_RANK, -1),
                           wqb3[:, :, QK_NOPE:].reshape(Q_RANK, -1)], axis=-1).astype(BF16)
    wkvb = w_kv_b[0].astype(BF16)
    cw_t = _tile_ffn_cols(ffn_conv_w[0])
    cbias_t = _tile_ffn_cols(ffn_conv_b)

    cq, sq = _rope_tables(seq)

    q, k_l, v_l, cb, z = _in_lat(x2, mod_l, g_mix_norm, wqkv, wconv, g_q_a, wqb, g_kv_a, wkvb,
                                 cq, sq, batch, seq)
    k_c, v_c = _in_ctx(ctx, mod_c, g_mix_norm, wqkv, g_kv_a, wkvb)
    att, wup_t, wdn, wout = _attn(q, k_l, v_l, k_c, v_c, w_up[0], w_down[0], w_out[0])
    x1, hf = _out(att.reshape(batch * seq, N_HEADS * HEAD), cb, z, conv_w[0], conv_b, g_mix_out,
                  wout, x2, mod_l, g_ffn_norm, batch, seq)
    y = _ffn(hf, wup_t, cw_t, cbias_t, wdn, x1, mod_l, g_final[None], batch, seq)
    return y.reshape(batch, seq, d)
```

```python
import functools

import jax
import jax.numpy as jnp
from jax import lax
from jax.experimental import pallas as pl
from jax.experimental.pallas import tpu as pltpu

F32 = jnp.float32
BF16 = jnp.bfloat16

D_MODEL = 2048
N_HEADS = 8
HEAD = 128
QK_NOPE = 128
QK_ROPE = 64
QK_DIM = QK_NOPE + QK_ROPE
Q_RANK = 512
KV_RANK = 256
CONV_W = 1024
GRID_W = 64
ROPE_BASE = 10000.0
EPS = 1e-6
LOG2E = 1.4426950408889634
FFN_DIM = 5504

V7X_VMEM_BYTES = 64 * 1024 * 1024
VMEM_LIMIT = V7X_VMEM_BYTES - 8 * 1024 * 1024

TM_IN = 512
TQ = 2048
TQ_SUB = 256
TM_OUT = 512
NORM_ROWS = 64
TM_FFN = 512
LANES = 128
TF_SUB = 512
FFN_SUBS = 1
HALO = 16
BPS = TF_SUB // LANES
FFN_BLKS = FFN_DIM // LANES
N_SUB = (FFN_DIM + TF_SUB - 1) // TF_SUB
FFN_PAD = N_SUB * TF_SUB


def _rms(x, g):
    return x * lax.rsqrt(jnp.mean(x * x, axis=-1, keepdims=True) + EPS) * g


def _silu(x):
    h = 0.5 * x
    return h * (1.0 + jnp.tanh(h))


def _dot(a, b):
    return jnp.dot(a, b, preferred_element_type=F32)


def _rope(x, c, s):
    w = x.shape[-1]
    lane = lax.broadcasted_iota(jnp.int32, x.shape, 1)
    from_right = pltpu.roll(x, w - 1, 1)
    from_left = pltpu.roll(x, 1, 1)
    partner = jnp.where(lane % 2 == 0, from_right, from_left)
    return x * c + partner * s


def _ada_kernel(c_ref, w_ref, b_ref, o_ref):
    s = _silu(c_ref[...]).astype(BF16)
    o_ref[...] = _dot(s, w_ref[...].astype(BF16)) + b_ref[...]


def _ada(c8, w, b):
    n = w.shape[1]
    tn = 1024
    return pl.pallas_call(
        _ada_kernel,
        grid=(n // tn,),
        in_specs=[
            pl.BlockSpec((8, D_MODEL), lambda j: (0, 0)),
            pl.BlockSpec((D_MODEL, tn), lambda j: (0, j)),
            pl.BlockSpec((1, tn), lambda j: (0, j)),
        ],
        out_specs=pl.BlockSpec((8, tn), lambda j: (0, j)),
        out_shape=jax.ShapeDtypeStruct((8, n), F32),
        compiler_params=pltpu.CompilerParams(
            dimension_semantics=("parallel",), vmem_limit_bytes=VMEM_LIMIT),
        name="ada",
    )(c8, w, b)


def _in_lat_kernel(x_ref, mod_ref, gmix_ref, wqkv_ref, wconv_ref, gq_ref, wqb_ref, gkv_ref,
                   wkvb_ref, cq_ref, sq_ref, q_ref, k_ref, v_ref, cb_ref, z_ref):
    sh = mod_ref[0, 0:1, :]
    sc = mod_ref[0, 1:2, :]
    h = (_rms(x_ref[...], gmix_ref[...]) * (1.0 + sc) + sh).astype(BF16)

    p2 = _dot(h, wconv_ref[...])
    cb_ref[...] = p2[:, :CONV_W]
    z_ref[...] = p2[:, CONV_W:2 * CONV_W] * p2[:, 2 * CONV_W:]

    p1 = _dot(h, wqkv_ref[...])
    qa = p1[:, :Q_RANK]
    kva = p1[:, Q_RANK:Q_RANK + KV_RANK]
    kr = p1[:, Q_RANK + KV_RANK:]

    c1 = cq_ref[...]
    s1 = sq_ref[...]
    reps = N_HEADS * QK_ROPE // LANES
    cq = jnp.concatenate([c1] * reps, axis=1)
    sq = jnp.concatenate([s1] * reps, axis=1)
    scale = QK_DIM ** -0.5 * LOG2E
    q = _dot(_rms(qa, gq_ref[...]).astype(BF16), wqb_ref[...])
    qn = (q[:, :N_HEADS * QK_NOPE] * scale).astype(BF16)
    qr = (_rope(q[:, N_HEADS * QK_NOPE:], cq, sq) * scale).astype(BF16)

    kv = _dot(_rms(kva, gkv_ref[...]).astype(BF16), wkvb_ref[...]).astype(BF16)
    krr = _rope(kr, c1, s1)[:, :QK_ROPE].astype(BF16)
    for hd in range(N_HEADS):
        q_ref[0, hd, :, 0:QK_NOPE] = qn[:, hd * QK_NOPE:(hd + 1) * QK_NOPE]
        q_ref[0, hd, :, QK_NOPE:QK_DIM] = qr[:, hd * QK_ROPE:(hd + 1) * QK_ROPE]
        k_ref[0, hd, :, 0:QK_NOPE] = kv[:, hd * 2 * HEAD:hd * 2 * HEAD + QK_NOPE]
        k_ref[0, hd, :, QK_NOPE:QK_DIM] = krr
        v_ref[0, hd, :, :] = kv[:, hd * 2 * HEAD + QK_NOPE:(hd + 1) * 2 * HEAD]


def _in_lat(x2, mod_l, gmix, wqkv, wconv, gq, wqb, gkv, wkvb, cq, sq, batch, seq):
    tm = TM_IN
    tpb = seq // tm
    const = lambda b, t: (0, 0)
    one = pl.Buffered(1)
    tok = lambda b, t: (b * tpb + t, 0)
    hd4 = lambda b, t: (b, 0, t, 0)
    return pl.pallas_call(
        _in_lat_kernel,
        grid=(batch, tpb),
        in_specs=[
            pl.BlockSpec((tm, D_MODEL), tok),
            pl.BlockSpec((1, 6, D_MODEL), lambda b, t: (b, 0, 0)),
            pl.BlockSpec((1, D_MODEL), const),
            pl.BlockSpec(wqkv.shape, const, pipeline_mode=one),
            pl.BlockSpec(wconv.shape, const, pipeline_mode=one),
            pl.BlockSpec((1, Q_RANK), const),
            pl.BlockSpec(wqb.shape, const, pipeline_mode=one),
            pl.BlockSpec((1, KV_RANK), const),
            pl.BlockSpec(wkvb.shape, const, pipeline_mode=one),
            pl.BlockSpec((tm, LANES), lambda b, t: (t, 0)),
            pl.BlockSpec((tm, LANES), lambda b, t: (t, 0)),
        ],
        out_specs=[
            pl.BlockSpec((1, N_HEADS, tm, QK_DIM), hd4),
            pl.BlockSpec((1, N_HEADS, tm, QK_DIM), hd4),
            pl.BlockSpec((1, N_HEADS, tm, HEAD), hd4),
            pl.BlockSpec((tm, CONV_W), tok),
            pl.BlockSpec((tm, CONV_W), tok),
        ],
        out_shape=[
            jax.ShapeDtypeStruct((batch, N_HEADS, seq, QK_DIM), BF16),
            jax.ShapeDtypeStruct((batch, N_HEADS, seq, QK_DIM), BF16),
            jax.ShapeDtypeStruct((batch, N_HEADS, seq, HEAD), BF16),
            jax.ShapeDtypeStruct((batch * seq, CONV_W), F32),
            jax.ShapeDtypeStruct((batch * seq, CONV_W), F32),
        ],
        compiler_params=pltpu.CompilerParams(
            dimension_semantics=("parallel", "parallel"), vmem_limit_bytes=VMEM_LIMIT),
        name="in_lat",
    )(x2, mod_l, gmix, wqkv, wconv, gq, wqb, gkv, wkvb, cq, sq)


def _in_ctx_kernel(x_ref, mod_ref, gmix_ref, wkv_ref, gkv_ref, wkvb_ref, k_ref, v_ref):
    sh = mod_ref[0, 0:1, :]
    sc = mod_ref[0, 1:2, :]
    h = (_rms(x_ref[0], gmix_ref[...]) * (1.0 + sc) + sh).astype(BF16)
    p1 = _dot(h, wkv_ref[:, Q_RANK:])
    kva = p1[:, :KV_RANK]
    kr = p1[:, KV_RANK:KV_RANK + QK_ROPE].astype(BF16)
    kv = _dot(_rms(kva, gkv_ref[...]).astype(BF16), wkvb_ref[...]).astype(BF16)
    for hd in range(N_HEADS):
        k_ref[0, hd, :, 0:QK_NOPE] = kv[:, hd * 2 * HEAD:hd * 2 * HEAD + QK_NOPE]
        k_ref[0, hd, :, QK_NOPE:QK_DIM] = kr
        v_ref[0, hd, :, :] = kv[:, hd * 2 * HEAD + QK_NOPE:(hd + 1) * 2 * HEAD]


def _in_ctx(ctx, mod_c, gmix, wkv, gkv, wkvb):
    batch, n_ctx, _ = ctx.shape
    const = lambda b: (0, 0)
    return pl.pallas_call(
        _in_ctx_kernel,
        grid=(batch,),
        in_specs=[
            pl.BlockSpec((1, n_ctx, D_MODEL), lambda b: (b, 0, 0)),
            pl.BlockSpec((1, 6, D_MODEL), lambda b: (0, 0, 0)),
            pl.BlockSpec((1, D_MODEL), const),
            pl.BlockSpec(wkv.shape, const),
            pl.BlockSpec((1, KV_RANK), const),
            pl.BlockSpec(wkvb.shape, const),
        ],
        out_specs=[
            pl.BlockSpec((1, N_HEADS, n_ctx, QK_DIM), lambda b: (b, 0, 0, 0)),
            pl.BlockSpec((1, N_HEADS, n_ctx, HEAD), lambda b: (b, 0, 0, 0)),
        ],
        out_shape=[
            jax.ShapeDtypeStruct((batch, N_HEADS, n_ctx, QK_DIM), BF16),
            jax.ShapeDtypeStruct((batch, N_HEADS, n_ctx, HEAD), BF16),
        ],
        compiler_params=pltpu.CompilerParams(
            dimension_semantics=("parallel",), vmem_limit_bytes=VMEM_LIMIT),
        name="in_ctx",
    )(ctx, mod_c, gmix, wkv, gkv, wkvb)


UP_LANE_BLKS = N_SUB * 2 * BPS
DN_ROW_BLKS = N_SUB * BPS


def _up_src(c):
    c_in = jnp.minimum(c, UP_LANE_BLKS - 1)
    blk = (c_in // (2 * BPS)) * BPS + (c_in % (2 * BPS)) // 2
    src = (c_in % 2) * FFN_BLKS + jnp.minimum(blk, FFN_BLKS - 1)
    return src, (c < UP_LANE_BLKS) & (blk < FFN_BLKS)


def _attn_kernel(*refs, n_up, n_dn):
    q_ref, kl_ref, vl_ref, kc_ref, vc_ref = refs[:5]
    wu_refs = refs[5:5 + n_up]
    wd_refs = refs[5 + n_up:5 + n_up + n_dn]
    wo_ref = refs[5 + n_up + n_dn]
    o_ref, wup_ref, wdn_ref, wout_ref = refs[6 + n_up + n_dn:]

    nt = (((1,), (1,)), ((), ()))

    def with_ones(v):
        e0 = (lax.broadcasted_iota(jnp.int32, v.shape, 1) == 0).astype(BF16)
        return jnp.concatenate([v, e0], axis=1)

    vl = with_ones(vl_ref[0, 0])
    vc = with_ones(vc_ref[0, 0])
    for j in range(TQ // TQ_SUB):
        rows = slice(j * TQ_SUB, (j + 1) * TQ_SUB)
        q = q_ref[0, 0, rows, :]
        s_l = lax.dot_general(q, kl_ref[0, 0], nt, preferred_element_type=F32)
        s_c = lax.dot_general(q, kc_ref[0, 0], nt, preferred_element_type=F32)
        m = jnp.maximum(jnp.max(s_l, axis=-1, keepdims=True), jnp.max(s_c, axis=-1, keepdims=True))
        p_l = jnp.exp2(s_l - m).astype(BF16)
        p_c = jnp.exp2(s_c - m).astype(BF16)
        o = _dot(p_l, vl) + _dot(p_c, vc)
        o_ref[0, rows, :] = o[:, :HEAD] / o[:, HEAD:HEAD + 1]

    step = ((pl.program_id(0) * pl.num_programs(1) + pl.program_id(1)) * pl.num_programs(2)
            + pl.program_id(2))
    for i in range(n_up):
        _, real = _up_src(step * n_up + i)
        wup_ref[:, i * LANES:(i + 1) * LANES] = jnp.where(real, wu_refs[i][...], 0.0).astype(BF16)
    dn_blk = jnp.minimum(step, DN_ROW_BLKS // n_dn - 1) * n_dn
    for i in range(n_dn):
        wdn_ref[i * LANES:(i + 1) * LANES, :] = jnp.where(dn_blk + i < FFN_BLKS, wd_refs[i][...], 0.0).astype(BF16)
    wout_ref[...] = wo_ref[...].astype(BF16)


def _attn(q, k_l, v_l, k_c, v_c, w_up, w_down, w_out):
    batch, heads, seq, _ = q.shape
    n_ctx = k_c.shape[2]
    nq = seq // TQ
    n_steps = batch * heads * nq
    n_up = -(-UP_LANE_BLKS // n_steps)
    n_dn = -(-DN_ROW_BLKS // n_steps)
    assert DN_ROW_BLKS % n_dn == 0 and D_MODEL % n_steps == 0
    wo_rows = D_MODEL // n_steps
    step = lambda b, h, i: (b * heads + h) * nq + i
    kv_map = lambda b, h, i: (b, h, 0, 0)
    dn_out = lambda b, h, i: jnp.minimum(step(b, h, i), DN_ROW_BLKS // n_dn - 1)
    up_specs = [pl.BlockSpec((D_MODEL, LANES),
                             functools.partial(lambda b, h, i, k: (0, _up_src(step(b, h, i) * n_up + k)[0]), k=k))
                for k in range(n_up)]
    dn_specs = [pl.BlockSpec((LANES, D_MODEL),
                             functools.partial(
                                 lambda b, h, i, k: (jnp.minimum(dn_out(b, h, i) * n_dn + k, FFN_BLKS - 1), 0), k=k))
                for k in range(n_dn)]
    return pl.pallas_call(
        functools.partial(_attn_kernel, n_up=n_up, n_dn=n_dn),
        grid=(batch, heads, nq),
        in_specs=[
            pl.BlockSpec((1, 1, TQ, QK_DIM), lambda b, h, i: (b, h, i, 0)),
            pl.BlockSpec((1, 1, seq, QK_DIM), kv_map),
            pl.BlockSpec((1, 1, seq, HEAD), kv_map),
            pl.BlockSpec((1, 1, n_ctx, QK_DIM), kv_map),
            pl.BlockSpec((1, 1, n_ctx, HEAD), kv_map),
            *up_specs,
            *dn_specs,
            pl.BlockSpec((wo_rows, D_MODEL), lambda b, h, i: (step(b, h, i), 0)),
        ],
        out_specs=[
            pl.BlockSpec((1, TQ, HEAD), lambda b, h, i: (b, i, h)),
            pl.BlockSpec((D_MODEL, n_up * LANES), lambda b, h, i: (0, step(b, h, i))),
            pl.BlockSpec((n_dn * LANES, D_MODEL), lambda b, h, i: (dn_out(b, h, i), 0)),
            pl.BlockSpec((wo_rows, D_MODEL), lambda b, h, i: (step(b, h, i), 0)),
        ],
        out_shape=[
            jax.ShapeDtypeStruct((batch, seq, heads * HEAD), F32),
            jax.ShapeDtypeStruct((D_MODEL, n_steps * n_up * LANES), BF16),
            jax.ShapeDtypeStruct((FFN_PAD, D_MODEL), BF16),
            jax.ShapeDtypeStruct((D_MODEL, D_MODEL), BF16),
        ],
        compiler_params=pltpu.CompilerParams(
            dimension_semantics=("arbitrary", "arbitrary", "arbitrary"),
            vmem_limit_bytes=VMEM_LIMIT),
        name="attn",
    )(q, k_l, v_l, k_c, v_c, *([w_up] * n_up), *([w_down] * n_dn), w_out)


def _out_kernel(att_ref, cb_ref, z_ref, zp_ref, zn_ref, cw_ref, cbias_ref, gmo_ref, wout_ref,
                x_ref, mod_ref, gffn_ref, x1_ref, hf_ref, mix_ref):
    t = pl.program_id(1)
    tm = z_ref.shape[0]
    z = z_ref[...]
    row = lax.broadcasted_iota(jnp.int32, z.shape, 0)
    z_before = jnp.where(t == 0, 0.0, zp_ref[0, 7:8, :])
    z_after = jnp.where(t == pl.num_programs(1) - 1, 0.0, zn_ref[0, 0:1, :])
    z_prev = jnp.where(row == 0, z_before, pltpu.roll(z, 1, 0))
    z_next = jnp.where(row == tm - 1, z_after, pltpu.roll(z, tm - 1, 0))
    conv = cb_ref[...] * (z_prev * cw_ref[0:1, :] + z * cw_ref[1:2, :] + z_next * cw_ref[2:3, :]
                          + cbias_ref[...])

    n_att = att_ref.shape[1] // HEAD
    gt_a = mod_ref[0, 2:3, :]
    ffn_gain = gffn_ref[...] * (1.0 + mod_ref[0, 4:5, :])
    ffn_shift = mod_ref[0, 3:4, :]
    for r in range(0, tm, NORM_ROWS):
        rows = slice(r, r + NORM_ROWS)
        for g in range(D_MODEL // HEAD):
            cols = slice(g * HEAD, (g + 1) * HEAD)
            blk = att_ref[rows, cols] if g < n_att else conv[rows, (g - n_att) * HEAD:(g - n_att + 1) * HEAD]
            mix_ref[rows, cols] = _rms(blk, gmo_ref[:, cols]).astype(BF16)
    x1 = x_ref[...] + gt_a * _dot(mix_ref[...], wout_ref[...])
    x1_ref[...] = x1
    hf_ref[...] = (_rms(x1, ffn_gain) + ffn_shift).astype(BF16)


def _out(att2, cb, z, cw, cbias, gmo, wout, x2, mod_l, gffn, batch, seq):
    tm = TM_OUT
    tpb = seq // tm
    n8 = batch * seq // 8
    z3 = z.reshape(n8, 8, CONV_W)
    const = lambda b, t: (0, 0)
    tok = lambda b, t: (b * tpb + t, 0)
    return pl.pallas_call(
        _out_kernel,
        grid=(batch, tpb),
        in_specs=[
            pl.BlockSpec((tm, att2.shape[1]), tok),
            pl.BlockSpec((tm, CONV_W), tok),
            pl.BlockSpec((tm, CONV_W), tok),
            pl.BlockSpec((1, 8, CONV_W), lambda b, t: (jnp.maximum((b * tpb + t) * (tm // 8) - 1, 0), 0, 0)),
            pl.BlockSpec((1, 8, CONV_W), lambda b, t: (jnp.minimum((b * tpb + t + 1) * (tm // 8), n8 - 1), 0, 0)),
            pl.BlockSpec((3, CONV_W), const),
            pl.BlockSpec((1, CONV_W), const),
            pl.BlockSpec((1, D_MODEL), const),
            pl.BlockSpec(wout.shape, const, pipeline_mode=pl.Buffered(1)),
            pl.BlockSpec((tm, D_MODEL), tok),
            pl.BlockSpec((1, 6, D_MODEL), lambda b, t: (b, 0, 0)),
            pl.BlockSpec((1, D_MODEL), const),
        ],
        out_specs=[
            pl.BlockSpec((tm, D_MODEL), tok),
            pl.BlockSpec((tm, D_MODEL), tok),
        ],
        out_shape=[
            jax.ShapeDtypeStruct((batch * seq, D_MODEL), F32),
            jax.ShapeDtypeStruct((batch * seq, D_MODEL), BF16),
        ],
        scratch_shapes=[pltpu.VMEM((tm, D_MODEL), BF16)],
        compiler_params=pltpu.CompilerParams(
            dimension_semantics=("parallel", "parallel"), vmem_limit_bytes=VMEM_LIMIT),
        name="out_proj",
    )(att2, cb, z, z3, z3, cw, cbias, gmo, wout, x2, mod_l, gffn)


def _ffn_kernel(h_ref, hp_ref, hn_ref, wup_ref, cw_ref, cbias_ref, wdn_ref, x1_ref, mod_ref, gfin_ref,
                o_ref, hbuf_ref, u_ref):
    t = pl.program_id(1)
    f = pl.program_id(2)
    tm = h_ref.shape[0]

    @pl.when(f == 0)
    def _():
        hbuf_ref[HALO:HALO + tm, :] = h_ref[...]
        hbuf_ref[0:HALO, :] = jnp.where(t == 0, jnp.zeros_like(hp_ref[0]), hp_ref[0])
        hbuf_ref[HALO + tm:, :] = jnp.where(t == pl.num_programs(1) - 1, jnp.zeros_like(hn_ref[0]), hn_ref[0])
        o_ref[...] = jnp.zeros_like(o_ref)

    hb = hbuf_ref[...]
    for j in range(BPS):
        slab = slice(2 * j * LANES, (2 * j + 2) * LANES)
        u_ref[:, slab] = _dot(hb, wup_ref[:, slab])
    cw = cw_ref[f]
    y = (u_ref[pl.ds(HALO - 1, tm), :] * cw[0:1, :] + u_ref[pl.ds(HALO, tm), :] * cw[1:2, :]
         + u_ref[pl.ds(HALO + 1, tm), :] * cw[2:3, :] + cbias_ref[f])
    act = jnp.concatenate(
        [y[:, 2 * j * LANES:(2 * j + 1) * LANES] * _silu(y[:, (2 * j + 1) * LANES:(2 * j + 2) * LANES])
         for j in range(BPS)], axis=1).astype(BF16)
    o_ref[...] += _dot(act, wdn_ref[...])

    @pl.when(f == pl.num_programs(2) - 1)
    def _():
        y = x1_ref[...] + mod_ref[0, 5:6, :] * o_ref[...]
        o_ref[...] = _rms(y, gfin_ref[...])


def _ffn(hf, wup_t, cw_t, cbias_t, wdn, x1, mod_l, gfin, batch, seq):
    tm = TM_FFN
    tpb = seq // tm
    nf = N_SUB // FFN_SUBS
    tf = FFN_SUBS * TF_SUB
    n_h = batch * seq // HALO
    hf3 = hf.reshape(n_h, HALO, D_MODEL)
    tok = lambda b, t, f: (b * tpb + t, 0)
    return pl.pallas_call(
        _ffn_kernel,
        grid=(batch, tpb, nf),
        in_specs=[
            pl.BlockSpec((tm, D_MODEL), tok),
            pl.BlockSpec((1, HALO, D_MODEL),
                         lambda b, t, f: (jnp.maximum((b * tpb + t) * (tm // HALO) - 1, 0), 0, 0)),
            pl.BlockSpec((1, HALO, D_MODEL),
                         lambda b, t, f: (jnp.minimum((b * tpb + t + 1) * (tm // HALO), n_h - 1), 0, 0)),
            pl.BlockSpec((D_MODEL, 2 * TF_SUB), lambda b, t, f: (0, f)),
            pl.BlockSpec((nf, 3, 2 * TF_SUB), lambda b, t, f: (0, 0, 0)),
            pl.BlockSpec((nf, 1, 2 * TF_SUB), lambda b, t, f: (0, 0, 0)),
            pl.BlockSpec((tf, D_MODEL), lambda b, t, f: (f, 0)),
            pl.BlockSpec((tm, D_MODEL), tok),
            pl.BlockSpec((1, 6, D_MODEL), lambda b, t, f: (b, 0, 0)),
            pl.BlockSpec((1, D_MODEL), lambda b, t, f: (0, 0)),
        ],
        out_specs=pl.BlockSpec((tm, D_MODEL), tok),
        out_shape=jax.ShapeDtypeStruct((batch * seq, D_MODEL), F32),
        scratch_shapes=[
            pltpu.VMEM((tm + 2 * HALO, D_MODEL), BF16),
            pltpu.VMEM((tm + 2 * HALO, 2 * TF_SUB), F32),
        ],
        compiler_params=pltpu.CompilerParams(
            dimension_semantics=("parallel", "parallel", "arbitrary"), vmem_limit_bytes=VMEM_LIMIT),
        name="ffn",
    )(hf, hf3, hf3, wup_t, cw_t, cbias_t, wdn, x1, mod_l, gfin)


def _rope_tables(seq):
    rows = seq // GRID_W
    row = jnp.repeat(jnp.arange(rows), GRID_W).astype(F32)
    col = jnp.tile(jnp.arange(GRID_W), rows).astype(F32)
    axis_dim = QK_ROPE // 2
    inv = ROPE_BASE ** (-jnp.arange(0, axis_dim, 2, dtype=F32) / axis_dim)
    ang = jnp.concatenate([row[:, None] * inv, col[:, None] * inv], axis=-1)
    cos, sin = jnp.cos(ang), jnp.sin(ang)
    c = jnp.repeat(cos, 2, axis=-1)
    s = jnp.stack([-sin, sin], axis=-1).reshape(seq, QK_ROPE)
    return jnp.tile(c, (1, LANES // QK_ROPE)), jnp.tile(s, (1, LANES // QK_ROPE))


def _tile_ffn_cols(a):
    pad = ((0, 0), (0, FFN_PAD - FFN_DIM))
    val = jnp.pad(a[:, :FFN_DIM], pad).reshape(a.shape[0], N_SUB, BPS, 1, LANES)
    gate = jnp.pad(a[:, FFN_DIM:], pad).reshape(a.shape[0], N_SUB, BPS, 1, LANES)
    both = jnp.concatenate([val, gate], axis=3).reshape(a.shape[0], N_SUB, 2 * TF_SUB)
    return jnp.moveaxis(both, 1, 0)


N_QKV = Q_RANK + KV_RANK + QK_ROPE
N_QKV_PAD = Q_RANK + KV_RANK + LANES


PREP_ROWS = 512


def _prep_qkv_kernel(w_ref, o_ref):
    o_ref[...] = w_ref[...].T.astype(BF16)


def _prep_conv_kernel(lo_ref, hi_ref, o_ref):
    off = N_QKV % PREP_ROWS
    o_ref[...] = jnp.concatenate([lo_ref[off:, :], hi_ref[:off, :]], axis=0).T.astype(BF16)


def _prep_in(w_in_t):
    wqkv = pl.pallas_call(
        _prep_qkv_kernel,
        grid=(N_QKV_PAD // LANES,),
        in_specs=[pl.BlockSpec((LANES, D_MODEL), lambda j: (j, 0))],
        out_specs=pl.BlockSpec((D_MODEL, LANES), lambda j: (0, j)),
        out_shape=jax.ShapeDtypeStruct((D_MODEL, N_QKV_PAD), BF16),
        compiler_params=pltpu.CompilerParams(
            dimension_semantics=("parallel",), vmem_limit_bytes=VMEM_LIMIT),
        name="prep_qkv",
    )(w_in_t)
    first = N_QKV // PREP_ROWS
    wconv = pl.pallas_call(
        _prep_conv_kernel,
        grid=(3 * CONV_W // PREP_ROWS,),
        in_specs=[
            pl.BlockSpec((PREP_ROWS, D_MODEL), lambda j: (first + j, 0)),
            pl.BlockSpec((PREP_ROWS, D_MODEL), lambda j: (first + j + 1, 0)),
        ],
        out_specs=pl.BlockSpec((D_MODEL, PREP_ROWS), lambda j: (0, j)),
        out_shape=jax.ShapeDtypeStruct((D_MODEL, 3 * CONV_W), BF16),
        compiler_params=pltpu.CompilerParams(
            dimension_semantics=("parallel",), vmem_limit_bytes=VMEM_LIMIT),
        name="prep_conv",
    )(w_in_t, w_in_t)
    return wqkv, wconv


def kernel(x, c, ctx, c_ctx, w_ada, b_ada, g_mix_norm, w_in, g_q_a, w_q_b, g_kv_a, w_kv_b, conv_w, conv_b, g_mix_out, w_out, g_ffn_norm, w_up, ffn_conv_w, ffn_conv_b, w_down, g_final):
    batch, seq, d = x.shape
    assert d == D_MODEL and w_ada.shape[0] == 1
    x2 = x.reshape(batch * seq, d)

    c8 = jnp.concatenate([c, c_ctx[None], jnp.zeros((8 - batch - 1, d), F32)], axis=0)
    mod = _ada(c8, w_ada[0], b_ada[0][None])
    mod_l = mod[:batch].reshape(batch, 6, d)
    mod_c = mod[batch:batch + 1].reshape(1, 6, d)

    wqkv, wconv = _prep_in(w_in[0].T)
    wqb3 = w_q_b[0].reshape(Q_RANK, N_HEADS, QK_DIM)
    wqb = jnp.concatenate([wqb3[:, :, :QK_NOPE].reshape(Q_RANK, -1),
                           wqb3[:, :, QK_NOPE:].reshape(Q_RANK, -1)], axis=-1).astype(BF16)
    wkvb = w_kv_b[0].astype(BF16)
    cw_t = _tile_ffn_cols(ffn_conv_w[0])
    cbias_t = _tile_ffn_cols(ffn_conv_b)

    cq, sq = _rope_tables(seq)

    q, k_l, v_l, cb, z = _in_lat(x2, mod_l, g_mix_norm, wqkv, wconv, g_q_a, wqb, g_kv_a, wkvb,
                                 cq, sq, batch, seq)
    k_c, v_c = _in_ctx(ctx, mod_c, g_mix_norm, wqkv, g_kv_a, wkvb)
    att, wup_t, wdn, wout = _attn(q, k_l, v_l, k_c, v_c, w_up[0], w_down[0], w_out[0])
    x1, hf = _out(att.reshape(batch * seq, N_HEADS * HEAD), cb, z, conv_w[0], conv_b, g_mix_out,
                  wout, x2, mod_l, g_ffn_norm, batch, seq)
    y = _ffn(hf, wup_t, cw_t, cbias_t, wdn, x1, mod_l, g_final[None], batch, seq)
    return y.reshape(batch, seq, d)
```

```python
import functools

import jax
import jax.numpy as jnp
from jax import lax
from jax.experimental import pallas as pl
from jax.experimental.pallas import tpu as pltpu

F32 = jnp.float32
BF16 = jnp.bfloat16

D_MODEL = 2048
N_HEADS = 8
HEAD = 128
QK_NOPE = 128
QK_ROPE = 64
QK_DIM = QK_NOPE + QK_ROPE
Q_RANK = 512
KV_RANK = 256
CONV_W = 1024
GRID_W = 64
ROPE_BASE = 10000.0
EPS = 1e-6
LOG2E = 1.4426950408889634
FFN_DIM = 5504

V7X_VMEM_BYTES = 64 * 1024 * 1024
VMEM_LIMIT = V7X_VMEM_BYTES - 8 * 1024 * 1024

TM_IN = 512
TQ = 2048
TQ_SUB = 256
TM_OUT = 512
NORM_ROWS = 64
TM_FFN = 512
LANES = 128
TF_SUB = 512
FFN_SUBS = 1
HALO = 8
BPS = TF_SUB // LANES
FFN_BLKS = FFN_DIM // LANES
N_SUB = (FFN_DIM + TF_SUB - 1) // TF_SUB
FFN_PAD = N_SUB * TF_SUB


def _rms(x, g):
    return x * lax.rsqrt(jnp.mean(x * x, axis=-1, keepdims=True) + EPS) * g


def _silu(x):
    h = 0.5 * x
    return h * (1.0 + jnp.tanh(h))


def _dot(a, b):
    return jnp.dot(a, b, preferred_element_type=F32)


def _rope(x, c, s):
    w = x.shape[-1]
    lane = lax.broadcasted_iota(jnp.int32, x.shape, 1)
    from_right = pltpu.roll(x, w - 1, 1)
    from_left = pltpu.roll(x, 1, 1)
    partner = jnp.where(lane % 2 == 0, from_right, from_left)
    return x * c + partner * s


def _ada_kernel(c_ref, w_ref, b_ref, o_ref):
    s = _silu(c_ref[...]).astype(BF16)
    o_ref[...] = _dot(s, w_ref[...].astype(BF16)) + b_ref[...]


def _ada(c8, w, b):
    n = w.shape[1]
    tn = 1024
    return pl.pallas_call(
        _ada_kernel,
        grid=(n // tn,),
        in_specs=[
            pl.BlockSpec((8, D_MODEL), lambda j: (0, 0)),
            pl.BlockSpec((D_MODEL, tn), lambda j: (0, j)),
            pl.BlockSpec((1, tn), lambda j: (0, j)),
        ],
        out_specs=pl.BlockSpec((8, tn), lambda j: (0, j)),
        out_shape=jax.ShapeDtypeStruct((8, n), F32),
        compiler_params=pltpu.CompilerParams(
            dimension_semantics=("parallel",), vmem_limit_bytes=VMEM_LIMIT),
        name="ada",
    )(c8, w, b)


def _in_lat_kernel(x_ref, mod_ref, gmix_ref, wqkv_ref, wconv_ref, gq_ref, wqb_ref, gkv_ref,
                   wkvb_ref, cq_ref, sq_ref, q_ref, k_ref, v_ref, cb_ref, z_ref):
    sh = mod_ref[0, 0:1, :]
    sc = mod_ref[0, 1:2, :]
    h = (_rms(x_ref[...], gmix_ref[...]) * (1.0 + sc) + sh).astype(BF16)

    p2 = _dot(h, wconv_ref[...])
    cb_ref[...] = p2[:, :CONV_W]
    z_ref[...] = p2[:, CONV_W:2 * CONV_W] * p2[:, 2 * CONV_W:]

    p1 = _dot(h, wqkv_ref[...])
    qa = p1[:, :Q_RANK]
    kva = p1[:, Q_RANK:Q_RANK + KV_RANK]
    kr = p1[:, Q_RANK + KV_RANK:]

    c1 = cq_ref[...]
    s1 = sq_ref[...]
    reps = N_HEADS * QK_ROPE // LANES
    cq = jnp.concatenate([c1] * reps, axis=1)
    sq = jnp.concatenate([s1] * reps, axis=1)
    scale = QK_DIM ** -0.5 * LOG2E
    q = _dot(_rms(qa, gq_ref[...]).astype(BF16), wqb_ref[...])
    qn = (q[:, :N_HEADS * QK_NOPE] * scale).astype(BF16)
    qr = (_rope(q[:, N_HEADS * QK_NOPE:], cq, sq) * scale).astype(BF16)

    kv = _dot(_rms(kva, gkv_ref[...]).astype(BF16), wkvb_ref[...]).astype(BF16)
    krr = _rope(kr, c1, s1)[:, :QK_ROPE].astype(BF16)
    for hd in range(N_HEADS):
        q_ref[0, hd, :, 0:QK_NOPE] = qn[:, hd * QK_NOPE:(hd + 1) * QK_NOPE]
        q_ref[0, hd, :, QK_NOPE:QK_DIM] = qr[:, hd * QK_ROPE:(hd + 1) * QK_ROPE]
        k_ref[0, hd, :, 0:QK_NOPE] = kv[:, hd * 2 * HEAD:hd * 2 * HEAD + QK_NOPE]
        k_ref[0, hd, :, QK_NOPE:QK_DIM] = krr
        v_ref[0, hd, :, :] = kv[:, hd * 2 * HEAD + QK_NOPE:(hd + 1) * 2 * HEAD]


def _in_lat(x2, mod_l, gmix, wqkv, wconv, gq, wqb, gkv, wkvb, cq, sq, batch, seq):
    tm = TM_IN
    tpb = seq // tm
    const = lambda b, t: (0, 0)
    one = pl.Buffered(1)
    tok = lambda b, t: (b * tpb + t, 0)
    hd4 = lambda b, t: (b, 0, t, 0)
    return pl.pallas_call(
        _in_lat_kernel,
        grid=(batch, tpb),
        in_specs=[
            pl.BlockSpec((tm, D_MODEL), tok),
            pl.BlockSpec((1, 6, D_MODEL), lambda b, t: (b, 0, 0)),
            pl.BlockSpec((1, D_MODEL), const),
            pl.BlockSpec(wqkv.shape, const, pipeline_mode=one),
            pl.BlockSpec(wconv.shape, const, pipeline_mode=one),
            pl.BlockSpec((1, Q_RANK), const),
            pl.BlockSpec(wqb.shape, const, pipeline_mode=one),
            pl.BlockSpec((1, KV_RANK), const),
            pl.BlockSpec(wkvb.shape, const, pipeline_mode=one),
            pl.BlockSpec((tm, LANES), lambda b, t: (t, 0)),
            pl.BlockSpec((tm, LANES), lambda b, t: (t, 0)),
        ],
        out_specs=[
            pl.BlockSpec((1, N_HEADS, tm, QK_DIM), hd4),
            pl.BlockSpec((1, N_HEADS, tm, QK_DIM), hd4),
            pl.BlockSpec((1, N_HEADS, tm, HEAD), hd4),
            pl.BlockSpec((tm, CONV_W), tok),
            pl.BlockSpec((tm, CONV_W), tok),
        ],
        out_shape=[
            jax.ShapeDtypeStruct((batch, N_HEADS, seq, QK_DIM), BF16),
            jax.ShapeDtypeStruct((batch, N_HEADS, seq, QK_DIM), BF16),
            jax.ShapeDtypeStruct((batch, N_HEADS, seq, HEAD), BF16),
            jax.ShapeDtypeStruct((batch * seq, CONV_W), F32),
            jax.ShapeDtypeStruct((batch * seq, CONV_W), F32),
        ],
        compiler_params=pltpu.CompilerParams(
            dimension_semantics=("parallel", "parallel"), vmem_limit_bytes=VMEM_LIMIT),
        name="in_lat",
    )(x2, mod_l, gmix, wqkv, wconv, gq, wqb, gkv, wkvb, cq, sq)


def _in_ctx_kernel(x_ref, mod_ref, gmix_ref, wkv_ref, gkv_ref, wkvb_ref, k_ref, v_ref):
    sh = mod_ref[0, 0:1, :]
    sc = mod_ref[0, 1:2, :]
    h = (_rms(x_ref[0], gmix_ref[...]) * (1.0 + sc) + sh).astype(BF16)
    p1 = _dot(h, wkv_ref[:, Q_RANK:])
    kva = p1[:, :KV_RANK]
    kr = p1[:, KV_RANK:KV_RANK + QK_ROPE].astype(BF16)
    kv = _dot(_rms(kva, gkv_ref[...]).astype(BF16), wkvb_ref[...]).astype(BF16)
    for hd in range(N_HEADS):
        k_ref[0, hd, :, 0:QK_NOPE] = kv[:, hd * 2 * HEAD:hd * 2 * HEAD + QK_NOPE]
        k_ref[0, hd, :, QK_NOPE:QK_DIM] = kr
        v_ref[0, hd, :, :] = kv[:, hd * 2 * HEAD + QK_NOPE:(hd + 1) * 2 * HEAD]


def _in_ctx(ctx, mod_c, gmix, wkv, gkv, wkvb):
    batch, n_ctx, _ = ctx.shape
    const = lambda b: (0, 0)
    return pl.pallas_call(
        _in_ctx_kernel,
        grid=(batch,),
        in_specs=[
            pl.BlockSpec((1, n_ctx, D_MODEL), lambda b: (b, 0, 0)),
            pl.BlockSpec((1, 6, D_MODEL), lambda b: (0, 0, 0)),
            pl.BlockSpec((1, D_MODEL), const),
            pl.BlockSpec(wkv.shape, const),
            pl.BlockSpec((1, KV_RANK), const),
            pl.BlockSpec(wkvb.shape, const),
        ],
        out_specs=[
            pl.BlockSpec((1, N_HEADS, n_ctx, QK_DIM), lambda b: (b, 0, 0, 0)),
            pl.BlockSpec((1, N_HEADS, n_ctx, HEAD), lambda b: (b, 0, 0, 0)),
        ],
        out_shape=[
            jax.ShapeDtypeStruct((batch, N_HEADS, n_ctx, QK_DIM), BF16),
            jax.ShapeDtypeStruct((batch, N_HEADS, n_ctx, HEAD), BF16),
        ],
        compiler_params=pltpu.CompilerParams(
            dimension_semantics=("parallel",), vmem_limit_bytes=VMEM_LIMIT),
        name="in_ctx",
    )(ctx, mod_c, gmix, wkv, gkv, wkvb)


UP_LANE_BLKS = N_SUB * 2 * BPS
DN_ROW_BLKS = N_SUB * BPS


def _up_src(c):
    c_in = jnp.minimum(c, UP_LANE_BLKS - 1)
    blk = (c_in // (2 * BPS)) * BPS + (c_in % (2 * BPS)) // 2
    src = (c_in % 2) * FFN_BLKS + jnp.minimum(blk, FFN_BLKS - 1)
    return src, (c < UP_LANE_BLKS) & (blk < FFN_BLKS)


def _attn_kernel(*refs, n_up, n_dn):
    q_ref, kl_ref, vl_ref, kc_ref, vc_ref = refs[:5]
    wu_refs = refs[5:5 + n_up]
    wd_refs = refs[5 + n_up:5 + n_up + n_dn]
    wo_ref = refs[5 + n_up + n_dn]
    o_ref, wup_ref, wdn_ref, wout_ref = refs[6 + n_up + n_dn:]

    nt = (((1,), (1,)), ((), ()))

    def with_ones(v):
        e0 = (lax.broadcasted_iota(jnp.int32, v.shape, 1) == 0).astype(BF16)
        return jnp.concatenate([v, e0], axis=1)

    vl = with_ones(vl_ref[0, 0])
    vc = with_ones(vc_ref[0, 0])
    for j in range(TQ // TQ_SUB):
        rows = slice(j * TQ_SUB, (j + 1) * TQ_SUB)
        q = q_ref[0, 0, rows, :]
        s_l = lax.dot_general(q, kl_ref[0, 0], nt, preferred_element_type=F32)
        s_c = lax.dot_general(q, kc_ref[0, 0], nt, preferred_element_type=F32)
        m = jnp.maximum(jnp.max(s_l, axis=-1, keepdims=True), jnp.max(s_c, axis=-1, keepdims=True))
        p_l = jnp.exp2(s_l - m).astype(BF16)
        p_c = jnp.exp2(s_c - m).astype(BF16)
        o = _dot(p_l, vl) + _dot(p_c, vc)
        o_ref[0, rows, :] = o[:, :HEAD] / o[:, HEAD:HEAD + 1]

    step = ((pl.program_id(0) * pl.num_programs(1) + pl.program_id(1)) * pl.num_programs(2)
            + pl.program_id(2))
    for i in range(n_up):
        _, real = _up_src(step * n_up + i)
        wup_ref[:, i * LANES:(i + 1) * LANES] = jnp.where(real, wu_refs[i][...], 0.0).astype(BF16)
    dn_blk = jnp.minimum(step, DN_ROW_BLKS // n_dn - 1) * n_dn
    for i in range(n_dn):
        wdn_ref[i * LANES:(i + 1) * LANES, :] = jnp.where(dn_blk + i < FFN_BLKS, wd_refs[i][...], 0.0).astype(BF16)
    wout_ref[...] = wo_ref[...].astype(BF16)


def _attn(q, k_l, v_l, k_c, v_c, w_up, w_down, w_out):
    batch, heads, seq, _ = q.shape
    n_ctx = k_c.shape[2]
    nq = seq // TQ
    n_steps = batch * heads * nq
    n_up = -(-UP_LANE_BLKS // n_steps)
    n_dn = -(-DN_ROW_BLKS // n_steps)
    assert DN_ROW_BLKS % n_dn == 0 and D_MODEL % n_steps == 0
    wo_rows = D_MODEL // n_steps
    step = lambda b, h, i: (b * heads + h) * nq + i
    kv_map = lambda b, h, i: (b, h, 0, 0)
    dn_out = lambda b, h, i: jnp.minimum(step(b, h, i), DN_ROW_BLKS // n_dn - 1)
    up_specs = [pl.BlockSpec((D_MODEL, LANES),
                             functools.partial(lambda b, h, i, k: (0, _up_src(step(b, h, i) * n_up + k)[0]), k=k))
                for k in range(n_up)]
    dn_specs = [pl.BlockSpec((LANES, D_MODEL),
                             functools.partial(
                                 lambda b, h, i, k: (jnp.minimum(dn_out(b, h, i) * n_dn + k, FFN_BLKS - 1), 0), k=k))
                for k in range(n_dn)]
    return pl.pallas_call(
        functools.partial(_attn_kernel, n_up=n_up, n_dn=n_dn),
        grid=(batch, heads, nq),
        in_specs=[
            pl.BlockSpec((1, 1, TQ, QK_DIM), lambda b, h, i: (b, h, i, 0)),
            pl.BlockSpec((1, 1, seq, QK_DIM), kv_map),
            pl.BlockSpec((1, 1, seq, HEAD), kv_map),
            pl.BlockSpec((1, 1, n_ctx, QK_DIM), kv_map),
            pl.BlockSpec((1, 1, n_ctx, HEAD), kv_map),
            *up_specs,
            *dn_specs,
            pl.BlockSpec((wo_rows, D_MODEL), lambda b, h, i: (step(b, h, i), 0)),
        ],
        out_specs=[
            pl.BlockSpec((1, TQ, HEAD), lambda b, h, i: (b, i, h)),
            pl.BlockSpec((D_MODEL, n_up * LANES), lambda b, h, i: (0, step(b, h, i))),
            pl.BlockSpec((n_dn * LANES, D_MODEL), lambda b, h, i: (dn_out(b, h, i), 0)),
            pl.BlockSpec((wo_rows, D_MODEL), lambda b, h, i: (step(b, h, i), 0)),
        ],
        out_shape=[
            jax.ShapeDtypeStruct((batch, seq, heads * HEAD), F32),
            jax.ShapeDtypeStruct((D_MODEL, n_steps * n_up * LANES), BF16),
            jax.ShapeDtypeStruct((FFN_PAD, D_MODEL), BF16),
            jax.ShapeDtypeStruct((D_MODEL, D_MODEL), BF16),
        ],
        compiler_params=pltpu.CompilerParams(
            dimension_semantics=("arbitrary", "arbitrary", "arbitrary"),
            vmem_limit_bytes=VMEM_LIMIT),
        name="attn",
    )(q, k_l, v_l, k_c, v_c, *([w_up] * n_up), *([w_down] * n_dn), w_out)


def _out_kernel(att_ref, cb_ref, z_ref, zp_ref, zn_ref, cw_ref, cbias_ref, gmo_ref, wout_ref,
                x_ref, mod_ref, gffn_ref, x1_ref, hf_ref, mix_ref):
    t = pl.program_id(1)
    tm = z_ref.shape[0]
    z = z_ref[...]
    row = lax.broadcasted_iota(jnp.int32, z.shape, 0)
    z_before = jnp.where(t == 0, 0.0, zp_ref[0, 7:8, :])
    z_after = jnp.where(t == pl.num_programs(1) - 1, 0.0, zn_ref[0, 0:1, :])
    z_prev = jnp.where(row == 0, z_before, pltpu.roll(z, 1, 0))
    z_next = jnp.where(row == tm - 1, z_after, pltpu.roll(z, tm - 1, 0))
    conv = cb_ref[...] * (z_prev * cw_ref[0:1, :] + z * cw_ref[1:2, :] + z_next * cw_ref[2:3, :]
                          + cbias_ref[...])

    n_att = att_ref.shape[1] // HEAD
    gt_a = mod_ref[0, 2:3, :]
    ffn_gain = gffn_ref[...] * (1.0 + mod_ref[0, 4:5, :])
    ffn_shift = mod_ref[0, 3:4, :]
    for r in range(0, tm, NORM_ROWS):
        rows = slice(r, r + NORM_ROWS)
        for g in range(D_MODEL // HEAD):
            cols = slice(g * HEAD, (g + 1) * HEAD)
            blk = att_ref[rows, cols] if g < n_att else conv[rows, (g - n_att) * HEAD:(g - n_att + 1) * HEAD]
            mix_ref[rows, cols] = _rms(blk, gmo_ref[:, cols]).astype(BF16)
    x1 = x_ref[...] + gt_a * _dot(mix_ref[...], wout_ref[...])
    x1_ref[...] = x1
    hf_ref[...] = (_rms(x1, ffn_gain) + ffn_shift).astype(BF16)


def _out(att2, cb, z, cw, cbias, gmo, wout, x2, mod_l, gffn, batch, seq):
    tm = TM_OUT
    tpb = seq // tm
    n8 = batch * seq // 8
    z3 = z.reshape(n8, 8, CONV_W)
    const = lambda b, t: (0, 0)
    tok = lambda b, t: (b * tpb + t, 0)
    return pl.pallas_call(
        _out_kernel,
        grid=(batch, tpb),
        in_specs=[
            pl.BlockSpec((tm, att2.shape[1]), tok),
            pl.BlockSpec((tm, CONV_W), tok),
            pl.BlockSpec((tm, CONV_W), tok),
            pl.BlockSpec((1, 8, CONV_W), lambda b, t: (jnp.maximum((b * tpb + t) * (tm // 8) - 1, 0), 0, 0)),
            pl.BlockSpec((1, 8, CONV_W), lambda b, t: (jnp.minimum((b * tpb + t + 1) * (tm // 8), n8 - 1), 0, 0)),
            pl.BlockSpec((3, CONV_W), const),
            pl.BlockSpec((1, CONV_W), const),
            pl.BlockSpec((1, D_MODEL), const),
            pl.BlockSpec(wout.shape, const, pipeline_mode=pl.Buffered(1)),
            pl.BlockSpec((tm, D_MODEL), tok),
            pl.BlockSpec((1, 6, D_MODEL), lambda b, t: (b, 0, 0)),
            pl.BlockSpec((1, D_MODEL), const),
        ],
        out_specs=[
            pl.BlockSpec((tm, D_MODEL), tok),
            pl.BlockSpec((tm, D_MODEL), tok),
        ],
        out_shape=[
            jax.ShapeDtypeStruct((batch * seq, D_MODEL), F32),
            jax.ShapeDtypeStruct((batch * seq, D_MODEL), BF16),
        ],
        scratch_shapes=[pltpu.VMEM((tm, D_MODEL), BF16)],
        compiler_params=pltpu.CompilerParams(
            dimension_semantics=("parallel", "parallel"), vmem_limit_bytes=VMEM_LIMIT),
        name="out_proj",
    )(att2, cb, z, z3, z3, cw, cbias, gmo, wout, x2, mod_l, gffn)


def _ffn_kernel(h_ref, hp_ref, hn_ref, wup_ref, cw_ref, cbias_ref, wdn_ref, x1_ref, mod_ref, gfin_ref,
                o_ref, hbuf_ref, u_ref):
    t = pl.program_id(1)
    f = pl.program_id(2)
    nf = pl.num_programs(2)
    tm = h_ref.shape[0]

    def step(first, last):
        if first:
            hbuf_ref[HALO:HALO + tm, :] = h_ref[...]
            hbuf_ref[0:HALO, :] = jnp.where(t == 0, jnp.zeros_like(hp_ref[0]), hp_ref[0])
            hbuf_ref[HALO + tm:, :] = jnp.where(t == pl.num_programs(1) - 1, jnp.zeros_like(hn_ref[0]), hn_ref[0])
        u_ref[...] = _dot(hbuf_ref[...], wup_ref[...])
        cw = cw_ref[f]
        y = (u_ref[pl.ds(HALO - 1, tm), :] * cw[0:1, :] + u_ref[pl.ds(HALO, tm), :] * cw[1:2, :]
             + u_ref[pl.ds(HALO + 1, tm), :] * cw[2:3, :] + cbias_ref[f])
        act = jnp.concatenate(
            [y[:, 2 * j * LANES:(2 * j + 1) * LANES] * _silu(y[:, (2 * j + 1) * LANES:(2 * j + 2) * LANES])
             for j in range(BPS)], axis=1).astype(BF16)
        d = _dot(act, wdn_ref[...])
        acc = d if first else o_ref[...] + d
        if last:
            acc = _rms(x1_ref[...] + mod_ref[0, 5:6, :] * acc, gfin_ref[...])
        o_ref[...] = acc

    pl.when(f == 0)(lambda: step(True, False))
    pl.when((f > 0) & (f < nf - 1))(lambda: step(False, False))
    pl.when(f == nf - 1)(lambda: step(False, True))


def _ffn(hf, wup_t, cw_t, cbias_t, wdn, x1, mod_l, gfin, batch, seq):
    tm = TM_FFN
    tpb = seq // tm
    nf = N_SUB // FFN_SUBS
    tf = FFN_SUBS * TF_SUB
    n_h = batch * seq // HALO
    hf3 = hf.reshape(n_h, HALO, D_MODEL)
    tok = lambda b, t, f: (b * tpb + t, 0)
    return pl.pallas_call(
        _ffn_kernel,
        grid=(batch, tpb, nf),
        in_specs=[
            pl.BlockSpec((tm, D_MODEL), tok),
            pl.BlockSpec((1, HALO, D_MODEL),
                         lambda b, t, f: (jnp.maximum((b * tpb + t) * (tm // HALO) - 1, 0), 0, 0)),
            pl.BlockSpec((1, HALO, D_MODEL),
                         lambda b, t, f: (jnp.minimum((b * tpb + t + 1) * (tm // HALO), n_h - 1), 0, 0)),
            pl.BlockSpec((D_MODEL, 2 * TF_SUB), lambda b, t, f: (0, f)),
            pl.BlockSpec((nf, 3, 2 * TF_SUB), lambda b, t, f: (0, 0, 0)),
            pl.BlockSpec((nf, 1, 2 * TF_SUB), lambda b, t, f: (0, 0, 0)),
            pl.BlockSpec((tf, D_MODEL), lambda b, t, f: (f, 0)),
            pl.BlockSpec((tm, D_MODEL), tok),
            pl.BlockSpec((1, 6, D_MODEL), lambda b, t, f: (b, 0, 0)),
            pl.BlockSpec((1, D_MODEL), lambda b, t, f: (0, 0)),
        ],
        out_specs=pl.BlockSpec((tm, D_MODEL), tok),
        out_shape=jax.ShapeDtypeStruct((batch * seq, D_MODEL), F32),
        scratch_shapes=[
            pltpu.VMEM((tm + 2 * HALO, D_MODEL), BF16),
            pltpu.VMEM((tm + 2 * HALO, 2 * TF_SUB), F32),
        ],
        compiler_params=pltpu.CompilerParams(
            dimension_semantics=("parallel", "parallel", "arbitrary"), vmem_limit_bytes=VMEM_LIMIT),
        name="ffn",
    )(hf, hf3, hf3, wup_t, cw_t, cbias_t, wdn, x1, mod_l, gfin)


def _rope_tables(seq):
    rows = seq // GRID_W
    row = jnp.repeat(jnp.arange(rows), GRID_W).astype(F32)
    col = jnp.tile(jnp.arange(GRID_W), rows).astype(F32)
    axis_dim = QK_ROPE // 2
    inv = ROPE_BASE ** (-jnp.arange(0, axis_dim, 2, dtype=F32) / axis_dim)
    ang = jnp.concatenate([row[:, None] * inv, col[:, None] * inv], axis=-1)
    cos, sin = jnp.cos(ang), jnp.sin(ang)
    c = jnp.repeat(cos, 2, axis=-1)
    s = jnp.stack([-sin, sin], axis=-1).reshape(seq, QK_ROPE)
    return jnp.tile(c, (1, LANES // QK_ROPE)), jnp.tile(s, (1, LANES // QK_ROPE))


def _tile_ffn_cols(a):
    pad = ((0, 0), (0, FFN_PAD - FFN_DIM))
    val = jnp.pad(a[:, :FFN_DIM], pad).reshape(a.shape[0], N_SUB, BPS, 1, LANES)
    gate = jnp.pad(a[:, FFN_DIM:], pad).reshape(a.shape[0], N_SUB, BPS, 1, LANES)
    both = jnp.concatenate([val, gate], axis=3).reshape(a.shape[0], N_SUB, 2 * TF_SUB)
    return jnp.moveaxis(both, 1, 0)


N_QKV = Q_RANK + KV_RANK + QK_ROPE
N_QKV_PAD = Q_RANK + KV_RANK + LANES


PREP_ROWS = 512


def _prep_qkv_kernel(w_ref, o_ref):
    o_ref[...] = w_ref[...].T.astype(BF16)


def _prep_conv_kernel(lo_ref, hi_ref, o_ref):
    off = N_QKV % PREP_ROWS
    o_ref[...] = jnp.concatenate([lo_ref[off:, :], hi_ref[:off, :]], axis=0).T.astype(BF16)


def _prep_in(w_in_t):
    wqkv = pl.pallas_call(
        _prep_qkv_kernel,
        grid=(N_QKV_PAD // LANES,),
        in_specs=[pl.BlockSpec((LANES, D_MODEL), lambda j: (j, 0))],
        out_specs=pl.BlockSpec((D_MODEL, LANES), lambda j: (0, j)),
        out_shape=jax.ShapeDtypeStruct((D_MODEL, N_QKV_PAD), BF16),
        compiler_params=pltpu.CompilerParams(
            dimension_semantics=("parallel",), vmem_limit_bytes=VMEM_LIMIT),
        name="prep_qkv",
    )(w_in_t)
    first = N_QKV // PREP_ROWS
    wconv = pl.pallas_call(
        _prep_conv_kernel,
        grid=(3 * CONV_W // PREP_ROWS,),
        in_specs=[
            pl.BlockSpec((PREP_ROWS, D_MODEL), lambda j: (first + j, 0)),
            pl.BlockSpec((PREP_ROWS, D_MODEL), lambda j: (first + j + 1, 0)),
        ],
        out_specs=pl.BlockSpec((D_MODEL, PREP_ROWS), lambda j: (0, j)),
        out_shape=jax.ShapeDtypeStruct((D_MODEL, 3 * CONV_W), BF16),
        compiler_params=pltpu.CompilerParams(
            dimension_semantics=("parallel",), vmem_limit_bytes=VMEM_LIMIT),
        name="prep_conv",
    )(w_in_t, w_in_t)
    return wqkv, wconv


def kernel(x, c, ctx, c_ctx, w_ada, b_ada, g_mix_norm, w_in, g_q_a, w_q_b, g_kv_a, w_kv_b, conv_w, conv_b, g_mix_out, w_out, g_ffn_norm, w_up, ffn_conv_w, ffn_conv_b, w_down, g_final):
    batch, seq, d = x.shape
    assert d == D_MODEL and w_ada.shape[0] == 1
    x2 = x.reshape(batch * seq, d)

    c8 = jnp.concatenate([c, c_ctx[None], jnp.zeros((8 - batch - 1, d), F32)], axis=0)
    mod = _ada(c8, w_ada[0], b_ada[0][None])
    mod_l = mod[:batch].reshape(batch, 6, d)
    mod_c = mod[batch:batch + 1].reshape(1, 6, d)

    wqkv, wconv = _prep_in(w_in[0].T)
    wqb3 = w_q_b[0].reshape(Q_RANK, N_HEADS, QK_DIM)
    wqb = jnp.concatenate([wqb3[:, :, :QK_NOPE].reshape(Q_RANK, -1),
                           wqb3[:, :, QK_NOPE:].reshape(Q_RANK, -1)], axis=-1).astype(BF16)
    wkvb = w_kv_b[0].astype(BF16)
    cw_t = _tile_ffn_cols(ffn_conv_w[0])
    cbias_t = _tile_ffn_cols(ffn_conv_b)

    cq, sq = _rope_tables(seq)

    q, k_l, v_l, cb, z = _in_lat(x2, mod_l, g_mix_norm, wqkv, wconv, g_q_a, wqb, g_kv_a, wkvb,
                                 cq, sq, batch, seq)
    k_c, v_c = _in_ctx(ctx, mod_c, g_mix_norm, wqkv, g_kv_a, wkvb)
    att, wup_t, wdn, wout = _attn(q, k_l, v_l, k_c, v_c, w_up[0], w_down[0], w_out[0])
    x1, hf = _out(att.reshape(batch * seq, N_HEADS * HEAD), cb, z, conv_w[0], conv_b, g_mix_out,
                  wout, x2, mod_l, g_ffn_norm, batch, seq)
    y = _ffn(hf, wup_t, cw_t, cbias_t, wdn, x1, mod_l, g_final[None], batch, seq)
    return y.reshape(batch, seq, d)
```

```python
import functools

import jax
import jax.numpy as jnp
from jax import lax
from jax.experimental import pallas as pl
from jax.experimental.pallas import tpu as pltpu

F32 = jnp.float32
BF16 = jnp.bfloat16

D_MODEL = 2048
N_HEADS = 8
HEAD = 128
QK_NOPE = 128
QK_ROPE = 64
QK_DIM = QK_NOPE + QK_ROPE
Q_RANK = 512
KV_RANK = 256
CONV_W = 1024
GRID_W = 64
ROPE_BASE = 10000.0
EPS = 1e-6
LOG2E = 1.4426950408889634
FFN_DIM = 5504

V7X_VMEM_BYTES = 64 * 1024 * 1024
VMEM_LIMIT = V7X_VMEM_BYTES - 8 * 1024 * 1024

TM_IN = 512
TQ = 2048
TQ_SUB = 256
TM_OUT = 512
NORM_ROWS = 64
TM_FFN = 512
LANES = 128
TF_SUB = 512
FFN_SUBS = 1
HALO = 8
BPS = TF_SUB // LANES
FFN_BLKS = FFN_DIM // LANES
N_SUB = (FFN_DIM + TF_SUB - 1) // TF_SUB
FFN_PAD = N_SUB * TF_SUB
SLAB_OF_BLOCK = (0, 2, 1, 3)


def _rms(x, g):
    return x * lax.rsqrt(jnp.mean(x * x, axis=-1, keepdims=True) + EPS) * g


def _silu(x):
    h = 0.5 * x
    return h * (1.0 + jnp.tanh(h))


def _dot(a, b):
    return jnp.dot(a, b, preferred_element_type=F32)


def _rope(x, c, s):
    w = x.shape[-1]
    lane = lax.broadcasted_iota(jnp.int32, x.shape, 1)
    from_right = pltpu.roll(x, w - 1, 1)
    from_left = pltpu.roll(x, 1, 1)
    partner = jnp.where(lane % 2 == 0, from_right, from_left)
    return x * c + partner * s


def _ada_kernel(c_ref, w_ref, b_ref, o_ref):
    s = _silu(c_ref[...]).astype(BF16)
    o_ref[...] = _dot(s, w_ref[...].astype(BF16)) + b_ref[...]


def _ada(c8, w, b):
    n = w.shape[1]
    tn = 1024
    return pl.pallas_call(
        _ada_kernel,
        grid=(n // tn,),
        in_specs=[
            pl.BlockSpec((8, D_MODEL), lambda j: (0, 0)),
            pl.BlockSpec((D_MODEL, tn), lambda j: (0, j)),
            pl.BlockSpec((1, tn), lambda j: (0, j)),
        ],
        out_specs=pl.BlockSpec((8, tn), lambda j: (0, j)),
        out_shape=jax.ShapeDtypeStruct((8, n), F32),
        compiler_params=pltpu.CompilerParams(
            dimension_semantics=("parallel",), vmem_limit_bytes=VMEM_LIMIT),
        name="ada",
    )(c8, w, b)


def _in_lat_kernel(x_ref, mod_ref, gmix_ref, wqkv_ref, wconv_ref, gq_ref, wqb_ref, gkv_ref,
                   wkvb_ref, cq_ref, sq_ref, q_ref, k_ref, v_ref, cb_ref, z_ref):
    sh = mod_ref[0, 0:1, :]
    sc = mod_ref[0, 1:2, :]
    h = (_rms(x_ref[...], gmix_ref[...]) * (1.0 + sc) + sh).astype(BF16)

    p2 = _dot(h, wconv_ref[...])
    cb_ref[...] = p2[:, :CONV_W]
    z_ref[...] = p2[:, CONV_W:2 * CONV_W] * p2[:, 2 * CONV_W:]

    p1 = _dot(h, wqkv_ref[...])
    qa = p1[:, :Q_RANK]
    kva = p1[:, Q_RANK:Q_RANK + KV_RANK]
    kr = p1[:, Q_RANK + KV_RANK:]

    c1 = cq_ref[...]
    s1 = sq_ref[...]
    reps = N_HEADS * QK_ROPE // LANES
    cq = jnp.concatenate([c1] * reps, axis=1)
    sq = jnp.concatenate([s1] * reps, axis=1)
    scale = QK_DIM ** -0.5 * LOG2E
    q = _dot(_rms(qa, gq_ref[...]).astype(BF16), wqb_ref[...])
    qn = (q[:, :N_HEADS * QK_NOPE] * scale).astype(BF16)
    qr = (_rope(q[:, N_HEADS * QK_NOPE:], cq, sq) * scale).astype(BF16)

    kv = _dot(_rms(kva, gkv_ref[...]).astype(BF16), wkvb_ref[...]).astype(BF16)
    krr = _rope(kr, c1, s1)[:, :QK_ROPE].astype(BF16)
    for hd in range(N_HEADS):
        q_ref[0, hd, :, 0:QK_NOPE] = qn[:, hd * QK_NOPE:(hd + 1) * QK_NOPE]
        q_ref[0, hd, :, QK_NOPE:QK_DIM] = qr[:, hd * QK_ROPE:(hd + 1) * QK_ROPE]
        k_ref[0, hd, :, 0:QK_NOPE] = kv[:, hd * 2 * HEAD:hd * 2 * HEAD + QK_NOPE]
        k_ref[0, hd, :, QK_NOPE:QK_DIM] = krr
        v_ref[0, hd, :, :] = kv[:, hd * 2 * HEAD + QK_NOPE:(hd + 1) * 2 * HEAD]


def _in_lat(x2, mod_l, gmix, wqkv, wconv, gq, wqb, gkv, wkvb, cq, sq, batch, seq):
    tm = TM_IN
    tpb = seq // tm
    const = lambda b, t: (0, 0)
    one = pl.Buffered(1)
    tok = lambda b, t: (b * tpb + t, 0)
    hd4 = lambda b, t: (b, 0, t, 0)
    return pl.pallas_call(
        _in_lat_kernel,
        grid=(batch, tpb),
        in_specs=[
            pl.BlockSpec((tm, D_MODEL), tok),
            pl.BlockSpec((1, 6, D_MODEL), lambda b, t: (b, 0, 0)),
            pl.BlockSpec((1, D_MODEL), const),
            pl.BlockSpec(wqkv.shape, const, pipeline_mode=one),
            pl.BlockSpec(wconv.shape, const, pipeline_mode=one),
            pl.BlockSpec((1, Q_RANK), const),
            pl.BlockSpec(wqb.shape, const, pipeline_mode=one),
            pl.BlockSpec((1, KV_RANK), const),
            pl.BlockSpec(wkvb.shape, const, pipeline_mode=one),
            pl.BlockSpec((tm, LANES), lambda b, t: (t, 0)),
            pl.BlockSpec((tm, LANES), lambda b, t: (t, 0)),
        ],
        out_specs=[
            pl.BlockSpec((1, N_HEADS, tm, QK_DIM), hd4),
            pl.BlockSpec((1, N_HEADS, tm, QK_DIM), hd4),
            pl.BlockSpec((1, N_HEADS, tm, HEAD), hd4),
            pl.BlockSpec((tm, CONV_W), tok),
            pl.BlockSpec((tm, CONV_W), tok),
        ],
        out_shape=[
            jax.ShapeDtypeStruct((batch, N_HEADS, seq, QK_DIM), BF16),
            jax.ShapeDtypeStruct((batch, N_HEADS, seq, QK_DIM), BF16),
            jax.ShapeDtypeStruct((batch, N_HEADS, seq, HEAD), BF16),
            jax.ShapeDtypeStruct((batch * seq, CONV_W), F32),
            jax.ShapeDtypeStruct((batch * seq, CONV_W), F32),
        ],
        compiler_params=pltpu.CompilerParams(
            dimension_semantics=("parallel", "parallel"), vmem_limit_bytes=VMEM_LIMIT),
        name="in_lat",
    )(x2, mod_l, gmix, wqkv, wconv, gq, wqb, gkv, wkvb, cq, sq)


def _in_ctx_kernel(x_ref, mod_ref, gmix_ref, wkv_ref, gkv_ref, wkvb_ref, k_ref, v_ref):
    sh = mod_ref[0, 0:1, :]
    sc = mod_ref[0, 1:2, :]
    h = (_rms(x_ref[0], gmix_ref[...]) * (1.0 + sc) + sh).astype(BF16)
    p1 = _dot(h, wkv_ref[:, Q_RANK:])
    kva = p1[:, :KV_RANK]
    kr = p1[:, KV_RANK:KV_RANK + QK_ROPE].astype(BF16)
    kv = _dot(_rms(kva, gkv_ref[...]).astype(BF16), wkvb_ref[...]).astype(BF16)
    for hd in range(N_HEADS):
        k_ref[0, hd, :, 0:QK_NOPE] = kv[:, hd * 2 * HEAD:hd * 2 * HEAD + QK_NOPE]
        k_ref[0, hd, :, QK_NOPE:QK_DIM] = kr
        v_ref[0, hd, :, :] = kv[:, hd * 2 * HEAD + QK_NOPE:(hd + 1) * 2 * HEAD]


def _in_ctx(ctx, mod_c, gmix, wkv, gkv, wkvb):
    batch, n_ctx, _ = ctx.shape
    const = lambda b: (0, 0)
    return pl.pallas_call(
        _in_ctx_kernel,
        grid=(batch,),
        in_specs=[
            pl.BlockSpec((1, n_ctx, D_MODEL), lambda b: (b, 0, 0)),
            pl.BlockSpec((1, 6, D_MODEL), lambda b: (0, 0, 0)),
            pl.BlockSpec((1, D_MODEL), const),
            pl.BlockSpec(wkv.shape, const),
            pl.BlockSpec((1, KV_RANK), const),
            pl.BlockSpec(wkvb.shape, const),
        ],
        out_specs=[
            pl.BlockSpec((1, N_HEADS, n_ctx, QK_DIM), lambda b: (b, 0, 0, 0)),
            pl.BlockSpec((1, N_HEADS, n_ctx, HEAD), lambda b: (b, 0, 0, 0)),
        ],
        out_shape=[
            jax.ShapeDtypeStruct((batch, N_HEADS, n_ctx, QK_DIM), BF16),
            jax.ShapeDtypeStruct((batch, N_HEADS, n_ctx, HEAD), BF16),
        ],
        compiler_params=pltpu.CompilerParams(
            dimension_semantics=("parallel",), vmem_limit_bytes=VMEM_LIMIT),
        name="in_ctx",
    )(ctx, mod_c, gmix, wkv, gkv, wkvb)


UP_LANE_BLKS = N_SUB * 2 * BPS
DN_ROW_BLKS = N_SUB * BPS


def _up_src(c):
    c_in = jnp.minimum(c, UP_LANE_BLKS - 1)
    slab = (c_in % (2 * BPS)) // 2
    blk_in_tile = sum(jnp.where(slab == SLAB_OF_BLOCK[b], b, 0) for b in range(BPS))
    blk = (c_in // (2 * BPS)) * BPS + blk_in_tile
    src = (c_in % 2) * FFN_BLKS + jnp.minimum(blk, FFN_BLKS - 1)
    return src, (c < UP_LANE_BLKS) & (blk < FFN_BLKS)


def _attn_kernel(*refs, n_up, n_dn):
    q_ref, kl_ref, vl_ref, kc_ref, vc_ref = refs[:5]
    wu_refs = refs[5:5 + n_up]
    wd_refs = refs[5 + n_up:5 + n_up + n_dn]
    wo_ref = refs[5 + n_up + n_dn]
    o_ref, wup_ref, wdn_ref, wout_ref = refs[6 + n_up + n_dn:]

    nt = (((1,), (1,)), ((), ()))

    def with_ones(v):
        e0 = (lax.broadcasted_iota(jnp.int32, v.shape, 1) == 0).astype(BF16)
        return jnp.concatenate([v, e0], axis=1)

    vl = with_ones(vl_ref[0, 0])
    vc = with_ones(vc_ref[0, 0])
    for j in range(TQ // TQ_SUB):
        rows = slice(j * TQ_SUB, (j + 1) * TQ_SUB)
        q = q_ref[0, 0, rows, :]
        s_l = lax.dot_general(q, kl_ref[0, 0], nt, preferred_element_type=F32)
        s_c = lax.dot_general(q, kc_ref[0, 0], nt, preferred_element_type=F32)
        m = jnp.maximum(jnp.max(s_l, axis=-1, keepdims=True), jnp.max(s_c, axis=-1, keepdims=True))
        p_l = jnp.exp2(s_l - m).astype(BF16)
        p_c = jnp.exp2(s_c - m).astype(BF16)
        o = _dot(p_l, vl) + _dot(p_c, vc)
        o_ref[0, rows, :] = o[:, :HEAD] / o[:, HEAD:HEAD + 1]

    step = ((pl.program_id(0) * pl.num_programs(1) + pl.program_id(1)) * pl.num_programs(2)
            + pl.program_id(2))
    for i in range(n_up):
        _, real = _up_src(step * n_up + i)
        wup_ref[:, i * LANES:(i + 1) * LANES] = jnp.where(real, wu_refs[i][...], 0.0).astype(BF16)
    dn_blk = jnp.minimum(step, DN_ROW_BLKS // n_dn - 1) * n_dn
    for i in range(n_dn):
        wdn_ref[i * LANES:(i + 1) * LANES, :] = jnp.where(dn_blk + i < FFN_BLKS, wd_refs[i][...], 0.0).astype(BF16)
    wout_ref[...] = wo_ref[...].astype(BF16)


def _attn(q, k_l, v_l, k_c, v_c, w_up, w_down, w_out):
    batch, heads, seq, _ = q.shape
    n_ctx = k_c.shape[2]
    nq = seq // TQ
    n_steps = batch * heads * nq
    n_up = -(-UP_LANE_BLKS // n_steps)
    n_dn = -(-DN_ROW_BLKS // n_steps)
    assert DN_ROW_BLKS % n_dn == 0 and D_MODEL % n_steps == 0
    wo_rows = D_MODEL // n_steps
    step = lambda b, h, i: (b * heads + h) * nq + i
    kv_map = lambda b, h, i: (b, h, 0, 0)
    dn_out = lambda b, h, i: jnp.minimum(step(b, h, i), DN_ROW_BLKS // n_dn - 1)
    up_specs = [pl.BlockSpec((D_MODEL, LANES),
                             functools.partial(lambda b, h, i, k: (0, _up_src(step(b, h, i) * n_up + k)[0]), k=k))
                for k in range(n_up)]
    dn_specs = [pl.BlockSpec((LANES, D_MODEL),
                             functools.partial(
                                 lambda b, h, i, k: (jnp.minimum(dn_out(b, h, i) * n_dn + k, FFN_BLKS - 1), 0), k=k))
                for k in range(n_dn)]
    return pl.pallas_call(
        functools.partial(_attn_kernel, n_up=n_up, n_dn=n_dn),
        grid=(batch, heads, nq),
        in_specs=[
            pl.BlockSpec((1, 1, TQ, QK_DIM), lambda b, h, i: (b, h, i, 0)),
            pl.BlockSpec((1, 1, seq, QK_DIM), kv_map),
            pl.BlockSpec((1, 1, seq, HEAD), kv_map),
            pl.BlockSpec((1, 1, n_ctx, QK_DIM), kv_map),
            pl.BlockSpec((1, 1, n_ctx, HEAD), kv_map),
            *up_specs,
            *dn_specs,
            pl.BlockSpec((wo_rows, D_MODEL), lambda b, h, i: (step(b, h, i), 0)),
        ],
        out_specs=[
            pl.BlockSpec((1, TQ, HEAD), lambda b, h, i: (b, i, h)),
            pl.BlockSpec((D_MODEL, n_up * LANES), lambda b, h, i: (0, step(b, h, i))),
            pl.BlockSpec((n_dn * LANES, D_MODEL), lambda b, h, i: (dn_out(b, h, i), 0)),
            pl.BlockSpec((wo_rows, D_MODEL), lambda b, h, i: (step(b, h, i), 0)),
        ],
        out_shape=[
            jax.ShapeDtypeStruct((batch, seq, heads * HEAD), F32),
            jax.ShapeDtypeStruct((D_MODEL, n_steps * n_up * LANES), BF16),
            jax.ShapeDtypeStruct((FFN_PAD, D_MODEL), BF16),
            jax.ShapeDtypeStruct((D_MODEL, D_MODEL), BF16),
        ],
        compiler_params=pltpu.CompilerParams(
            dimension_semantics=("arbitrary", "arbitrary", "arbitrary"),
            vmem_limit_bytes=VMEM_LIMIT),
        name="attn",
    )(q, k_l, v_l, k_c, v_c, *([w_up] * n_up), *([w_down] * n_dn), w_out)


def _out_kernel(att_ref, cb_ref, z_ref, zp_ref, zn_ref, cw_ref, cbias_ref, gmo_ref, wout_ref,
                x_ref, mod_ref, gffn_ref, x1_ref, hf_ref, mix_ref):
    t = pl.program_id(1)
    tm = z_ref.shape[0]
    z = z_ref[...]
    row = lax.broadcasted_iota(jnp.int32, z.shape, 0)
    z_before = jnp.where(t == 0, 0.0, zp_ref[0, 7:8, :])
    z_after = jnp.where(t == pl.num_programs(1) - 1, 0.0, zn_ref[0, 0:1, :])
    z_prev = jnp.where(row == 0, z_before, pltpu.roll(z, 1, 0))
    z_next = jnp.where(row == tm - 1, z_after, pltpu.roll(z, tm - 1, 0))
    conv = cb_ref[...] * (z_prev * cw_ref[0:1, :] + z * cw_ref[1:2, :] + z_next * cw_ref[2:3, :]
                          + cbias_ref[...])

    n_att = att_ref.shape[1] // HEAD
    gt_a = mod_ref[0, 2:3, :]
    ffn_gain = gffn_ref[...] * (1.0 + mod_ref[0, 4:5, :])
    ffn_shift = mod_ref[0, 3:4, :]
    for r in range(0, tm, NORM_ROWS):
        rows = slice(r, r + NORM_ROWS)
        for g in range(D_MODEL // HEAD):
            cols = slice(g * HEAD, (g + 1) * HEAD)
            blk = att_ref[rows, cols] if g < n_att else conv[rows, (g - n_att) * HEAD:(g - n_att + 1) * HEAD]
            mix_ref[rows, cols] = _rms(blk, gmo_ref[:, cols]).astype(BF16)
    x1 = x_ref[...] + gt_a * _dot(mix_ref[...], wout_ref[...])
    x1_ref[...] = x1
    hf_ref[...] = (_rms(x1, ffn_gain) + ffn_shift).astype(BF16)


def _out(att2, cb, z, cw, cbias, gmo, wout, x2, mod_l, gffn, batch, seq):
    tm = TM_OUT
    tpb = seq // tm
    n8 = batch * seq // 8
    z3 = z.reshape(n8, 8, CONV_W)
    const = lambda b, t: (0, 0)
    tok = lambda b, t: (b * tpb + t, 0)
    return pl.pallas_call(
        _out_kernel,
        grid=(batch, tpb),
        in_specs=[
            pl.BlockSpec((tm, att2.shape[1]), tok),
            pl.BlockSpec((tm, CONV_W), tok),
            pl.BlockSpec((tm, CONV_W), tok),
            pl.BlockSpec((1, 8, CONV_W), lambda b, t: (jnp.maximum((b * tpb + t) * (tm // 8) - 1, 0), 0, 0)),
            pl.BlockSpec((1, 8, CONV_W), lambda b, t: (jnp.minimum((b * tpb + t + 1) * (tm // 8), n8 - 1), 0, 0)),
            pl.BlockSpec((3, CONV_W), const),
            pl.BlockSpec((1, CONV_W), const),
            pl.BlockSpec((1, D_MODEL), const),
            pl.BlockSpec(wout.shape, const, pipeline_mode=pl.Buffered(1)),
            pl.BlockSpec((tm, D_MODEL), tok),
            pl.BlockSpec((1, 6, D_MODEL), lambda b, t: (b, 0, 0)),
            pl.BlockSpec((1, D_MODEL), const),
        ],
        out_specs=[
            pl.BlockSpec((tm, D_MODEL), tok),
            pl.BlockSpec((tm, D_MODEL), tok),
        ],
        out_shape=[
            jax.ShapeDtypeStruct((batch * seq, D_MODEL), F32),
            jax.ShapeDtypeStruct((batch * seq, D_MODEL), BF16),
        ],
        scratch_shapes=[pltpu.VMEM((tm, D_MODEL), BF16)],
        compiler_params=pltpu.CompilerParams(
            dimension_semantics=("parallel", "parallel"), vmem_limit_bytes=VMEM_LIMIT),
        name="out_proj",
    )(att2, cb, z, z3, z3, cw, cbias, gmo, wout, x2, mod_l, gffn)


def _ffn_kernel(h_ref, hp_ref, hn_ref, wup_ref, cw_ref, cbias_ref, wdn_ref, x1_ref, mod_ref, gfin_ref,
                o_ref, hbuf_ref, u_ref):
    t = pl.program_id(1)
    f = pl.program_id(2)
    nf = pl.num_programs(2)
    tm = h_ref.shape[0]

    def step(first, last):
        if first:
            hbuf_ref[HALO:HALO + tm, :] = h_ref[...]
            hbuf_ref[0:HALO, :] = jnp.where(t == 0, jnp.zeros_like(hp_ref[0]), hp_ref[0])
            hbuf_ref[HALO + tm:, :] = jnp.where(t == pl.num_programs(1) - 1, jnp.zeros_like(hn_ref[0]), hn_ref[0])
        u_ref[...] = _dot(hbuf_ref[...], wup_ref[...])
        cw = cw_ref[f]
        y = (u_ref[pl.ds(HALO - 1, tm), :] * cw[0:1, :] + u_ref[pl.ds(HALO, tm), :] * cw[1:2, :]
             + u_ref[pl.ds(HALO + 1, tm), :] * cw[2:3, :] + cbias_ref[f])
        act = jnp.concatenate(
            [y[:, 2 * j * LANES:(2 * j + 1) * LANES] * _silu(y[:, (2 * j + 1) * LANES:(2 * j + 2) * LANES])
             for j in SLAB_OF_BLOCK], axis=1).astype(BF16)
        d = _dot(act, wdn_ref[...])
        acc = d if first else o_ref[...] + d
        if last:
            acc = _rms(x1_ref[...] + mod_ref[0, 5:6, :] * acc, gfin_ref[...])
        o_ref[...] = acc

    pl.when(f == 0)(lambda: step(True, False))
    pl.when((f > 0) & (f < nf - 1))(lambda: step(False, False))
    pl.when(f == nf - 1)(lambda: step(False, True))


def _ffn(hf, wup_t, cw_t, cbias_t, wdn, x1, mod_l, gfin, batch, seq):
    tm = TM_FFN
    tpb = seq // tm
    nf = N_SUB // FFN_SUBS
    tf = FFN_SUBS * TF_SUB
    n_h = batch * seq // HALO
    hf3 = hf.reshape(n_h, HALO, D_MODEL)
    tok = lambda b, t, f: (b * tpb + t, 0)
    return pl.pallas_call(
        _ffn_kernel,
        grid=(batch, tpb, nf),
        in_specs=[
            pl.BlockSpec((tm, D_MODEL), tok),
            pl.BlockSpec((1, HALO, D_MODEL),
                         lambda b, t, f: (jnp.maximum((b * tpb + t) * (tm // HALO) - 1, 0), 0, 0)),
            pl.BlockSpec((1, HALO, D_MODEL),
                         lambda b, t, f: (jnp.minimum((b * tpb + t + 1) * (tm // HALO), n_h - 1), 0, 0)),
            pl.BlockSpec((D_MODEL, 2 * TF_SUB), lambda b, t, f: (0, f)),
            pl.BlockSpec((nf, 3, 2 * TF_SUB), lambda b, t, f: (0, 0, 0)),
            pl.BlockSpec((nf, 1, 2 * TF_SUB), lambda b, t, f: (0, 0, 0)),
            pl.BlockSpec((tf, D_MODEL), lambda b, t, f: (f, 0)),
            pl.BlockSpec((tm, D_MODEL), tok),
            pl.BlockSpec((1, 6, D_MODEL), lambda b, t, f: (b, 0, 0)),
            pl.BlockSpec((1, D_MODEL), lambda b, t, f: (0, 0)),
        ],
        out_specs=pl.BlockSpec((tm, D_MODEL), tok),
        out_shape=jax.ShapeDtypeStruct((batch * seq, D_MODEL), F32),
        scratch_shapes=[
            pltpu.VMEM((tm + 2 * HALO, D_MODEL), BF16),
            pltpu.VMEM((tm + 2 * HALO, 2 * TF_SUB), F32),
        ],
        compiler_params=pltpu.CompilerParams(
            dimension_semantics=("parallel", "parallel", "arbitrary"), vmem_limit_bytes=VMEM_LIMIT),
        name="ffn",
    )(hf, hf3, hf3, wup_t, cw_t, cbias_t, wdn, x1, mod_l, gfin)


def _rope_tables(seq):
    rows = seq // GRID_W
    row = jnp.repeat(jnp.arange(rows), GRID_W).astype(F32)
    col = jnp.tile(jnp.arange(GRID_W), rows).astype(F32)
    axis_dim = QK_ROPE // 2
    inv = ROPE_BASE ** (-jnp.arange(0, axis_dim, 2, dtype=F32) / axis_dim)
    ang = jnp.concatenate([row[:, None] * inv, col[:, None] * inv], axis=-1)
    cos, sin = jnp.cos(ang), jnp.sin(ang)
    c = jnp.repeat(cos, 2, axis=-1)
    s = jnp.stack([-sin, sin], axis=-1).reshape(seq, QK_ROPE)
    return jnp.tile(c, (1, LANES // QK_ROPE)), jnp.tile(s, (1, LANES // QK_ROPE))


def _tile_ffn_cols(a):
    pad = ((0, 0), (0, FFN_PAD - FFN_DIM))
    val = jnp.pad(a[:, :FFN_DIM], pad).reshape(a.shape[0], N_SUB, BPS, 1, LANES)
    gate = jnp.pad(a[:, FFN_DIM:], pad).reshape(a.shape[0], N_SUB, BPS, 1, LANES)
    block_of_slab = [SLAB_OF_BLOCK.index(p) for p in range(BPS)]
    both = jnp.concatenate([val, gate], axis=3)[:, :, block_of_slab].reshape(a.shape[0], N_SUB, 2 * TF_SUB)
    return jnp.moveaxis(both, 1, 0)


N_QKV = Q_RANK + KV_RANK + QK_ROPE
N_QKV_PAD = Q_RANK + KV_RANK + LANES


PREP_ROWS = 512


def _prep_qkv_kernel(w_ref, o_ref):
    o_ref[...] = w_ref[...].T.astype(BF16)


def _prep_conv_kernel(lo_ref, hi_ref, o_ref):
    off = N_QKV % PREP_ROWS
    o_ref[...] = jnp.concatenate([lo_ref[off:, :], hi_ref[:off, :]], axis=0).T.astype(BF16)


def _prep_in(w_in_t):
    wqkv = pl.pallas_call(
        _prep_qkv_kernel,
        grid=(N_QKV_PAD // LANES,),
        in_specs=[pl.BlockSpec((LANES, D_MODEL), lambda j: (j, 0))],
        out_specs=pl.BlockSpec((D_MODEL, LANES), lambda j: (0, j)),
        out_shape=jax.ShapeDtypeStruct((D_MODEL, N_QKV_PAD), BF16),
        compiler_params=pltpu.CompilerParams(
            dimension_semantics=("parallel",), vmem_limit_bytes=VMEM_LIMIT),
        name="prep_qkv",
    )(w_in_t)
    first = N_QKV // PREP_ROWS
    wconv = pl.pallas_call(
        _prep_conv_kernel,
        grid=(3 * CONV_W // PREP_ROWS,),
        in_specs=[
            pl.BlockSpec((PREP_ROWS, D_MODEL), lambda j: (first + j, 0)),
            pl.BlockSpec((PREP_ROWS, D_MODEL), lambda j: (first + j + 1, 0)),
        ],
        out_specs=pl.BlockSpec((D_MODEL, PREP_ROWS), lambda j: (0, j)),
        out_shape=jax.ShapeDtypeStruct((D_MODEL, 3 * CONV_W), BF16),
        compiler_params=pltpu.CompilerParams(
            dimension_semantics=("parallel",), vmem_limit_bytes=VMEM_LIMIT),
        name="prep_conv",
    )(w_in_t, w_in_t)
    return wqkv, wconv


def kernel(x, c, ctx, c_ctx, w_ada, b_ada, g_mix_norm, w_in, g_q_a, w_q_b, g_kv_a, w_kv_b, conv_w, conv_b, g_mix_out, w_out, g_ffn_norm, w_up, ffn_conv_w, ffn_conv_b, w_down, g_final):
    batch, seq, d = x.shape
    assert d == D_MODEL and w_ada.shape[0] == 1
    x2 = x.reshape(batch * seq, d)

    c8 = jnp.concatenate([c, c_ctx[None], jnp.zeros((8 - batch - 1, d), F32)], axis=0)
    mod = _ada(c8, w_ada[0], b_ada[0][None])
    mod_l = mod[:batch].reshape(batch, 6, d)
    mod_c = mod[batch:batch + 1].reshape(1, 6, d)

    wqkv, wconv = _prep_in(w_in[0].T)
    wqb3 = w_q_b[0].reshape(Q_RANK, N_HEADS, QK_DIM)
    wqb = jnp.concatenate([wqb3[:, :, :QK_NOPE].reshape(Q_RANK, -1),
                           wqb3[:, :, QK_NOPE:].reshape(Q_RANK, -1)], axis=-1).astype(BF16)
    wkvb = w_kv_b[0].astype(BF16)
    cw_t = _tile_ffn_cols(ffn_conv_w[0])
    cbias_t = _tile_ffn_cols(ffn_conv_b)

    cq, sq = _rope_tables(seq)

    q, k_l, v_l, cb, z = _in_lat(x2, mod_l, g_mix_norm, wqkv, wconv, g_q_a, wqb, g_kv_a, wkvb,
                                 cq, sq, batch, seq)
    k_c, v_c = _in_ctx(ctx, mod_c, g_mix_norm, wqkv, g_kv_a, wkvb)
    att, wup_t, wdn, wout = _attn(q, k_l, v_l, k_c, v_c, w_up[0], w_down[0], w_out[0])
    x1, hf = _out(att.reshape(batch * seq, N_HEADS * HEAD), cb, z, conv_w[0], conv_b, g_mix_out,
                  wout, x2, mod_l, g_ffn_norm, batch, seq)
    y = _ffn(hf, wup_t, cw_t, cbias_t, wdn, x1, mod_l, g_final[None], batch, seq)
    return y.reshape(batch, seq, d)
```

```python
import functools

import jax
import jax.numpy as jnp
from jax import lax
from jax.experimental import pallas as pl
from jax.experimental.pallas import tpu as pltpu

F32 = jnp.float32
BF16 = jnp.bfloat16

D_MODEL = 2048
N_HEADS = 8
HEAD = 128
QK_NOPE = 128
QK_ROPE = 64
QK_DIM = QK_NOPE + QK_ROPE
Q_RANK = 512
KV_RANK = 256
CONV_W = 1024
GRID_W = 64
ROPE_BASE = 10000.0
EPS = 1e-6
LOG2E = 1.4426950408889634
FFN_DIM = 5504

V7X_VMEM_BYTES = 64 * 1024 * 1024
VMEM_LIMIT = V7X_VMEM_BYTES - 8 * 1024 * 1024

TM_IN = 512
TQ = 2048
TQ_SUB = 256
TM_OUT = 512
NORM_ROWS = 64
TM_FFN = 512
LANES = 128
TF_SUB = 512
FFN_SUBS = 1
HALO = 8
BPS = TF_SUB // LANES
FFN_BLKS = FFN_DIM // LANES
N_SUB = (FFN_DIM + TF_SUB - 1) // TF_SUB
FFN_PAD = N_SUB * TF_SUB
SLAB_OF_BLOCK = (0, 2, 1, 3)


def _rms(x, g):
    return x * lax.rsqrt(jnp.mean(x * x, axis=-1, keepdims=True) + EPS) * g


def _silu(x):
    h = 0.5 * x
    return h * (1.0 + jnp.tanh(h))


def _dot(a, b):
    return jnp.dot(a, b, preferred_element_type=F32)


def _rope(x, c, s):
    w = x.shape[-1]
    lane = lax.broadcasted_iota(jnp.int32, x.shape, 1)
    from_right = pltpu.roll(x, w - 1, 1)
    from_left = pltpu.roll(x, 1, 1)
    partner = jnp.where(lane % 2 == 0, from_right, from_left)
    return x * c + partner * s


def _ada_kernel(c_ref, w_ref, b_ref, o_ref):
    s = _silu(c_ref[...]).astype(BF16)
    o_ref[...] = _dot(s, w_ref[...].astype(BF16)) + b_ref[...]


def _ada(c8, w, b):
    n = w.shape[1]
    tn = 1024
    return pl.pallas_call(
        _ada_kernel,
        grid=(n // tn,),
        in_specs=[
            pl.BlockSpec((8, D_MODEL), lambda j: (0, 0)),
            pl.BlockSpec((D_MODEL, tn), lambda j: (0, j)),
            pl.BlockSpec((1, tn), lambda j: (0, j)),
        ],
        out_specs=pl.BlockSpec((8, tn), lambda j: (0, j)),
        out_shape=jax.ShapeDtypeStruct((8, n), F32),
        compiler_params=pltpu.CompilerParams(
            dimension_semantics=("parallel",), vmem_limit_bytes=VMEM_LIMIT),
        name="ada",
    )(c8, w, b)


def _in_lat_kernel(x_ref, mod_ref, gmix_ref, wqkv_ref, wconv_ref, gq_ref, wqb_ref, gkv_ref,
                   wkvb_ref, cq_ref, sq_ref, q_ref, k_ref, v_ref, cb_ref, z_ref):
    sh = mod_ref[0, 0:1, :]
    sc = mod_ref[0, 1:2, :]
    h = (_rms(x_ref[...], gmix_ref[...]) * (1.0 + sc) + sh).astype(BF16)

    p2 = _dot(h, wconv_ref[...])
    cb_ref[...] = p2[:, :CONV_W]
    z_ref[...] = p2[:, CONV_W:2 * CONV_W] * p2[:, 2 * CONV_W:]

    p1 = _dot(h, wqkv_ref[...])
    qa = p1[:, :Q_RANK]
    kva = p1[:, Q_RANK:Q_RANK + KV_RANK]
    kr = p1[:, Q_RANK + KV_RANK:]

    c1 = cq_ref[...]
    s1 = sq_ref[...]
    reps = N_HEADS * QK_ROPE // LANES
    cq = jnp.concatenate([c1] * reps, axis=1)
    sq = jnp.concatenate([s1] * reps, axis=1)
    scale = QK_DIM ** -0.5 * LOG2E
    q = _dot(_rms(qa, gq_ref[...]).astype(BF16), wqb_ref[...])
    qn = (q[:, :N_HEADS * QK_NOPE] * scale).astype(BF16)
    qr = (_rope(q[:, N_HEADS * QK_NOPE:], cq, sq) * scale).astype(BF16)

    kv = _dot(_rms(kva, gkv_ref[...]).astype(BF16), wkvb_ref[...]).astype(BF16)
    krr = _rope(kr, c1, s1)[:, :QK_ROPE].astype(BF16)
    for hd in range(N_HEADS):
        q_ref[0, hd, :, 0:QK_NOPE] = qn[:, hd * QK_NOPE:(hd + 1) * QK_NOPE]
        q_ref[0, hd, :, QK_NOPE:QK_DIM] = qr[:, hd * QK_ROPE:(hd + 1) * QK_ROPE]
        k_ref[0, hd, :, 0:QK_NOPE] = kv[:, hd * 2 * HEAD:hd * 2 * HEAD + QK_NOPE]
        k_ref[0, hd, :, QK_NOPE:QK_DIM] = krr
        v_ref[0, hd, :, :] = kv[:, hd * 2 * HEAD + QK_NOPE:(hd + 1) * 2 * HEAD]


def _in_lat(x2, mod_l, gmix, wqkv, wconv, gq, wqb, gkv, wkvb, cq, sq, batch, seq):
    tm = TM_IN
    tpb = seq // tm
    const = lambda b, t: (0, 0)
    one = pl.Buffered(1)
    tok = lambda b, t: (b * tpb + t, 0)
    hd4 = lambda b, t: (b, 0, t, 0)
    return pl.pallas_call(
        _in_lat_kernel,
        grid=(batch, tpb),
        in_specs=[
            pl.BlockSpec((tm, D_MODEL), tok),
            pl.BlockSpec((1, 6, D_MODEL), lambda b, t: (b, 0, 0)),
            pl.BlockSpec((1, D_MODEL), const),
            pl.BlockSpec(wqkv.shape, const, pipeline_mode=one),
            pl.BlockSpec(wconv.shape, const, pipeline_mode=one),
            pl.BlockSpec((1, Q_RANK), const),
            pl.BlockSpec(wqb.shape, const, pipeline_mode=one),
            pl.BlockSpec((1, KV_RANK), const),
            pl.BlockSpec(wkvb.shape, const, pipeline_mode=one),
            pl.BlockSpec((tm, LANES), lambda b, t: (t, 0)),
            pl.BlockSpec((tm, LANES), lambda b, t: (t, 0)),
        ],
        out_specs=[
            pl.BlockSpec((1, N_HEADS, tm, QK_DIM), hd4),
            pl.BlockSpec((1, N_HEADS, tm, QK_DIM), hd4),
            pl.BlockSpec((1, N_HEADS, tm, HEAD), hd4),
            pl.BlockSpec((tm, CONV_W), tok),
            pl.BlockSpec((tm, CONV_W), tok),
        ],
        out_shape=[
            jax.ShapeDtypeStruct((batch, N_HEADS, seq, QK_DIM), BF16),
            jax.ShapeDtypeStruct((batch, N_HEADS, seq, QK_DIM), BF16),
            jax.ShapeDtypeStruct((batch, N_HEADS, seq, HEAD), BF16),
            jax.ShapeDtypeStruct((batch * seq, CONV_W), F32),
            jax.ShapeDtypeStruct((batch * seq, CONV_W), F32),
        ],
        compiler_params=pltpu.CompilerParams(
            dimension_semantics=("parallel", "parallel"), vmem_limit_bytes=VMEM_LIMIT),
        name="in_lat",
    )(x2, mod_l, gmix, wqkv, wconv, gq, wqb, gkv, wkvb, cq, sq)


def _in_ctx_kernel(x_ref, mod_ref, gmix_ref, wkv_ref, gkv_ref, wkvb_ref, k_ref, v_ref):
    sh = mod_ref[0, 0:1, :]
    sc = mod_ref[0, 1:2, :]
    h = (_rms(x_ref[0], gmix_ref[...]) * (1.0 + sc) + sh).astype(BF16)
    p1 = _dot(h, wkv_ref[:, Q_RANK:])
    kva = p1[:, :KV_RANK]
    kr = p1[:, KV_RANK:KV_RANK + QK_ROPE].astype(BF16)
    kv = _dot(_rms(kva, gkv_ref[...]).astype(BF16), wkvb_ref[...]).astype(BF16)
    for hd in range(N_HEADS):
        k_ref[0, hd, :, 0:QK_NOPE] = kv[:, hd * 2 * HEAD:hd * 2 * HEAD + QK_NOPE]
        k_ref[0, hd, :, QK_NOPE:QK_DIM] = kr
        v_ref[0, hd, :, :] = kv[:, hd * 2 * HEAD + QK_NOPE:(hd + 1) * 2 * HEAD]


def _in_ctx(ctx, mod_c, gmix, wkv, gkv, wkvb):
    batch, n_ctx, _ = ctx.shape
    const = lambda b: (0, 0)
    return pl.pallas_call(
        _in_ctx_kernel,
        grid=(batch,),
        in_specs=[
            pl.BlockSpec((1, n_ctx, D_MODEL), lambda b: (b, 0, 0)),
            pl.BlockSpec((1, 6, D_MODEL), lambda b: (0, 0, 0)),
            pl.BlockSpec((1, D_MODEL), const),
            pl.BlockSpec(wkv.shape, const),
            pl.BlockSpec((1, KV_RANK), const),
            pl.BlockSpec(wkvb.shape, const),
        ],
        out_specs=[
            pl.BlockSpec((1, N_HEADS, n_ctx, QK_DIM), lambda b: (b, 0, 0, 0)),
            pl.BlockSpec((1, N_HEADS, n_ctx, HEAD), lambda b: (b, 0, 0, 0)),
        ],
        out_shape=[
            jax.ShapeDtypeStruct((batch, N_HEADS, n_ctx, QK_DIM), BF16),
            jax.ShapeDtypeStruct((batch, N_HEADS, n_ctx, HEAD), BF16),
        ],
        compiler_params=pltpu.CompilerParams(
            dimension_semantics=("parallel",), vmem_limit_bytes=VMEM_LIMIT),
        name="in_ctx",
    )(ctx, mod_c, gmix, wkv, gkv, wkvb)


UP_LANE_BLKS = N_SUB * 2 * BPS
DN_ROW_BLKS = N_SUB * BPS


def _up_src(c):
    c_in = jnp.minimum(c, UP_LANE_BLKS - 1)
    slab = (c_in % (2 * BPS)) // 2
    blk_in_tile = sum(jnp.where(slab == SLAB_OF_BLOCK[b], b, 0) for b in range(BPS))
    blk = (c_in // (2 * BPS)) * BPS + blk_in_tile
    src = (c_in % 2) * FFN_BLKS + jnp.minimum(blk, FFN_BLKS - 1)
    return src, (c < UP_LANE_BLKS) & (blk < FFN_BLKS)


def _attn_kernel(*refs, n_up, n_dn):
    q_ref, kl_ref, vl_ref, kc_ref, vc_ref = refs[:5]
    wu_refs = refs[5:5 + n_up]
    wd_refs = refs[5 + n_up:5 + n_up + n_dn]
    wo_ref = refs[5 + n_up + n_dn]
    o_ref, wup_ref, wdn_ref, wout_ref = refs[6 + n_up + n_dn:]

    nt = (((1,), (1,)), ((), ()))

    def with_ones(v):
        e0 = (lax.broadcasted_iota(jnp.int32, v.shape, 1) == 0).astype(BF16)
        return jnp.concatenate([v, e0], axis=1)

    vl = with_ones(vl_ref[0, 0])
    vc = with_ones(vc_ref[0, 0])
    for j in range(TQ // TQ_SUB):
        rows = slice(j * TQ_SUB, (j + 1) * TQ_SUB)
        q = q_ref[0, 0, rows, :]
        s_l = lax.dot_general(q, kl_ref[0, 0], nt, preferred_element_type=F32)
        s_c = lax.dot_general(q, kc_ref[0, 0], nt, preferred_element_type=F32)
        m = jnp.maximum(jnp.max(s_l, axis=-1, keepdims=True), jnp.max(s_c, axis=-1, keepdims=True))
        p_l = jnp.exp2(s_l - m).astype(BF16)
        p_c = jnp.exp2(s_c - m).astype(BF16)
        o = _dot(p_l, vl) + _dot(p_c, vc)
        o_ref[0, rows, :] = o[:, :HEAD] / o[:, HEAD:HEAD + 1]

    step = ((pl.program_id(0) * pl.num_programs(1) + pl.program_id(1)) * pl.num_programs(2)
            + pl.program_id(2))
    for i in range(n_up):
        _, real = _up_src(step * n_up + i)
        wup_ref[:, i * LANES:(i + 1) * LANES] = jnp.where(real, wu_refs[i][...], 0.0).astype(BF16)
    dn_blk = jnp.minimum(step, DN_ROW_BLKS // n_dn - 1) * n_dn
    for i in range(n_dn):
        wdn_ref[i * LANES:(i + 1) * LANES, :] = jnp.where(dn_blk + i < FFN_BLKS, wd_refs[i][...], 0.0).astype(BF16)
    wout_ref[...] = wo_ref[...].astype(BF16)


def _attn(q, k_l, v_l, k_c, v_c, w_up, w_down, w_out):
    batch, heads, seq, _ = q.shape
    n_ctx = k_c.shape[2]
    nq = seq // TQ
    n_steps = batch * heads * nq
    n_up = -(-UP_LANE_BLKS // n_steps)
    n_dn = -(-DN_ROW_BLKS // n_steps)
    assert DN_ROW_BLKS % n_dn == 0 and D_MODEL % n_steps == 0
    wo_rows = D_MODEL // n_steps
    step = lambda b, h, i: (b * heads + h) * nq + i
    kv_map = lambda b, h, i: (b, h, 0, 0)
    dn_out = lambda b, h, i: jnp.minimum(step(b, h, i), DN_ROW_BLKS // n_dn - 1)
    up_specs = [pl.BlockSpec((D_MODEL, LANES),
                             functools.partial(lambda b, h, i, k: (0, _up_src(step(b, h, i) * n_up + k)[0]), k=k))
                for k in range(n_up)]
    dn_specs = [pl.BlockSpec((LANES, D_MODEL),
                             functools.partial(
                                 lambda b, h, i, k: (jnp.minimum(dn_out(b, h, i) * n_dn + k, FFN_BLKS - 1), 0), k=k))
                for k in range(n_dn)]
    return pl.pallas_call(
        functools.partial(_attn_kernel, n_up=n_up, n_dn=n_dn),
        grid=(batch, heads, nq),
        in_specs=[
            pl.BlockSpec((1, 1, TQ, QK_DIM), lambda b, h, i: (b, h, i, 0)),
            pl.BlockSpec((1, 1, seq, QK_DIM), kv_map),
            pl.BlockSpec((1, 1, seq, HEAD), kv_map),
            pl.BlockSpec((1, 1, n_ctx, QK_DIM), kv_map),
            pl.BlockSpec((1, 1, n_ctx, HEAD), kv_map),
            *up_specs,
            *dn_specs,
            pl.BlockSpec((wo_rows, D_MODEL), lambda b, h, i: (step(b, h, i), 0)),
        ],
        out_specs=[
            pl.BlockSpec((1, TQ, HEAD), lambda b, h, i: (b, i, h)),
            pl.BlockSpec((D_MODEL, n_up * LANES), lambda b, h, i: (0, step(b, h, i))),
            pl.BlockSpec((n_dn * LANES, D_MODEL), lambda b, h, i: (dn_out(b, h, i), 0)),
            pl.BlockSpec((wo_rows, D_MODEL), lambda b, h, i: (step(b, h, i), 0)),
        ],
        out_shape=[
            jax.ShapeDtypeStruct((batch, seq, heads * HEAD), F32),
            jax.ShapeDtypeStruct((D_MODEL, n_steps * n_up * LANES), BF16),
            jax.ShapeDtypeStruct((FFN_PAD, D_MODEL), BF16),
            jax.ShapeDtypeStruct((D_MODEL, D_MODEL), BF16),
        ],
        compiler_params=pltpu.CompilerParams(
            dimension_semantics=("arbitrary", "arbitrary", "arbitrary"),
            vmem_limit_bytes=VMEM_LIMIT),
        name="attn",
    )(q, k_l, v_l, k_c, v_c, *([w_up] * n_up), *([w_down] * n_dn), w_out)


def _out_kernel(att_ref, cb_ref, z_ref, zp_ref, zn_ref, cw_ref, cbias_ref, gmo_ref, wout_ref,
                x_ref, mod_ref, gffn_ref, x1_ref, hf_ref, mix_ref):
    t = pl.program_id(1)
    tm = z_ref.shape[0]
    z = z_ref[...]
    row = lax.broadcasted_iota(jnp.int32, z.shape, 0)
    z_before = jnp.where(t == 0, 0.0, zp_ref[0, 7:8, :])
    z_after = jnp.where(t == pl.num_programs(1) - 1, 0.0, zn_ref[0, 0:1, :])
    z_prev = jnp.where(row == 0, z_before, pltpu.roll(z, 1, 0))
    z_next = jnp.where(row == tm - 1, z_after, pltpu.roll(z, tm - 1, 0))
    conv = cb_ref[...] * (z_prev * cw_ref[0:1, :] + z * cw_ref[1:2, :] + z_next * cw_ref[2:3, :]
                          + cbias_ref[...])

    n_att = att_ref.shape[1] // HEAD
    gt_a = mod_ref[0, 2:3, :]
    ffn_gain = gffn_ref[...] * (1.0 + mod_ref[0, 4:5, :])
    ffn_shift = mod_ref[0, 3:4, :]
    for r in range(0, tm, NORM_ROWS):
        rows = slice(r, r + NORM_ROWS)
        for g in range(D_MODEL // HEAD):
            cols = slice(g * HEAD, (g + 1) * HEAD)
            blk = att_ref[rows, cols] if g < n_att else conv[rows, (g - n_att) * HEAD:(g - n_att + 1) * HEAD]
            mix_ref[rows, cols] = _rms(blk, gmo_ref[:, cols]).astype(BF16)
    x1 = x_ref[...] + gt_a * _dot(mix_ref[...], wout_ref[...])
    x1_ref[...] = x1
    hf_ref[...] = (_rms(x1, ffn_gain) + ffn_shift).astype(BF16)


def _out(att2, cb, z, cw, cbias, gmo, wout, x2, mod_l, gffn, batch, seq):
    tm = TM_OUT
    tpb = seq // tm
    n8 = batch * seq // 8
    z3 = z.reshape(n8, 8, CONV_W)
    const = lambda b, t: (0, 0)
    tok = lambda b, t: (b * tpb + t, 0)
    return pl.pallas_call(
        _out_kernel,
        grid=(batch, tpb),
        in_specs=[
            pl.BlockSpec((tm, att2.shape[1]), tok),
            pl.BlockSpec((tm, CONV_W), tok),
            pl.BlockSpec((tm, CONV_W), tok),
            pl.BlockSpec((1, 8, CONV_W), lambda b, t: (jnp.maximum((b * tpb + t) * (tm // 8) - 1, 0), 0, 0)),
            pl.BlockSpec((1, 8, CONV_W), lambda b, t: (jnp.minimum((b * tpb + t + 1) * (tm // 8), n8 - 1), 0, 0)),
            pl.BlockSpec((3, CONV_W), const),
            pl.BlockSpec((1, CONV_W), const),
            pl.BlockSpec((1, D_MODEL), const),
            pl.BlockSpec(wout.shape, const, pipeline_mode=pl.Buffered(1)),
            pl.BlockSpec((tm, D_MODEL), tok),
            pl.BlockSpec((1, 6, D_MODEL), lambda b, t: (b, 0, 0)),
            pl.BlockSpec((1, D_MODEL), const),
        ],
        out_specs=[
            pl.BlockSpec((tm, D_MODEL), tok),
            pl.BlockSpec((tm, D_MODEL), tok),
        ],
        out_shape=[
            jax.ShapeDtypeStruct((batch * seq, D_MODEL), F32),
            jax.ShapeDtypeStruct((batch * seq, D_MODEL), BF16),
        ],
        scratch_shapes=[pltpu.VMEM((tm, D_MODEL), BF16)],
        compiler_params=pltpu.CompilerParams(
            dimension_semantics=("parallel", "parallel"), vmem_limit_bytes=VMEM_LIMIT),
        name="out_proj",
    )(att2, cb, z, z3, z3, cw, cbias, gmo, wout, x2, mod_l, gffn)


def _ffn_kernel(h_ref, hp_ref, hn_ref, wup_ref, cw_ref, cbias_ref, wdn_ref, x1_ref, mod_ref, gfin_ref,
                o_ref, hbuf_ref, u_ref):
    t = pl.program_id(1)
    f = pl.program_id(2)
    nf = pl.num_programs(2)
    tm = h_ref.shape[0]

    def step(first, last):
        if first:
            hbuf_ref[HALO:HALO + tm, :] = h_ref[...]
            hbuf_ref[0:HALO, :] = jnp.where(t == 0, jnp.zeros_like(hp_ref[0]), hp_ref[0])
            hbuf_ref[HALO + tm:, :] = jnp.where(t == pl.num_programs(1) - 1, jnp.zeros_like(hn_ref[0]), hn_ref[0])
        u_ref[...] = _dot(hbuf_ref[...], wup_ref[...])
        cw = cw_ref[f]
        y = (u_ref[pl.ds(HALO - 1, tm), :] * cw[0:1, :] + u_ref[pl.ds(HALO, tm), :] * cw[1:2, :]
             + u_ref[pl.ds(HALO + 1, tm), :] * cw[2:3, :] + cbias_ref[f])
        act = jnp.concatenate(
            [y[:, 2 * j * LANES:(2 * j + 1) * LANES] * _silu(y[:, (2 * j + 1) * LANES:(2 * j + 2) * LANES])
             for j in SLAB_OF_BLOCK], axis=1).astype(BF16)
        half = TF_SUB // 2
        d = _dot(act[:, :half], wdn_ref[:half, :]) + _dot(act[:, half:], wdn_ref[half:, :])
        acc = d if first else o_ref[...] + d
        if last:
            acc = _rms(x1_ref[...] + mod_ref[0, 5:6, :] * acc, gfin_ref[...])
        o_ref[...] = acc

    pl.when(f == 0)(lambda: step(True, False))
    pl.when((f > 0) & (f < nf - 1))(lambda: step(False, False))
    pl.when(f == nf - 1)(lambda: step(False, True))


def _ffn(hf, wup_t, cw_t, cbias_t, wdn, x1, mod_l, gfin, batch, seq):
    tm = TM_FFN
    tpb = seq // tm
    nf = N_SUB // FFN_SUBS
    tf = FFN_SUBS * TF_SUB
    n_h = batch * seq // HALO
    hf3 = hf.reshape(n_h, HALO, D_MODEL)
    tok = lambda b, t, f: (b * tpb + t, 0)
    return pl.pallas_call(
        _ffn_kernel,
        grid=(batch, tpb, nf),
        in_specs=[
            pl.BlockSpec((tm, D_MODEL), tok),
            pl.BlockSpec((1, HALO, D_MODEL),
                         lambda b, t, f: (jnp.maximum((b * tpb + t) * (tm // HALO) - 1, 0), 0, 0)),
            pl.BlockSpec((1, HALO, D_MODEL),
                         lambda b, t, f: (jnp.minimum((b * tpb + t + 1) * (tm // HALO), n_h - 1), 0, 0)),
            pl.BlockSpec((D_MODEL, 2 * TF_SUB), lambda b, t, f: (0, f)),
            pl.BlockSpec((nf, 3, 2 * TF_SUB), lambda b, t, f: (0, 0, 0)),
            pl.BlockSpec((nf, 1, 2 * TF_SUB), lambda b, t, f: (0, 0, 0)),
            pl.BlockSpec((tf, D_MODEL), lambda b, t, f: (f, 0)),
            pl.BlockSpec((tm, D_MODEL), tok),
            pl.BlockSpec((1, 6, D_MODEL), lambda b, t, f: (b, 0, 0)),
            pl.BlockSpec((1, D_MODEL), lambda b, t, f: (0, 0)),
        ],
        out_specs=pl.BlockSpec((tm, D_MODEL), tok),
        out_shape=jax.ShapeDtypeStruct((batch * seq, D_MODEL), F32),
        scratch_shapes=[
            pltpu.VMEM((tm + 2 * HALO, D_MODEL), BF16),
            pltpu.VMEM((tm + 2 * HALO, 2 * TF_SUB), F32),
        ],
        compiler_params=pltpu.CompilerParams(
            dimension_semantics=("parallel", "parallel", "arbitrary"), vmem_limit_bytes=VMEM_LIMIT),
        name="ffn",
    )(hf, hf3, hf3, wup_t, cw_t, cbias_t, wdn, x1, mod_l, gfin)


def _rope_tables(seq):
    rows = seq // GRID_W
    row = jnp.repeat(jnp.arange(rows), GRID_W).astype(F32)
    col = jnp.tile(jnp.arange(GRID_W), rows).astype(F32)
    axis_dim = QK_ROPE // 2
    inv = ROPE_BASE ** (-jnp.arange(0, axis_dim, 2, dtype=F32) / axis_dim)
    ang = jnp.concatenate([row[:, None] * inv, col[:, None] * inv], axis=-1)
    cos, sin = jnp.cos(ang), jnp.sin(ang)
    c = jnp.repeat(cos, 2, axis=-1)
    s = jnp.stack([-sin, sin], axis=-1).reshape(seq, QK_ROPE)
    return jnp.tile(c, (1, LANES // QK_ROPE)), jnp.tile(s, (1, LANES // QK_ROPE))


def _tile_ffn_cols(a):
    pad = ((0, 0), (0, FFN_PAD - FFN_DIM))
    val = jnp.pad(a[:, :FFN_DIM], pad).reshape(a.shape[0], N_SUB, BPS, 1, LANES)
    gate = jnp.pad(a[:, FFN_DIM:], pad).reshape(a.shape[0], N_SUB, BPS, 1, LANES)
    block_of_slab = [SLAB_OF_BLOCK.index(p) for p in range(BPS)]
    both = jnp.concatenate([val, gate], axis=3)[:, :, block_of_slab].reshape(a.shape[0], N_SUB, 2 * TF_SUB)
    return jnp.moveaxis(both, 1, 0)


N_QKV = Q_RANK + KV_RANK + QK_ROPE
N_QKV_PAD = Q_RANK + KV_RANK + LANES


PREP_ROWS = 512


def _prep_qkv_kernel(w_ref, o_ref):
    o_ref[...] = w_ref[...].T.astype(BF16)


def _prep_conv_kernel(lo_ref, hi_ref, o_ref):
    off = N_QKV % PREP_ROWS
    o_ref[...] = jnp.concatenate([lo_ref[off:, :], hi_ref[:off, :]], axis=0).T.astype(BF16)


def _prep_in(w_in_t):
    wqkv = pl.pallas_call(
        _prep_qkv_kernel,
        grid=(N_QKV_PAD // LANES,),
        in_specs=[pl.BlockSpec((LANES, D_MODEL), lambda j: (j, 0))],
        out_specs=pl.BlockSpec((D_MODEL, LANES), lambda j: (0, j)),
        out_shape=jax.ShapeDtypeStruct((D_MODEL, N_QKV_PAD), BF16),
        compiler_params=pltpu.CompilerParams(
            dimension_semantics=("parallel",), vmem_limit_bytes=VMEM_LIMIT),
        name="prep_qkv",
    )(w_in_t)
    first = N_QKV // PREP_ROWS
    wconv = pl.pallas_call(
        _prep_conv_kernel,
        grid=(3 * CONV_W // PREP_ROWS,),
        in_specs=[
            pl.BlockSpec((PREP_ROWS, D_MODEL), lambda j: (first + j, 0)),
            pl.BlockSpec((PREP_ROWS, D_MODEL), lambda j: (first + j + 1, 0)),
        ],
        out_specs=pl.BlockSpec((D_MODEL, PREP_ROWS), lambda j: (0, j)),
        out_shape=jax.ShapeDtypeStruct((D_MODEL, 3 * CONV_W), BF16),
        compiler_params=pltpu.CompilerParams(
            dimension_semantics=("parallel",), vmem_limit_bytes=VMEM_LIMIT),
        name="prep_conv",
    )(w_in_t, w_in_t)
    return wqkv, wconv


def kernel(x, c, ctx, c_ctx, w_ada, b_ada, g_mix_norm, w_in, g_q_a, w_q_b, g_kv_a, w_kv_b, conv_w, conv_b, g_mix_out, w_out, g_ffn_norm, w_up, ffn_conv_w, ffn_conv_b, w_down, g_final):
    batch, seq, d = x.shape
    assert d == D_MODEL and w_ada.shape[0] == 1
    x2 = x.reshape(batch * seq, d)

    c8 = jnp.concatenate([c, c_ctx[None], jnp.zeros((8 - batch - 1, d), F32)], axis=0)
    mod = _ada(c8, w_ada[0], b_ada[0][None])
    mod_l = mod[:batch].reshape(batch, 6, d)
    mod_c = mod[batch:batch + 1].reshape(1, 6, d)

    wqkv, wconv = _prep_in(w_in[0].T)
    wqb3 = w_q_b[0].reshape(Q_RANK, N_HEADS, QK_DIM)
    wqb = jnp.concatenate([wqb3[:, :, :QK_NOPE].reshape(Q_RANK, -1),
                           wqb3[:, :, QK_NOPE:].reshape(Q_RANK, -1)], axis=-1).astype(BF16)
    wkvb = w_kv_b[0].astype(BF16)
    cw_t = _tile_ffn_cols(ffn_conv_w[0])
    cbias_t = _tile_ffn_cols(ffn_conv_b)

    cq, sq = _rope_tables(seq)

    q, k_l, v_l, cb, z = _in_lat(x2, mod_l, g_mix_norm, wqkv, wconv, g_q_a, wqb, g_kv_a, wkvb,
                                 cq, sq, batch, seq)
    k_c, v_c = _in_ctx(ctx, mod_c, g_mix_norm, wqkv, g_kv_a, wkvb)
    att, wup_t, wdn, wout = _attn(q, k_l, v_l, k_c, v_c, w_up[0], w_down[0], w_out[0])
    x1, hf = _out(att.reshape(batch * seq, N_HEADS * HEAD), cb, z, conv_w[0], conv_b, g_mix_out,
                  wout, x2, mod_l, g_ffn_norm, batch, seq)
    y = _ffn(hf, wup_t, cw_t, cbias_t, wdn, x1, mod_l, g_final[None], batch, seq)
    return y.reshape(batch, seq, d)
```

```python
import functools

import jax
import jax.numpy as jnp
from jax import lax
from jax.experimental import pallas as pl
from jax.experimental.pallas import tpu as pltpu

F32 = jnp.float32
BF16 = jnp.bfloat16

D_MODEL = 2048
N_HEADS = 8
HEAD = 128
QK_NOPE = 128
QK_ROPE = 64
QK_DIM = QK_NOPE + QK_ROPE
Q_RANK = 512
KV_RANK = 256
CONV_W = 1024
GRID_W = 64
ROPE_BASE = 10000.0
EPS = 1e-6
LOG2E = 1.4426950408889634
FFN_DIM = 5504

V7X_VMEM_BYTES = 64 * 1024 * 1024
VMEM_LIMIT = V7X_VMEM_BYTES - 8 * 1024 * 1024

TM_IN = 512
TQ = 2048
TQ_SUB = 256
TM_OUT = 512
NORM_ROWS = 64
TM_FFN = 512
LANES = 128
TF_SUB = 512
FFN_SUBS = 1
HALO = 8
BPS = TF_SUB // LANES
FFN_BLKS = FFN_DIM // LANES
N_SUB = (FFN_DIM + TF_SUB - 1) // TF_SUB
FFN_PAD = N_SUB * TF_SUB
SLAB_OF_BLOCK = (0, 2, 1, 3)


def _rms(x, g):
    return x * lax.rsqrt(jnp.mean(x * x, axis=-1, keepdims=True) + EPS) * g


def _silu(x):
    h = 0.5 * x
    return h * (1.0 + jnp.tanh(h))


def _dot(a, b):
    return jnp.dot(a, b, preferred_element_type=F32)


def _rope(x, c, s):
    w = x.shape[-1]
    lane = lax.broadcasted_iota(jnp.int32, x.shape, 1)
    from_right = pltpu.roll(x, w - 1, 1)
    from_left = pltpu.roll(x, 1, 1)
    partner = jnp.where(lane % 2 == 0, from_right, from_left)
    return x * c + partner * s


N_QKV = Q_RANK + KV_RANK + QK_ROPE
PREP_ROWS = 512
N_QKV_BLKS = -(-N_QKV // PREP_ROWS)
N_CONV_BLKS = 3 * CONV_W // PREP_ROWS


def _ada_kernel(c_ref, w_ref, b_ref, lo_ref, hi_ref, qk_ref, o_ref, conv_ref, qkv_ref):
    j = pl.program_id(0)
    s = _silu(c_ref[...]).astype(BF16)
    o_ref[...] = _dot(s, w_ref[...].astype(BF16)) + b_ref[...]

    @pl.when(j < N_CONV_BLKS)
    def _():
        off = N_QKV % PREP_ROWS
        conv_ref[...] = jnp.concatenate([lo_ref[off:, :], hi_ref[:off, :]], axis=0).T.astype(BF16)

    @pl.when((j >= N_CONV_BLKS) & (j < N_CONV_BLKS + N_QKV_BLKS))
    def _():
        qkv_ref[...] = qk_ref[...].T.astype(BF16)


def _ada(c8, w, b, w_in_t):
    n = w.shape[1]
    tn = 1024
    assert n // tn >= N_CONV_BLKS + N_QKV_BLKS
    first = N_QKV // PREP_ROWS
    conv_blk = lambda j: jnp.minimum(j, N_CONV_BLKS - 1)
    qkv_blk = lambda j: jnp.clip(j - N_CONV_BLKS, 0, N_QKV_BLKS - 1)
    return pl.pallas_call(
        _ada_kernel,
        grid=(n // tn,),
        in_specs=[
            pl.BlockSpec((8, D_MODEL), lambda j: (0, 0)),
            pl.BlockSpec((D_MODEL, tn), lambda j: (0, j)),
            pl.BlockSpec((1, tn), lambda j: (0, j)),
            pl.BlockSpec((PREP_ROWS, D_MODEL), lambda j: (first + conv_blk(j), 0)),
            pl.BlockSpec((PREP_ROWS, D_MODEL), lambda j: (first + conv_blk(j) + 1, 0)),
            pl.BlockSpec((PREP_ROWS, D_MODEL), lambda j: (qkv_blk(j), 0)),
        ],
        out_specs=[
            pl.BlockSpec((8, tn), lambda j: (0, j)),
            pl.BlockSpec((D_MODEL, PREP_ROWS), lambda j: (0, conv_blk(j))),
            pl.BlockSpec((D_MODEL, PREP_ROWS), lambda j: (0, qkv_blk(j))),
        ],
        out_shape=[
            jax.ShapeDtypeStruct((8, n), F32),
            jax.ShapeDtypeStruct((D_MODEL, N_CONV_BLKS * PREP_ROWS), BF16),
            jax.ShapeDtypeStruct((D_MODEL, N_QKV_BLKS * PREP_ROWS), BF16),
        ],
        compiler_params=pltpu.CompilerParams(
            dimension_semantics=("arbitrary",), vmem_limit_bytes=VMEM_LIMIT),
        name="ada",
    )(c8, w, b, w_in_t, w_in_t, w_in_t)


def _in_lat_kernel(x_ref, mod_ref, gmix_ref, wqkv_ref, wconv_ref, gq_ref, wqb_ref, gkv_ref,
                   wkvb_ref, cq_ref, sq_ref, q_ref, k_ref, v_ref, cb_ref, z_ref):
    sh = mod_ref[0, 0:1, :]
    sc = mod_ref[0, 1:2, :]
    h = (_rms(x_ref[...], gmix_ref[...]) * (1.0 + sc) + sh).astype(BF16)

    p2 = _dot(h, wconv_ref[...])
    cb_ref[...] = p2[:, :CONV_W]
    z_ref[...] = p2[:, CONV_W:2 * CONV_W] * p2[:, 2 * CONV_W:]

    p1 = _dot(h, wqkv_ref[...])
    qa = p1[:, :Q_RANK]
    kva = p1[:, Q_RANK:Q_RANK + KV_RANK]
    kr = p1[:, Q_RANK + KV_RANK:Q_RANK + KV_RANK + LANES]

    c1 = cq_ref[...]
    s1 = sq_ref[...]
    reps = N_HEADS * QK_ROPE // LANES
    cq = jnp.concatenate([c1] * reps, axis=1)
    sq = jnp.concatenate([s1] * reps, axis=1)
    scale = QK_DIM ** -0.5 * LOG2E
    q = _dot(_rms(qa, gq_ref[...]).astype(BF16), wqb_ref[...])
    qn = (q[:, :N_HEADS * QK_NOPE] * scale).astype(BF16)
    qr = (_rope(q[:, N_HEADS * QK_NOPE:], cq, sq) * scale).astype(BF16)

    kv = _dot(_rms(kva, gkv_ref[...]).astype(BF16), wkvb_ref[...]).astype(BF16)
    krr = _rope(kr, c1, s1)[:, :QK_ROPE].astype(BF16)
    for hd in range(N_HEADS):
        q_ref[0, hd, :, 0:QK_NOPE] = qn[:, hd * QK_NOPE:(hd + 1) * QK_NOPE]
        q_ref[0, hd, :, QK_NOPE:QK_DIM] = qr[:, hd * QK_ROPE:(hd + 1) * QK_ROPE]
        k_ref[0, hd, :, 0:QK_NOPE] = kv[:, hd * 2 * HEAD:hd * 2 * HEAD + QK_NOPE]
        k_ref[0, hd, :, QK_NOPE:QK_DIM] = krr
        v_ref[0, hd, :, :] = kv[:, hd * 2 * HEAD + QK_NOPE:(hd + 1) * 2 * HEAD]


def _in_lat(x2, mod_l, gmix, wqkv, wconv, gq, wqb, gkv, wkvb, cq, sq, batch, seq):
    tm = TM_IN
    tpb = seq // tm
    const = lambda b, t: (0, 0)
    one = pl.Buffered(1)
    tok = lambda b, t: (b * tpb + t, 0)
    hd4 = lambda b, t: (b, 0, t, 0)
    return pl.pallas_call(
        _in_lat_kernel,
        grid=(batch, tpb),
        in_specs=[
            pl.BlockSpec((tm, D_MODEL), tok),
            pl.BlockSpec((1, 6, D_MODEL), lambda b, t: (b, 0, 0)),
            pl.BlockSpec((1, D_MODEL), const),
            pl.BlockSpec(wqkv.shape, const, pipeline_mode=one),
            pl.BlockSpec(wconv.shape, const, pipeline_mode=one),
            pl.BlockSpec((1, Q_RANK), const),
            pl.BlockSpec(wqb.shape, const, pipeline_mode=one),
            pl.BlockSpec((1, KV_RANK), const),
            pl.BlockSpec(wkvb.shape, const, pipeline_mode=one),
            pl.BlockSpec((tm, LANES), lambda b, t: (t, 0)),
            pl.BlockSpec((tm, LANES), lambda b, t: (t, 0)),
        ],
        out_specs=[
            pl.BlockSpec((1, N_HEADS, tm, QK_DIM), hd4),
            pl.BlockSpec((1, N_HEADS, tm, QK_DIM), hd4),
            pl.BlockSpec((1, N_HEADS, tm, HEAD), hd4),
            pl.BlockSpec((tm, CONV_W), tok),
            pl.BlockSpec((tm, CONV_W), tok),
        ],
        out_shape=[
            jax.ShapeDtypeStruct((batch, N_HEADS, seq, QK_DIM), BF16),
            jax.ShapeDtypeStruct((batch, N_HEADS, seq, QK_DIM), BF16),
            jax.ShapeDtypeStruct((batch, N_HEADS, seq, HEAD), BF16),
            jax.ShapeDtypeStruct((batch * seq, CONV_W), F32),
            jax.ShapeDtypeStruct((batch * seq, CONV_W), F32),
        ],
        compiler_params=pltpu.CompilerParams(
            dimension_semantics=("parallel", "parallel"), vmem_limit_bytes=VMEM_LIMIT),
        name="in_lat",
    )(x2, mod_l, gmix, wqkv, wconv, gq, wqb, gkv, wkvb, cq, sq)


def _in_ctx_kernel(x_ref, mod_ref, gmix_ref, wkv_ref, gkv_ref, wkvb_ref, k_ref, v_ref):
    sh = mod_ref[0, 0:1, :]
    sc = mod_ref[0, 1:2, :]
    h = (_rms(x_ref[0], gmix_ref[...]) * (1.0 + sc) + sh).astype(BF16)
    p1 = _dot(h, wkv_ref[:, Q_RANK:Q_RANK + KV_RANK + LANES])
    kva = p1[:, :KV_RANK]
    kr = p1[:, KV_RANK:KV_RANK + QK_ROPE].astype(BF16)
    kv = _dot(_rms(kva, gkv_ref[...]).astype(BF16), wkvb_ref[...]).astype(BF16)
    for hd in range(N_HEADS):
        k_ref[0, hd, :, 0:QK_NOPE] = kv[:, hd * 2 * HEAD:hd * 2 * HEAD + QK_NOPE]
        k_ref[0, hd, :, QK_NOPE:QK_DIM] = kr
        v_ref[0, hd, :, :] = kv[:, hd * 2 * HEAD + QK_NOPE:(hd + 1) * 2 * HEAD]


def _in_ctx(ctx, mod_c, gmix, wkv, gkv, wkvb):
    batch, n_ctx, _ = ctx.shape
    const = lambda b: (0, 0)
    return pl.pallas_call(
        _in_ctx_kernel,
        grid=(batch,),
        in_specs=[
            pl.BlockSpec((1, n_ctx, D_MODEL), lambda b: (b, 0, 0)),
            pl.BlockSpec((1, 6, D_MODEL), lambda b: (0, 0, 0)),
            pl.BlockSpec((1, D_MODEL), const),
            pl.BlockSpec(wkv.shape, const),
            pl.BlockSpec((1, KV_RANK), const),
            pl.BlockSpec(wkvb.shape, const),
        ],
        out_specs=[
            pl.BlockSpec((1, N_HEADS, n_ctx, QK_DIM), lambda b: (b, 0, 0, 0)),
            pl.BlockSpec((1, N_HEADS, n_ctx, HEAD), lambda b: (b, 0, 0, 0)),
        ],
        out_shape=[
            jax.ShapeDtypeStruct((batch, N_HEADS, n_ctx, QK_DIM), BF16),
            jax.ShapeDtypeStruct((batch, N_HEADS, n_ctx, HEAD), BF16),
        ],
        compiler_params=pltpu.CompilerParams(
            dimension_semantics=("parallel",), vmem_limit_bytes=VMEM_LIMIT),
        name="in_ctx",
    )(ctx, mod_c, gmix, wkv, gkv, wkvb)


UP_LANE_BLKS = N_SUB * 2 * BPS
DN_ROW_BLKS = N_SUB * BPS


def _up_src(c):
    c_in = jnp.minimum(c, UP_LANE_BLKS - 1)
    slab = (c_in % (2 * BPS)) // 2
    blk_in_tile = sum(jnp.where(slab == SLAB_OF_BLOCK[b], b, 0) for b in range(BPS))
    blk = (c_in // (2 * BPS)) * BPS + blk_in_tile
    src = (c_in % 2) * FFN_BLKS + jnp.minimum(blk, FFN_BLKS - 1)
    return src, (c < UP_LANE_BLKS) & (blk < FFN_BLKS)


def _attn_kernel(*refs, n_up, n_dn):
    q_ref, kl_ref, vl_ref, kc_ref, vc_ref = refs[:5]
    wu_refs = refs[5:5 + n_up]
    wd_refs = refs[5 + n_up:5 + n_up + n_dn]
    wo_ref = refs[5 + n_up + n_dn]
    o_ref, wup_ref, wdn_ref, wout_ref = refs[6 + n_up + n_dn:]

    step = ((pl.program_id(0) * pl.num_programs(1) + pl.program_id(1)) * pl.num_programs(2)
            + pl.program_id(2))
    for i in range(n_up):
        _, real = _up_src(step * n_up + i)
        wup_ref[:, i * LANES:(i + 1) * LANES] = jnp.where(real, wu_refs[i][...], 0.0).astype(BF16)
    dn_blk = jnp.minimum(step, DN_ROW_BLKS // n_dn - 1) * n_dn
    for i in range(n_dn):
        wdn_ref[i * LANES:(i + 1) * LANES, :] = jnp.where(dn_blk + i < FFN_BLKS, wd_refs[i][...], 0.0).astype(BF16)
    wout_ref[...] = wo_ref[...].astype(BF16)

    nt = (((1,), (1,)), ((), ()))

    def with_ones(v):
        return jnp.concatenate([v, jnp.ones_like(v)], axis=1)

    vl = with_ones(vl_ref[0, 0])
    vc = with_ones(vc_ref[0, 0])
    for j in range(TQ // TQ_SUB):
        rows = slice(j * TQ_SUB, (j + 1) * TQ_SUB)
        q = q_ref[0, 0, rows, :]
        s_l = lax.dot_general(q, kl_ref[0, 0], nt, preferred_element_type=F32)
        s_c = lax.dot_general(q, kc_ref[0, 0], nt, preferred_element_type=F32)
        m = jnp.maximum(jnp.max(s_l, axis=-1, keepdims=True), jnp.max(s_c, axis=-1, keepdims=True))
        p_l = jnp.exp2(s_l - m).astype(BF16)
        p_c = jnp.exp2(s_c - m).astype(BF16)
        o = _dot(p_l, vl) + _dot(p_c, vc)
        o_ref[0, rows, :] = o[:, :HEAD] / o[:, HEAD:]


def _attn(q, k_l, v_l, k_c, v_c, w_up, w_down, w_out):
    batch, heads, seq, _ = q.shape
    n_ctx = k_c.shape[2]
    nq = seq // TQ
    n_steps = batch * heads * nq
    n_up = -(-UP_LANE_BLKS // n_steps)
    n_dn = -(-DN_ROW_BLKS // n_steps)
    assert DN_ROW_BLKS % n_dn == 0 and D_MODEL % n_steps == 0
    wo_rows = D_MODEL // n_steps
    step = lambda b, h, i: (b * heads + h) * nq + i
    kv_map = lambda b, h, i: (b, h, 0, 0)
    dn_out = lambda b, h, i: jnp.minimum(step(b, h, i), DN_ROW_BLKS // n_dn - 1)
    up_specs = [pl.BlockSpec((D_MODEL, LANES),
                             functools.partial(lambda b, h, i, k: (0, _up_src(step(b, h, i) * n_up + k)[0]), k=k))
                for k in range(n_up)]
    dn_specs = [pl.BlockSpec((LANES, D_MODEL),
                             functools.partial(
                                 lambda b, h, i, k: (jnp.minimum(dn_out(b, h, i) * n_dn + k, FFN_BLKS - 1), 0), k=k))
                for k in range(n_dn)]
    return pl.pallas_call(
        functools.partial(_attn_kernel, n_up=n_up, n_dn=n_dn),
        grid=(batch, heads, nq),
        in_specs=[
            pl.BlockSpec((1, 1, TQ, QK_DIM), lambda b, h, i: (b, h, i, 0)),
            pl.BlockSpec((1, 1, seq, QK_DIM), kv_map),
            pl.BlockSpec((1, 1, seq, HEAD), kv_map),
            pl.BlockSpec((1, 1, n_ctx, QK_DIM), kv_map),
            pl.BlockSpec((1, 1, n_ctx, HEAD), kv_map),
            *up_specs,
            *dn_specs,
            pl.BlockSpec((wo_rows, D_MODEL), lambda b, h, i: (step(b, h, i), 0)),
        ],
        out_specs=[
            pl.BlockSpec((1, TQ, HEAD), lambda b, h, i: (b, i, h)),
            pl.BlockSpec((D_MODEL, n_up * LANES), lambda b, h, i: (0, step(b, h, i))),
            pl.BlockSpec((n_dn * LANES, D_MODEL), lambda b, h, i: (dn_out(b, h, i), 0)),
            pl.BlockSpec((wo_rows, D_MODEL), lambda b, h, i: (step(b, h, i), 0)),
        ],
        out_shape=[
            jax.ShapeDtypeStruct((batch, seq, heads * HEAD), F32),
            jax.ShapeDtypeStruct((D_MODEL, n_steps * n_up * LANES), BF16),
            jax.ShapeDtypeStruct((FFN_PAD, D_MODEL), BF16),
            jax.ShapeDtypeStruct((D_MODEL, D_MODEL), BF16),
        ],
        compiler_params=pltpu.CompilerParams(
            dimension_semantics=("arbitrary", "arbitrary", "arbitrary"),
            vmem_limit_bytes=VMEM_LIMIT),
        name="attn",
    )(q, k_l, v_l, k_c, v_c, *([w_up] * n_up), *([w_down] * n_dn), w_out)


def _out_kernel(att_ref, cb_ref, z_ref, zp_ref, zn_ref, cw_ref, cbias_ref, gmo_ref, wout_ref,
                x_ref, mod_ref, gffn_ref, x1_ref, hf_ref, mix_ref):
    t = pl.program_id(1)
    tm = z_ref.shape[0]
    z = z_ref[...]
    row = lax.broadcasted_iota(jnp.int32, z.shape, 0)
    z_before = jnp.where(t == 0, 0.0, zp_ref[0, 7:8, :])
    z_after = jnp.where(t == pl.num_programs(1) - 1, 0.0, zn_ref[0, 0:1, :])
    z_prev = jnp.where(row == 0, z_before, pltpu.roll(z, 1, 0))
    z_next = jnp.where(row == tm - 1, z_after, pltpu.roll(z, tm - 1, 0))
    conv = cb_ref[...] * (z_prev * cw_ref[0:1, :] + z * cw_ref[1:2, :] + z_next * cw_ref[2:3, :]
                          + cbias_ref[...])

    n_att = att_ref.shape[1] // HEAD
    gt_a = mod_ref[0, 2:3, :]
    ffn_gain = gffn_ref[...] * (1.0 + mod_ref[0, 4:5, :])
    ffn_shift = mod_ref[0, 3:4, :]
    for r in range(0, tm, NORM_ROWS):
        rows = slice(r, r + NORM_ROWS)
        for g in range(D_MODEL // HEAD):
            cols = slice(g * HEAD, (g + 1) * HEAD)
            blk = att_ref[rows, cols] if g < n_att else conv[rows, (g - n_att) * HEAD:(g - n_att + 1) * HEAD]
            mix_ref[rows, cols] = _rms(blk, gmo_ref[:, cols]).astype(BF16)
    x1 = x_ref[...] + gt_a * _dot(mix_ref[...], wout_ref[...])
    x1_ref[...] = x1
    hf_ref[...] = (_rms(x1, ffn_gain) + ffn_shift).astype(BF16)


def _out(att2, cb, z, cw, cbias, gmo, wout, x2, mod_l, gffn, batch, seq):
    tm = TM_OUT
    tpb = seq // tm
    n8 = batch * seq // 8
    z3 = z.reshape(n8, 8, CONV_W)
    const = lambda b, t: (0, 0)
    tok = lambda b, t: (b * tpb + t, 0)
    return pl.pallas_call(
        _out_kernel,
        grid=(batch, tpb),
        in_specs=[
            pl.BlockSpec((tm, att2.shape[1]), tok),
            pl.BlockSpec((tm, CONV_W), tok),
            pl.BlockSpec((tm, CONV_W), tok),
            pl.BlockSpec((1, 8, CONV_W), lambda b, t: (jnp.maximum((b * tpb + t) * (tm // 8) - 1, 0), 0, 0)),
            pl.BlockSpec((1, 8, CONV_W), lambda b, t: (jnp.minimum((b * tpb + t + 1) * (tm // 8), n8 - 1), 0, 0)),
            pl.BlockSpec((3, CONV_W), const),
            pl.BlockSpec((1, CONV_W), const),
            pl.BlockSpec((1, D_MODEL), const),
            pl.BlockSpec(wout.shape, const, pipeline_mode=pl.Buffered(1)),
            pl.BlockSpec((tm, D_MODEL), tok),
            pl.BlockSpec((1, 6, D_MODEL), lambda b, t: (b, 0, 0)),
            pl.BlockSpec((1, D_MODEL), const),
        ],
        out_specs=[
            pl.BlockSpec((tm, D_MODEL), tok),
            pl.BlockSpec((tm, D_MODEL), tok),
        ],
        out_shape=[
            jax.ShapeDtypeStruct((batch * seq, D_MODEL), F32),
            jax.ShapeDtypeStruct((batch * seq, D_MODEL), BF16),
        ],
        scratch_shapes=[pltpu.VMEM((tm, D_MODEL), BF16)],
        compiler_params=pltpu.CompilerParams(
            dimension_semantics=("parallel", "parallel"), vmem_limit_bytes=VMEM_LIMIT),
        name="out_proj",
    )(att2, cb, z, z3, z3, cw, cbias, gmo, wout, x2, mod_l, gffn)


def _ffn_kernel(h_ref, hp_ref, hn_ref, wup_ref, cw_ref, cbias_ref, wdn_ref, x1_ref, mod_ref, gfin_ref,
                o_ref, hbuf_ref, u_ref):
    t = pl.program_id(1)
    f = pl.program_id(2)
    nf = pl.num_programs(2)
    tm = h_ref.shape[0]

    def step(first, last):
        if first:
            hbuf_ref[HALO:HALO + tm, :] = h_ref[...]
            hbuf_ref[0:HALO, :] = jnp.where(t == 0, jnp.zeros_like(hp_ref[0]), hp_ref[0])
            hbuf_ref[HALO + tm:, :] = jnp.where(t == pl.num_programs(1) - 1, jnp.zeros_like(hn_ref[0]), hn_ref[0])
        u_ref[...] = _dot(hbuf_ref[...], wup_ref[...])
        cw = cw_ref[f]
        y = (u_ref[pl.ds(HALO - 1, tm), :] * cw[0:1, :] + u_ref[pl.ds(HALO, tm), :] * cw[1:2, :]
             + u_ref[pl.ds(HALO + 1, tm), :] * cw[2:3, :] + cbias_ref[f])
        act = jnp.concatenate(
            [y[:, 2 * j * LANES:(2 * j + 1) * LANES] * _silu(y[:, (2 * j + 1) * LANES:(2 * j + 2) * LANES])
             for j in SLAB_OF_BLOCK], axis=1).astype(BF16)
        half = TF_SUB // 2
        d = _dot(act[:, :half], wdn_ref[:half, :]) + _dot(act[:, half:], wdn_ref[half:, :])
        acc = d if first else o_ref[...] + d
        if last:
            acc = _rms(x1_ref[...] + mod_ref[0, 5:6, :] * acc, gfin_ref[...])
        o_ref[...] = acc

    pl.when(f == 0)(lambda: step(True, False))
    pl.when((f > 0) & (f < nf - 1))(lambda: step(False, False))
    pl.when(f == nf - 1)(lambda: step(False, True))


def _ffn(hf, wup_t, cw_t, cbias_t, wdn, x1, mod_l, gfin, batch, seq):
    tm = TM_FFN
    tpb = seq // tm
    nf = N_SUB // FFN_SUBS
    tf = FFN_SUBS * TF_SUB
    n_h = batch * seq // HALO
    hf3 = hf.reshape(n_h, HALO, D_MODEL)
    tok = lambda b, t, f: (b * tpb + t, 0)
    return pl.pallas_call(
        _ffn_kernel,
        grid=(batch, tpb, nf),
        in_specs=[
            pl.BlockSpec((tm, D_MODEL), tok),
            pl.BlockSpec((1, HALO, D_MODEL),
                         lambda b, t, f: (jnp.maximum((b * tpb + t) * (tm // HALO) - 1, 0), 0, 0)),
            pl.BlockSpec((1, HALO, D_MODEL),
                         lambda b, t, f: (jnp.minimum((b * tpb + t + 1) * (tm // HALO), n_h - 1), 0, 0)),
            pl.BlockSpec((D_MODEL, 2 * TF_SUB), lambda b, t, f: (0, f)),
            pl.BlockSpec((nf, 3, 2 * TF_SUB), lambda b, t, f: (0, 0, 0)),
            pl.BlockSpec((nf, 1, 2 * TF_SUB), lambda b, t, f: (0, 0, 0)),
            pl.BlockSpec((tf, D_MODEL), lambda b, t, f: (f, 0)),
            pl.BlockSpec((tm, D_MODEL), tok),
            pl.BlockSpec((1, 6, D_MODEL), lambda b, t, f: (b, 0, 0)),
            pl.BlockSpec((1, D_MODEL), lambda b, t, f: (0, 0)),
        ],
        out_specs=pl.BlockSpec((tm, D_MODEL), tok),
        out_shape=jax.ShapeDtypeStruct((batch * seq, D_MODEL), F32),
        scratch_shapes=[
            pltpu.VMEM((tm + 2 * HALO, D_MODEL), BF16),
            pltpu.VMEM((tm + 2 * HALO, 2 * TF_SUB), F32),
        ],
        compiler_params=pltpu.CompilerParams(
            dimension_semantics=("parallel", "parallel", "arbitrary"), vmem_limit_bytes=VMEM_LIMIT),
        name="ffn",
    )(hf, hf3, hf3, wup_t, cw_t, cbias_t, wdn, x1, mod_l, gfin)


def _rope_tables(seq):
    rows = seq // GRID_W
    row = jnp.repeat(jnp.arange(rows), GRID_W).astype(F32)
    col = jnp.tile(jnp.arange(GRID_W), rows).astype(F32)
    axis_dim = QK_ROPE // 2
    inv = ROPE_BASE ** (-jnp.arange(0, axis_dim, 2, dtype=F32) / axis_dim)
    ang = jnp.concatenate([row[:, None] * inv, col[:, None] * inv], axis=-1)
    cos, sin = jnp.cos(ang), jnp.sin(ang)
    c = jnp.repeat(cos, 2, axis=-1)
    s = jnp.stack([-sin, sin], axis=-1).reshape(seq, QK_ROPE)
    return jnp.tile(c, (1, LANES // QK_ROPE)), jnp.tile(s, (1, LANES // QK_ROPE))


def _tile_ffn_cols(a):
    pad = ((0, 0), (0, FFN_PAD - FFN_DIM))
    val = jnp.pad(a[:, :FFN_DIM], pad).reshape(a.shape[0], N_SUB, BPS, 1, LANES)
    gate = jnp.pad(a[:, FFN_DIM:], pad).reshape(a.shape[0], N_SUB, BPS, 1, LANES)
    block_of_slab = [SLAB_OF_BLOCK.index(p) for p in range(BPS)]
    both = jnp.concatenate([val, gate], axis=3)[:, :, block_of_slab].reshape(a.shape[0], N_SUB, 2 * TF_SUB)
    return jnp.moveaxis(both, 1, 0)


def kernel(x, c, ctx, c_ctx, w_ada, b_ada, g_mix_norm, w_in, g_q_a, w_q_b, g_kv_a, w_kv_b, conv_w, conv_b, g_mix_out, w_out, g_ffn_norm, w_up, ffn_conv_w, ffn_conv_b, w_down, g_final):
    batch, seq, d = x.shape
    assert d == D_MODEL and w_ada.shape[0] == 1
    x2 = x.reshape(batch * seq, d)

    c8 = jnp.concatenate([c, c_ctx[None], jnp.zeros((8 - batch - 1, d), F32)], axis=0)
    mod, wconv, wqkv = _ada(c8, w_ada[0], b_ada[0][None], w_in[0].T)
    mod_l = mod[:batch].reshape(batch, 6, d)
    mod_c = mod[batch:batch + 1].reshape(1, 6, d)

    wqb3 = w_q_b[0].reshape(Q_RANK, N_HEADS, QK_DIM)
    wqb = jnp.concatenate([wqb3[:, :, :QK_NOPE].reshape(Q_RANK, -1),
                           wqb3[:, :, QK_NOPE:].reshape(Q_RANK, -1)], axis=-1).astype(BF16)
    wkvb = w_kv_b[0].astype(BF16)
    cw_t = _tile_ffn_cols(ffn_conv_w[0])
    cbias_t = _tile_ffn_cols(ffn_conv_b)

    cq, sq = _rope_tables(seq)

    q, k_l, v_l, cb, z = _in_lat(x2, mod_l, g_mix_norm, wqkv, wconv, g_q_a, wqb, g_kv_a, wkvb,
                                 cq, sq, batch, seq)
    k_c, v_c = _in_ctx(ctx, mod_c, g_mix_norm, wqkv, g_kv_a, wkvb)
    att, wup_t, wdn, wout = _attn(q, k_l, v_l, k_c, v_c, w_up[0], w_down[0], w_out[0])
    x1, hf = _out(att.reshape(batch * seq, N_HEADS * HEAD), cb, z, conv_w[0], conv_b, g_mix_out,
                  wout, x2, mod_l, g_ffn_norm, batch, seq)
    y = _ffn(hf, wup_t, cw_t, cbias_t, wdn, x1, mod_l, g_final[None], batch, seq)
    return y.reshape(batch, seq, d)
```

```python
import functools

import jax
import jax.numpy as jnp
from jax import lax
from jax.experimental import pallas as pl
from jax.experimental.pallas import tpu as pltpu

F32 = jnp.float32
BF16 = jnp.bfloat16

D_MODEL = 2048
N_HEADS = 8
HEAD = 128
QK_NOPE = 128
QK_ROPE = 64
QK_DIM = QK_NOPE + QK_ROPE
Q_RANK = 512
KV_RANK = 256
CONV_W = 1024
GRID_W = 64
ROPE_BASE = 10000.0
EPS = 1e-6
LOG2E = 1.4426950408889634
FFN_DIM = 5504

V7X_VMEM_BYTES = 64 * 1024 * 1024
VMEM_LIMIT = V7X_VMEM_BYTES - 8 * 1024 * 1024

TM_IN = 512
TQ = 2048
TQ_SUB = 256
TM_OUT = 512
NORM_ROWS = 64
TM_FFN = 512
LANES = 128
TF_SUB = 512
FFN_SUBS = 1
HALO = 8
BPS = TF_SUB // LANES
FFN_BLKS = FFN_DIM // LANES
N_SUB = (FFN_DIM + TF_SUB - 1) // TF_SUB
FFN_PAD = N_SUB * TF_SUB
SLAB_OF_BLOCK = (0, 2, 1, 3)


def _rms(x, g):
    return x * lax.rsqrt(jnp.mean(x * x, axis=-1, keepdims=True) + EPS) * g


def _silu(x):
    h = 0.5 * x
    return h * (1.0 + jnp.tanh(h))


def _dot(a, b):
    return jnp.dot(a, b, preferred_element_type=F32)


def _rope(x, c, s):
    w = x.shape[-1]
    lane = lax.broadcasted_iota(jnp.int32, x.shape, 1)
    from_right = pltpu.roll(x, w - 1, 1)
    from_left = pltpu.roll(x, 1, 1)
    partner = jnp.where(lane % 2 == 0, from_right, from_left)
    return x * c + partner * s


N_QKV = Q_RANK + KV_RANK + QK_ROPE
PREP_ROWS = 512
N_QKV_BLKS = -(-N_QKV // PREP_ROWS)
N_CONV_BLKS = 3 * CONV_W // PREP_ROWS


def _ada_kernel(c_ref, w_ref, b_ref, wi_ref, o_ref, conv_ref, qkv_ref, prev_ref):
    j = pl.program_id(0)
    s = _silu(c_ref[...]).astype(BF16)
    o_ref[...] = _dot(s, w_ref[...].astype(BF16)) + b_ref[...]

    @pl.when(j < N_QKV_BLKS)
    def _():
        qkv_ref[...] = wi_ref[...].T.astype(BF16)

    @pl.when((j >= N_QKV_BLKS) & (j < N_QKV_BLKS + N_CONV_BLKS))
    def _():
        off = N_QKV % PREP_ROWS
        conv_ref[...] = jnp.concatenate([prev_ref[off:, :], wi_ref[:off, :]], axis=0).T.astype(BF16)

    @pl.when(j < N_QKV_BLKS + N_CONV_BLKS - 1)
    def _():
        prev_ref[...] = wi_ref[...]


def _ada(c8, w, b, w_in_t):
    n = w.shape[1]
    tn = 1024
    n_blks = N_QKV_BLKS + N_CONV_BLKS
    assert n // tn >= n_blks and N_QKV_BLKS == N_QKV // PREP_ROWS + 1
    conv_blk = lambda j: jnp.clip(j - N_QKV_BLKS, 0, N_CONV_BLKS - 1)
    return pl.pallas_call(
        _ada_kernel,
        grid=(n // tn,),
        in_specs=[
            pl.BlockSpec((8, D_MODEL), lambda j: (0, 0)),
            pl.BlockSpec((D_MODEL, tn), lambda j: (0, j)),
            pl.BlockSpec((1, tn), lambda j: (0, j)),
            pl.BlockSpec((PREP_ROWS, D_MODEL), lambda j: (jnp.minimum(j, n_blks - 1), 0)),
        ],
        out_specs=[
            pl.BlockSpec((8, tn), lambda j: (0, j)),
            pl.BlockSpec((D_MODEL, PREP_ROWS), lambda j: (0, conv_blk(j))),
            pl.BlockSpec((D_MODEL, PREP_ROWS), lambda j: (0, jnp.minimum(j, N_QKV_BLKS - 1))),
        ],
        out_shape=[
            jax.ShapeDtypeStruct((8, n), F32),
            jax.ShapeDtypeStruct((D_MODEL, N_CONV_BLKS * PREP_ROWS), BF16),
            jax.ShapeDtypeStruct((D_MODEL, N_QKV_BLKS * PREP_ROWS), BF16),
        ],
        scratch_shapes=[pltpu.VMEM((PREP_ROWS, D_MODEL), F32)],
        compiler_params=pltpu.CompilerParams(
            dimension_semantics=("arbitrary",), vmem_limit_bytes=VMEM_LIMIT),
        name="ada",
    )(c8, w, b, w_in_t)


def _in_lat_kernel(x_ref, mod_ref, gmix_ref, wqkv_ref, wconv_ref, gq_ref, wqb_ref, gkv_ref,
                   wkvb_ref, cq_ref, sq_ref, q_ref, k_ref, v_ref, cb_ref, z_ref):
    sh = mod_ref[0, 0:1, :]
    sc = mod_ref[0, 1:2, :]
    h = (_rms(x_ref[...], gmix_ref[...]) * (1.0 + sc) + sh).astype(BF16)

    p2 = _dot(h, wconv_ref[...])
    cb_ref[...] = p2[:, :CONV_W]
    z_ref[...] = p2[:, CONV_W:2 * CONV_W] * p2[:, 2 * CONV_W:]

    p1 = _dot(h, wqkv_ref[...])
    qa = p1[:, :Q_RANK]
    kva = p1[:, Q_RANK:Q_RANK + KV_RANK]
    kr = p1[:, Q_RANK + KV_RANK:Q_RANK + KV_RANK + LANES]

    c1 = cq_ref[...]
    s1 = sq_ref[...]
    reps = N_HEADS * QK_ROPE // LANES
    cq = jnp.concatenate([c1] * reps, axis=1)
    sq = jnp.concatenate([s1] * reps, axis=1)
    scale = QK_DIM ** -0.5 * LOG2E
    q = _dot(_rms(qa, gq_ref[...]).astype(BF16), wqb_ref[...])
    qn = (q[:, :N_HEADS * QK_NOPE] * scale).astype(BF16)
    qr = (_rope(q[:, N_HEADS * QK_NOPE:], cq, sq) * scale).astype(BF16)

    kv = _dot(_rms(kva, gkv_ref[...]).astype(BF16), wkvb_ref[...]).astype(BF16)
    krr = _rope(kr, c1, s1)[:, :QK_ROPE].astype(BF16)
    for hd in range(N_HEADS):
        q_ref[0, hd, :, 0:QK_NOPE] = qn[:, hd * QK_NOPE:(hd + 1) * QK_NOPE]
        q_ref[0, hd, :, QK_NOPE:QK_DIM] = qr[:, hd * QK_ROPE:(hd + 1) * QK_ROPE]
        k_ref[0, hd, :, 0:QK_NOPE] = kv[:, hd * 2 * HEAD:hd * 2 * HEAD + QK_NOPE]
        k_ref[0, hd, :, QK_NOPE:QK_DIM] = krr
        v_ref[0, hd, :, :] = kv[:, hd * 2 * HEAD + QK_NOPE:(hd + 1) * 2 * HEAD]


def _in_lat(x2, mod_l, gmix, wqkv, wconv, gq, wqb, gkv, wkvb, cq, sq, batch, seq):
    tm = TM_IN
    tpb = seq // tm
    const = lambda b, t: (0, 0)
    one = pl.Buffered(1)
    tok = lambda b, t: (b * tpb + t, 0)
    hd4 = lambda b, t: (b, 0, t, 0)
    return pl.pallas_call(
        _in_lat_kernel,
        grid=(batch, tpb),
        in_specs=[
            pl.BlockSpec((tm, D_MODEL), tok),
            pl.BlockSpec((1, 6, D_MODEL), lambda b, t: (b, 0, 0)),
            pl.BlockSpec((1, D_MODEL), const),
            pl.BlockSpec(wqkv.shape, const, pipeline_mode=one),
            pl.BlockSpec(wconv.shape, const, pipeline_mode=one),
            pl.BlockSpec((1, Q_RANK), const),
            pl.BlockSpec(wqb.shape, const, pipeline_mode=one),
            pl.BlockSpec((1, KV_RANK), const),
            pl.BlockSpec(wkvb.shape, const, pipeline_mode=one),
            pl.BlockSpec((tm, LANES), lambda b, t: (t, 0)),
            pl.BlockSpec((tm, LANES), lambda b, t: (t, 0)),
        ],
        out_specs=[
            pl.BlockSpec((1, N_HEADS, tm, QK_DIM), hd4),
            pl.BlockSpec((1, N_HEADS, tm, QK_DIM), hd4),
            pl.BlockSpec((1, N_HEADS, tm, HEAD), hd4),
            pl.BlockSpec((tm, CONV_W), tok),
            pl.BlockSpec((tm, CONV_W), tok),
        ],
        out_shape=[
            jax.ShapeDtypeStruct((batch, N_HEADS, seq, QK_DIM), BF16),
            jax.ShapeDtypeStruct((batch, N_HEADS, seq, QK_DIM), BF16),
            jax.ShapeDtypeStruct((batch, N_HEADS, seq, HEAD), BF16),
            jax.ShapeDtypeStruct((batch * seq, CONV_W), F32),
            jax.ShapeDtypeStruct((batch * seq, CONV_W), F32),
        ],
        compiler_params=pltpu.CompilerParams(
            dimension_semantics=("parallel", "parallel"), vmem_limit_bytes=VMEM_LIMIT),
        name="in_lat",
    )(x2, mod_l, gmix, wqkv, wconv, gq, wqb, gkv, wkvb, cq, sq)


def _in_ctx_kernel(x_ref, mod_ref, gmix_ref, wkv_ref, gkv_ref, wkvb_ref, k_ref, v_ref):
    sh = mod_ref[0, 0:1, :]
    sc = mod_ref[0, 1:2, :]
    h = (_rms(x_ref[0], gmix_ref[...]) * (1.0 + sc) + sh).astype(BF16)
    p1 = _dot(h, wkv_ref[:, Q_RANK:Q_RANK + KV_RANK + LANES])
    kva = p1[:, :KV_RANK]
    kr = p1[:, KV_RANK:KV_RANK + QK_ROPE].astype(BF16)
    kv = _dot(_rms(kva, gkv_ref[...]).astype(BF16), wkvb_ref[...]).astype(BF16)
    for hd in range(N_HEADS):
        k_ref[0, hd, :, 0:QK_NOPE] = kv[:, hd * 2 * HEAD:hd * 2 * HEAD + QK_NOPE]
        k_ref[0, hd, :, QK_NOPE:QK_DIM] = kr
        v_ref[0, hd, :, :] = kv[:, hd * 2 * HEAD + QK_NOPE:(hd + 1) * 2 * HEAD]


def _in_ctx(ctx, mod_c, gmix, wkv, gkv, wkvb):
    batch, n_ctx, _ = ctx.shape
    const = lambda b: (0, 0)
    return pl.pallas_call(
        _in_ctx_kernel,
        grid=(batch,),
        in_specs=[
            pl.BlockSpec((1, n_ctx, D_MODEL), lambda b: (b, 0, 0)),
            pl.BlockSpec((1, 6, D_MODEL), lambda b: (0, 0, 0)),
            pl.BlockSpec((1, D_MODEL), const),
            pl.BlockSpec(wkv.shape, const),
            pl.BlockSpec((1, KV_RANK), const),
            pl.BlockSpec(wkvb.shape, const),
        ],
        out_specs=[
            pl.BlockSpec((1, N_HEADS, n_ctx, QK_DIM), lambda b: (b, 0, 0, 0)),
            pl.BlockSpec((1, N_HEADS, n_ctx, HEAD), lambda b: (b, 0, 0, 0)),
        ],
        out_shape=[
            jax.ShapeDtypeStruct((batch, N_HEADS, n_ctx, QK_DIM), BF16),
            jax.ShapeDtypeStruct((batch, N_HEADS, n_ctx, HEAD), BF16),
        ],
        compiler_params=pltpu.CompilerParams(
            dimension_semantics=("parallel",), vmem_limit_bytes=VMEM_LIMIT),
        name="in_ctx",
    )(ctx, mod_c, gmix, wkv, gkv, wkvb)


UP_LANE_BLKS = N_SUB * 2 * BPS
DN_ROW_BLKS = N_SUB * BPS


def _up_src(c):
    c_in = jnp.minimum(c, UP_LANE_BLKS - 1)
    slab = (c_in % (2 * BPS)) // 2
    blk_in_tile = sum(jnp.where(slab == SLAB_OF_BLOCK[b], b, 0) for b in range(BPS))
    blk = (c_in // (2 * BPS)) * BPS + blk_in_tile
    src = (c_in % 2) * FFN_BLKS + jnp.minimum(blk, FFN_BLKS - 1)
    return src, (c < UP_LANE_BLKS) & (blk < FFN_BLKS)


def _attn_kernel(*refs, n_up, n_dn):
    q_ref, kl_ref, vl_ref, kc_ref, vc_ref = refs[:5]
    wu_refs = refs[5:5 + n_up]
    wd_refs = refs[5 + n_up:5 + n_up + n_dn]
    wo_ref = refs[5 + n_up + n_dn]
    o_ref, wup_ref, wdn_ref, wout_ref = refs[6 + n_up + n_dn:]

    step = ((pl.program_id(0) * pl.num_programs(1) + pl.program_id(1)) * pl.num_programs(2)
            + pl.program_id(2))
    for i in range(n_up):
        _, real = _up_src(step * n_up + i)
        wup_ref[:, i * LANES:(i + 1) * LANES] = jnp.where(real, wu_refs[i][...], 0.0).astype(BF16)
    dn_blk = jnp.minimum(step, DN_ROW_BLKS // n_dn - 1) * n_dn
    for i in range(n_dn):
        wdn_ref[i * LANES:(i + 1) * LANES, :] = jnp.where(dn_blk + i < FFN_BLKS, wd_refs[i][...], 0.0).astype(BF16)
    wout_ref[...] = wo_ref[...].astype(BF16)

    nt = (((1,), (1,)), ((), ()))

    def with_ones(v):
        return jnp.concatenate([v, jnp.ones_like(v)], axis=1)

    vl = with_ones(vl_ref[0, 0])
    vc = with_ones(vc_ref[0, 0])
    for j in range(TQ // TQ_SUB):
        rows = slice(j * TQ_SUB, (j + 1) * TQ_SUB)
        q = q_ref[0, 0, rows, :]
        s_l = lax.dot_general(q, kl_ref[0, 0], nt, preferred_element_type=F32)
        s_c = lax.dot_general(q, kc_ref[0, 0], nt, preferred_element_type=F32)
        m = jnp.maximum(jnp.max(s_l, axis=-1, keepdims=True), jnp.max(s_c, axis=-1, keepdims=True))
        p_l = jnp.exp2(s_l - m).astype(BF16)
        p_c = jnp.exp2(s_c - m).astype(BF16)
        o = _dot(p_l, vl) + _dot(p_c, vc)
        o_ref[0, rows, :] = o[:, :HEAD] / o[:, HEAD:]


def _attn(q, k_l, v_l, k_c, v_c, w_up, w_down, w_out):
    batch, heads, seq, _ = q.shape
    n_ctx = k_c.shape[2]
    nq = seq // TQ
    n_steps = batch * heads * nq
    n_up = -(-UP_LANE_BLKS // n_steps)
    n_dn = -(-DN_ROW_BLKS // n_steps)
    assert DN_ROW_BLKS % n_dn == 0 and D_MODEL % n_steps == 0
    wo_rows = D_MODEL // n_steps
    step = lambda b, h, i: (b * heads + h) * nq + i
    kv_map = lambda b, h, i: (b, h, 0, 0)
    dn_out = lambda b, h, i: jnp.minimum(step(b, h, i), DN_ROW_BLKS // n_dn - 1)
    up_specs = [pl.BlockSpec((D_MODEL, LANES),
                             functools.partial(lambda b, h, i, k: (0, _up_src(step(b, h, i) * n_up + k)[0]), k=k))
                for k in range(n_up)]
    dn_specs = [pl.BlockSpec((LANES, D_MODEL),
                             functools.partial(
                                 lambda b, h, i, k: (jnp.minimum(dn_out(b, h, i) * n_dn + k, FFN_BLKS - 1), 0), k=k))
                for k in range(n_dn)]
    return pl.pallas_call(
        functools.partial(_attn_kernel, n_up=n_up, n_dn=n_dn),
        grid=(batch, heads, nq),
        in_specs=[
            pl.BlockSpec((1, 1, TQ, QK_DIM), lambda b, h, i: (b, h, i, 0)),
            pl.BlockSpec((1, 1, seq, QK_DIM), kv_map),
            pl.BlockSpec((1, 1, seq, HEAD), kv_map),
            pl.BlockSpec((1, 1, n_ctx, QK_DIM), kv_map),
            pl.BlockSpec((1, 1, n_ctx, HEAD), kv_map),
            *up_specs,
            *dn_specs,
            pl.BlockSpec((wo_rows, D_MODEL), lambda b, h, i: (step(b, h, i), 0)),
        ],
        out_specs=[
            pl.BlockSpec((1, TQ, HEAD), lambda b, h, i: (b, i, h)),
            pl.BlockSpec((D_MODEL, n_up * LANES), lambda b, h, i: (0, step(b, h, i))),
            pl.BlockSpec((n_dn * LANES, D_MODEL), lambda b, h, i: (dn_out(b, h, i), 0)),
            pl.BlockSpec((wo_rows, D_MODEL), lambda b, h, i: (step(b, h, i), 0)),
        ],
        out_shape=[
            jax.ShapeDtypeStruct((batch, seq, heads * HEAD), F32),
            jax.ShapeDtypeStruct((D_MODEL, n_steps * n_up * LANES), BF16),
            jax.ShapeDtypeStruct((FFN_PAD, D_MODEL), BF16),
            jax.ShapeDtypeStruct((D_MODEL, D_MODEL), BF16),
        ],
        compiler_params=pltpu.CompilerParams(
            dimension_semantics=("arbitrary", "arbitrary", "arbitrary"),
            vmem_limit_bytes=VMEM_LIMIT),
        name="attn",
    )(q, k_l, v_l, k_c, v_c, *([w_up] * n_up), *([w_down] * n_dn), w_out)


def _out_kernel(att_ref, cb_ref, z_ref, zp_ref, zn_ref, cw_ref, cbias_ref, gmo_ref, wout_ref,
                x_ref, mod_ref, gffn_ref, x1_ref, hf_ref, mix_ref):
    t = pl.program_id(1)
    tm = z_ref.shape[0]
    z = z_ref[...]
    row = lax.broadcasted_iota(jnp.int32, z.shape, 0)
    z_before = jnp.where(t == 0, 0.0, zp_ref[0, 7:8, :])
    z_after = jnp.where(t == pl.num_programs(1) - 1, 0.0, zn_ref[0, 0:1, :])
    z_prev = jnp.where(row == 0, z_before, pltpu.roll(z, 1, 0))
    z_next = jnp.where(row == tm - 1, z_after, pltpu.roll(z, tm - 1, 0))
    conv = cb_ref[...] * (z_prev * cw_ref[0:1, :] + z * cw_ref[1:2, :] + z_next * cw_ref[2:3, :]
                          + cbias_ref[...])

    n_att = att_ref.shape[1] // HEAD
    gt_a = mod_ref[0, 2:3, :]
    ffn_gain = gffn_ref[...] * (1.0 + mod_ref[0, 4:5, :])
    ffn_shift = mod_ref[0, 3:4, :]
    for r in range(0, tm, NORM_ROWS):
        rows = slice(r, r + NORM_ROWS)
        for g in range(D_MODEL // HEAD):
            cols = slice(g * HEAD, (g + 1) * HEAD)
            blk = att_ref[rows, cols] if g < n_att else conv[rows, (g - n_att) * HEAD:(g - n_att + 1) * HEAD]
            mix_ref[rows, cols] = _rms(blk, gmo_ref[:, cols]).astype(BF16)
    x1 = x_ref[...] + gt_a * _dot(mix_ref[...], wout_ref[...])
    x1_ref[...] = x1
    hf_ref[...] = (_rms(x1, ffn_gain) + ffn_shift).astype(BF16)


def _out(att2, cb, z, cw, cbias, gmo, wout, x2, mod_l, gffn, batch, seq):
    tm = TM_OUT
    tpb = seq // tm
    n8 = batch * seq // 8
    z3 = z.reshape(n8, 8, CONV_W)
    const = lambda b, t: (0, 0)
    tok = lambda b, t: (b * tpb + t, 0)
    return pl.pallas_call(
        _out_kernel,
        grid=(batch, tpb),
        in_specs=[
            pl.BlockSpec((tm, att2.shape[1]), tok),
            pl.BlockSpec((tm, CONV_W), tok),
            pl.BlockSpec((tm, CONV_W), tok),
            pl.BlockSpec((1, 8, CONV_W), lambda b, t: (jnp.maximum((b * tpb + t) * (tm // 8) - 1, 0), 0, 0)),
            pl.BlockSpec((1, 8, CONV_W), lambda b, t: (jnp.minimum((b * tpb + t + 1) * (tm // 8), n8 - 1), 0, 0)),
            pl.BlockSpec((3, CONV_W), const),
            pl.BlockSpec((1, CONV_W), const),
            pl.BlockSpec((1, D_MODEL), const),
            pl.BlockSpec(wout.shape, const, pipeline_mode=pl.Buffered(1)),
            pl.BlockSpec((tm, D_MODEL), tok),
            pl.BlockSpec((1, 6, D_MODEL), lambda b, t: (b, 0, 0)),
            pl.BlockSpec((1, D_MODEL), const),
        ],
        out_specs=[
            pl.BlockSpec((tm, D_MODEL), tok),
            pl.BlockSpec((tm, D_MODEL), tok),
        ],
        out_shape=[
            jax.ShapeDtypeStruct((batch * seq, D_MODEL), F32),
            jax.ShapeDtypeStruct((batch * seq, D_MODEL), BF16),
        ],
        scratch_shapes=[pltpu.VMEM((tm, D_MODEL), BF16)],
        compiler_params=pltpu.CompilerParams(
            dimension_semantics=("parallel", "parallel"), vmem_limit_bytes=VMEM_LIMIT),
        name="out_proj",
    )(att2, cb, z, z3, z3, cw, cbias, gmo, wout, x2, mod_l, gffn)


def _ffn_kernel(h_ref, hp_ref, hn_ref, wup_ref, cw_ref, cbias_ref, wdn_ref, x1_ref, mod_ref, gfin_ref,
                o_ref, hbuf_ref, u_ref):
    t = pl.program_id(1)
    f = pl.program_id(2)
    nf = pl.num_programs(2)
    tm = h_ref.shape[0]

    def step(first, last):
        if first:
            hbuf_ref[HALO:HALO + tm, :] = h_ref[...]
            hbuf_ref[0:HALO, :] = jnp.where(t == 0, jnp.zeros_like(hp_ref[0]), hp_ref[0])
            hbuf_ref[HALO + tm:, :] = jnp.where(t == pl.num_programs(1) - 1, jnp.zeros_like(hn_ref[0]), hn_ref[0])
        u_ref[...] = _dot(hbuf_ref[...], wup_ref[...])
        cw = cw_ref[f]
        y = (u_ref[pl.ds(HALO - 1, tm), :] * cw[0:1, :] + u_ref[pl.ds(HALO, tm), :] * cw[1:2, :]
             + u_ref[pl.ds(HALO + 1, tm), :] * cw[2:3, :] + cbias_ref[f])
        act = jnp.concatenate(
            [y[:, 2 * j * LANES:(2 * j + 1) * LANES] * _silu(y[:, (2 * j + 1) * LANES:(2 * j + 2) * LANES])
             for j in SLAB_OF_BLOCK], axis=1).astype(BF16)
        half = TF_SUB // 2
        d = _dot(act[:, :half], wdn_ref[:half, :]) + _dot(act[:, half:], wdn_ref[half:, :])
        acc = d if first else o_ref[...] + d
        if last:
            acc = _rms(x1_ref[...] + mod_ref[0, 5:6, :] * acc, gfin_ref[...])
        o_ref[...] = acc

    pl.when(f == 0)(lambda: step(True, False))
    pl.when((f > 0) & (f < nf - 1))(lambda: step(False, False))
    pl.when(f == nf - 1)(lambda: step(False, True))


def _ffn(hf, wup_t, cw_t, cbias_t, wdn, x1, mod_l, gfin, batch, seq):
    tm = TM_FFN
    tpb = seq // tm
    nf = N_SUB // FFN_SUBS
    tf = FFN_SUBS * TF_SUB
    n_h = batch * seq // HALO
    hf3 = hf.reshape(n_h, HALO, D_MODEL)
    tok = lambda b, t, f: (b * tpb + t, 0)
    return pl.pallas_call(
        _ffn_kernel,
        grid=(batch, tpb, nf),
        in_specs=[
            pl.BlockSpec((tm, D_MODEL), tok),
            pl.BlockSpec((1, HALO, D_MODEL),
                         lambda b, t, f: (jnp.maximum((b * tpb + t) * (tm // HALO) - 1, 0), 0, 0)),
            pl.BlockSpec((1, HALO, D_MODEL),
                         lambda b, t, f: (jnp.minimum((b * tpb + t + 1) * (tm // HALO), n_h - 1), 0, 0)),
            pl.BlockSpec((D_MODEL, 2 * TF_SUB), lambda b, t, f: (0, f)),
            pl.BlockSpec((nf, 3, 2 * TF_SUB), lambda b, t, f: (0, 0, 0)),
            pl.BlockSpec((nf, 1, 2 * TF_SUB), lambda b, t, f: (0, 0, 0)),
            pl.BlockSpec((tf, D_MODEL), lambda b, t, f: (f, 0)),
            pl.BlockSpec((tm, D_MODEL), tok),
            pl.BlockSpec((1, 6, D_MODEL), lambda b, t, f: (b, 0, 0)),
            pl.BlockSpec((1, D_MODEL), lambda b, t, f: (0, 0)),
        ],
        out_specs=pl.BlockSpec((tm, D_MODEL), tok),
        out_shape=jax.ShapeDtypeStruct((batch * seq, D_MODEL), F32),
        scratch_shapes=[
            pltpu.VMEM((tm + 2 * HALO, D_MODEL), BF16),
            pltpu.VMEM((tm + 2 * HALO, 2 * TF_SUB), F32),
        ],
        compiler_params=pltpu.CompilerParams(
            dimension_semantics=("parallel", "parallel", "arbitrary"), vmem_limit_bytes=VMEM_LIMIT),
        name="ffn",
    )(hf, hf3, hf3, wup_t, cw_t, cbias_t, wdn, x1, mod_l, gfin)


def _rope_tables(seq):
    rows = seq // GRID_W
    row = jnp.repeat(jnp.arange(rows), GRID_W).astype(F32)
    col = jnp.tile(jnp.arange(GRID_W), rows).astype(F32)
    axis_dim = QK_ROPE // 2
    inv = ROPE_BASE ** (-jnp.arange(0, axis_dim, 2, dtype=F32) / axis_dim)
    ang = jnp.concatenate([row[:, None] * inv, col[:, None] * inv], axis=-1)
    cos, sin = jnp.cos(ang), jnp.sin(ang)
    c = jnp.repeat(cos, 2, axis=-1)
    s = jnp.stack([-sin, sin], axis=-1).reshape(seq, QK_ROPE)
    return jnp.tile(c, (1, LANES // QK_ROPE)), jnp.tile(s, (1, LANES // QK_ROPE))


def _tile_ffn_cols(a):
    pad = ((0, 0), (0, FFN_PAD - FFN_DIM))
    val = jnp.pad(a[:, :FFN_DIM], pad).reshape(a.shape[0], N_SUB, BPS, 1, LANES)
    gate = jnp.pad(a[:, FFN_DIM:], pad).reshape(a.shape[0], N_SUB, BPS, 1, LANES)
    block_of_slab = [SLAB_OF_BLOCK.index(p) for p in range(BPS)]
    both = jnp.concatenate([val, gate], axis=3)[:, :, block_of_slab].reshape(a.shape[0], N_SUB, 2 * TF_SUB)
    return jnp.moveaxis(both, 1, 0)


def kernel(x, c, ctx, c_ctx, w_ada, b_ada, g_mix_norm, w_in, g_q_a, w_q_b, g_kv_a, w_kv_b, conv_w, conv_b, g_mix_out, w_out, g_ffn_norm, w_up, ffn_conv_w, ffn_conv_b, w_down, g_final):
    batch, seq, d = x.shape
    assert d == D_MODEL and w_ada.shape[0] == 1
    x2 = x.reshape(batch * seq, d)

    c8 = jnp.concatenate([c, c_ctx[None], jnp.zeros((8 - batch - 1, d), F32)], axis=0)
    mod, wconv, wqkv = _ada(c8, w_ada[0], b_ada[0][None], w_in[0].T)
    mod_l = mod[:batch].reshape(batch, 6, d)
    mod_c = mod[batch:batch + 1].reshape(1, 6, d)

    wqb3 = w_q_b[0].reshape(Q_RANK, N_HEADS, QK_DIM)
    wqb = jnp.concatenate([wqb3[:, :, :QK_NOPE].reshape(Q_RANK, -1),
                           wqb3[:, :, QK_NOPE:].reshape(Q_RANK, -1)], axis=-1).astype(BF16)
    wkvb = w_kv_b[0].astype(BF16)
    cw_t = _tile_ffn_cols(ffn_conv_w[0])
    cbias_t = _tile_ffn_cols(ffn_conv_b)

    cq, sq = _rope_tables(seq)

    q, k_l, v_l, cb, z = _in_lat(x2, mod_l, g_mix_norm, wqkv, wconv, g_q_a, wqb, g_kv_a, wkvb,
                                 cq, sq, batch, seq)
    k_c, v_c = _in_ctx(ctx, mod_c, g_mix_norm, wqkv, g_kv_a, wkvb)
    att, wup_t, wdn, wout = _attn(q, k_l, v_l, k_c, v_c, w_up[0], w_down[0], w_out[0])
    x1, hf = _out(att.reshape(batch * seq, N_HEADS * HEAD), cb, z, conv_w[0], conv_b, g_mix_out,
                  wout, x2, mod_l, g_ffn_norm, batch, seq)
    y = _ffn(hf, wup_t, cw_t, cbias_t, wdn, x1, mod_l, g_final[None], batch, seq)
    return y.reshape(batch, seq, d)
```

```python
import functools

import jax
import jax.numpy as jnp
import numpy as np
from jax import lax
from jax.experimental import pallas as pl
from jax.experimental.pallas import tpu as pltpu

F32 = jnp.float32
BF16 = jnp.bfloat16

D_MODEL = 2048
N_HEADS = 8
HEAD = 128
QK_NOPE = 128
QK_ROPE = 64
QK_DIM = QK_NOPE + QK_ROPE
Q_RANK = 512
KV_RANK = 256
CONV_W = 1024
GRID_W = 64
ROPE_BASE = 10000.0
EPS = 1e-6
LOG2E = 1.4426950408889634
FFN_DIM = 5504

V7X_VMEM_BYTES = 64 * 1024 * 1024
VMEM_LIMIT = V7X_VMEM_BYTES - 8 * 1024 * 1024

TM_IN = 512
TQ = 2048
TQ_SUB = 256
TM_OUT = 512
NORM_ROWS = 64
TM_FFN = 512
LANES = 128
TF_SUB = 512
FFN_SUBS = 1
HALO = 8
BPS = TF_SUB // LANES
FFN_BLKS = FFN_DIM // LANES
N_SUB = (FFN_DIM + TF_SUB - 1) // TF_SUB
FFN_PAD = N_SUB * TF_SUB
SLAB_OF_BLOCK = (0, 2, 1, 3)


def _rms(x, g):
    return x * lax.rsqrt(jnp.mean(x * x, axis=-1, keepdims=True) + EPS) * g


def _silu(x):
    h = 0.5 * x
    return h * (1.0 + jnp.tanh(h))


def _dot(a, b):
    return jnp.dot(a, b, preferred_element_type=F32)


def _rope(x, c, s):
    w = x.shape[-1]
    lane = lax.broadcasted_iota(jnp.int32, x.shape, 1)
    from_right = pltpu.roll(x, w - 1, 1)
    from_left = pltpu.roll(x, 1, 1)
    partner = jnp.where(lane % 2 == 0, from_right, from_left)
    return x * c + partner * s


N_QKV = Q_RANK + KV_RANK + QK_ROPE
PREP_ROWS = 512
N_QKV_BLKS = -(-N_QKV // PREP_ROWS)
N_CONV_BLKS = 3 * CONV_W // PREP_ROWS


def _ada_kernel(c_ref, cctx_ref, w_ref, b_ref, wi_ref, wq_ref, wkv_ref,
                o_ref, conv_ref, qkv_ref, wqb_ref, wkvb_ref, lhs_ref, prev_ref):
    j = pl.program_id(0)
    nb = c_ref.shape[0]
    lhs_ref[...] = jnp.zeros_like(lhs_ref)
    lhs_ref[0:nb, :] = _silu(c_ref[...])
    lhs_ref[nb:nb + 1, :] = _silu(cctx_ref[...])
    o_ref[...] = _dot(lhs_ref[...].astype(BF16), w_ref[...].astype(BF16)) + b_ref[...]

    @pl.when(j < N_QKV_BLKS)
    def _():
        qkv_ref[...] = wi_ref[...].T.astype(BF16)

    @pl.when((j >= N_QKV_BLKS) & (j < N_QKV_BLKS + N_CONV_BLKS))
    def _():
        off = N_QKV % PREP_ROWS
        conv_ref[...] = jnp.concatenate([prev_ref[off:, :], wi_ref[:off, :]], axis=0).T.astype(BF16)

    @pl.when(j < N_QKV_BLKS + N_CONV_BLKS - 1)
    def _():
        prev_ref[...] = wi_ref[...]

    @pl.when(j == N_QKV_BLKS + N_CONV_BLKS)
    def _():
        w = wq_ref[...]
        nope = [w[:, h * QK_DIM:h * QK_DIM + QK_NOPE] for h in range(N_HEADS)]
        rope = [w[:, h * QK_DIM + QK_NOPE:(h + 1) * QK_DIM] for h in range(N_HEADS)]
        wqb_ref[...] = jnp.concatenate(nope + rope, axis=1).astype(BF16)

    @pl.when(j == N_QKV_BLKS + N_CONV_BLKS + 1)
    def _():
        wkvb_ref[...] = wkv_ref[...].astype(BF16)


def _ada(c, c_ctx, w, b, w_in_t, w_q_b, w_kv_b):
    n = w.shape[1]
    tn = 1024
    n_blks = N_QKV_BLKS + N_CONV_BLKS
    assert n // tn >= n_blks + 2 and N_QKV_BLKS == N_QKV // PREP_ROWS + 1 and c.shape[0] + 1 <= 8
    conv_blk = lambda j: jnp.clip(j - N_QKV_BLKS, 0, N_CONV_BLKS - 1)
    const = lambda j: (0, 0)
    return pl.pallas_call(
        _ada_kernel,
        grid=(n // tn,),
        in_specs=[
            pl.BlockSpec(c.shape, const),
            pl.BlockSpec((1, D_MODEL), const),
            pl.BlockSpec((D_MODEL, tn), lambda j: (0, j)),
            pl.BlockSpec((1, tn), lambda j: (0, j)),
            pl.BlockSpec((PREP_ROWS, D_MODEL), lambda j: (jnp.minimum(j, n_blks - 1), 0)),
            pl.BlockSpec(w_q_b.shape, const),
            pl.BlockSpec(w_kv_b.shape, const),
        ],
        out_specs=[
            pl.BlockSpec((8, tn), lambda j: (0, j)),
            pl.BlockSpec((D_MODEL, PREP_ROWS), lambda j: (0, conv_blk(j))),
            pl.BlockSpec((D_MODEL, PREP_ROWS), lambda j: (0, jnp.minimum(j, N_QKV_BLKS - 1))),
            pl.BlockSpec(w_q_b.shape, const),
            pl.BlockSpec(w_kv_b.shape, const),
        ],
        out_shape=[
            jax.ShapeDtypeStruct((8, n), F32),
            jax.ShapeDtypeStruct((D_MODEL, N_CONV_BLKS * PREP_ROWS), BF16),
            jax.ShapeDtypeStruct((D_MODEL, N_QKV_BLKS * PREP_ROWS), BF16),
            jax.ShapeDtypeStruct(w_q_b.shape, BF16),
            jax.ShapeDtypeStruct(w_kv_b.shape, BF16),
        ],
        scratch_shapes=[pltpu.VMEM((8, D_MODEL), F32), pltpu.VMEM((PREP_ROWS, D_MODEL), F32)],
        compiler_params=pltpu.CompilerParams(
            dimension_semantics=("arbitrary",), vmem_limit_bytes=VMEM_LIMIT),
        name="ada",
    )(c, c_ctx, w, b, w_in_t, w_q_b, w_kv_b)


def _in_lat_kernel(x_ref, mod_ref, gmix_ref, wqkv_ref, wconv_ref, gq_ref, wqb_ref, gkv_ref,
                   wkvb_ref, cq_ref, sq_ref, q_ref, k_ref, v_ref, cb_ref, z_ref):
    sh = mod_ref[0, 0:1, :]
    sc = mod_ref[0, 1:2, :]
    h = (_rms(x_ref[...], gmix_ref[...]) * (1.0 + sc) + sh).astype(BF16)

    p2 = _dot(h, wconv_ref[...])
    cb_ref[...] = p2[:, :CONV_W]
    z_ref[...] = p2[:, CONV_W:2 * CONV_W] * p2[:, 2 * CONV_W:]

    p1 = _dot(h, wqkv_ref[...])
    qa = p1[:, :Q_RANK]
    kva = p1[:, Q_RANK:Q_RANK + KV_RANK]
    kr = p1[:, Q_RANK + KV_RANK:Q_RANK + KV_RANK + LANES]

    c1 = cq_ref[...]
    s1 = sq_ref[...]
    reps = N_HEADS * QK_ROPE // LANES
    cq = jnp.concatenate([c1] * reps, axis=1)
    sq = jnp.concatenate([s1] * reps, axis=1)
    scale = QK_DIM ** -0.5 * LOG2E
    q = _dot(_rms(qa, gq_ref[...]).astype(BF16), wqb_ref[...])
    qn = (q[:, :N_HEADS * QK_NOPE] * scale).astype(BF16)
    qr = (_rope(q[:, N_HEADS * QK_NOPE:], cq, sq) * scale).astype(BF16)

    kv = _dot(_rms(kva, gkv_ref[...]).astype(BF16), wkvb_ref[...]).astype(BF16)
    krr = _rope(kr, c1, s1)[:, :QK_ROPE].astype(BF16)
    for hd in range(N_HEADS):
        q_ref[0, hd, :, 0:QK_NOPE] = qn[:, hd * QK_NOPE:(hd + 1) * QK_NOPE]
        q_ref[0, hd, :, QK_NOPE:QK_DIM] = qr[:, hd * QK_ROPE:(hd + 1) * QK_ROPE]
        k_ref[0, hd, :, 0:QK_NOPE] = kv[:, hd * 2 * HEAD:hd * 2 * HEAD + QK_NOPE]
        k_ref[0, hd, :, QK_NOPE:QK_DIM] = krr
        v_ref[0, hd, :, :] = kv[:, hd * 2 * HEAD + QK_NOPE:(hd + 1) * 2 * HEAD]


def _in_lat(x2, mod_l, gmix, wqkv, wconv, gq, wqb, gkv, wkvb, cq, sq, batch, seq):
    tm = TM_IN
    tpb = seq // tm
    const = lambda b, t: (0, 0)
    one = pl.Buffered(1)
    tok = lambda b, t: (b * tpb + t, 0)
    hd4 = lambda b, t: (b, 0, t, 0)
    return pl.pallas_call(
        _in_lat_kernel,
        grid=(batch, tpb),
        in_specs=[
            pl.BlockSpec((tm, D_MODEL), tok),
            pl.BlockSpec((1, 6, D_MODEL), lambda b, t: (b, 0, 0)),
            pl.BlockSpec((1, D_MODEL), const),
            pl.BlockSpec(wqkv.shape, const, pipeline_mode=one),
            pl.BlockSpec(wconv.shape, const, pipeline_mode=one),
            pl.BlockSpec((1, Q_RANK), const),
            pl.BlockSpec(wqb.shape, const, pipeline_mode=one),
            pl.BlockSpec((1, KV_RANK), const),
            pl.BlockSpec(wkvb.shape, const, pipeline_mode=one),
            pl.BlockSpec((tm, LANES), lambda b, t: (t, 0)),
            pl.BlockSpec((tm, LANES), lambda b, t: (t, 0)),
        ],
        out_specs=[
            pl.BlockSpec((1, N_HEADS, tm, QK_DIM), hd4),
            pl.BlockSpec((1, N_HEADS, tm, QK_DIM), hd4),
            pl.BlockSpec((1, N_HEADS, tm, HEAD), hd4),
            pl.BlockSpec((tm, CONV_W), tok),
            pl.BlockSpec((tm, CONV_W), tok),
        ],
        out_shape=[
            jax.ShapeDtypeStruct((batch, N_HEADS, seq, QK_DIM), BF16),
            jax.ShapeDtypeStruct((batch, N_HEADS, seq, QK_DIM), BF16),
            jax.ShapeDtypeStruct((batch, N_HEADS, seq, HEAD), BF16),
            jax.ShapeDtypeStruct((batch * seq, CONV_W), F32),
            jax.ShapeDtypeStruct((batch * seq, CONV_W), F32),
        ],
        compiler_params=pltpu.CompilerParams(
            dimension_semantics=("parallel", "parallel"), vmem_limit_bytes=VMEM_LIMIT),
        name="in_lat",
    )(x2, mod_l, gmix, wqkv, wconv, gq, wqb, gkv, wkvb, cq, sq)


def _in_ctx_kernel(x_ref, mod_ref, gmix_ref, wkv_ref, gkv_ref, wkvb_ref, k_ref, v_ref):
    sh = mod_ref[0, 0:1, :]
    sc = mod_ref[0, 1:2, :]
    h = (_rms(x_ref[0], gmix_ref[...]) * (1.0 + sc) + sh).astype(BF16)
    p1 = _dot(h, wkv_ref[:, Q_RANK:Q_RANK + KV_RANK + LANES])
    kva = p1[:, :KV_RANK]
    kr = p1[:, KV_RANK:KV_RANK + QK_ROPE].astype(BF16)
    kv = _dot(_rms(kva, gkv_ref[...]).astype(BF16), wkvb_ref[...]).astype(BF16)
    for hd in range(N_HEADS):
        k_ref[0, hd, :, 0:QK_NOPE] = kv[:, hd * 2 * HEAD:hd * 2 * HEAD + QK_NOPE]
        k_ref[0, hd, :, QK_NOPE:QK_DIM] = kr
        v_ref[0, hd, :, :] = kv[:, hd * 2 * HEAD + QK_NOPE:(hd + 1) * 2 * HEAD]


def _in_ctx(ctx, mod, gmix, wkv, gkv, wkvb):
    batch, n_ctx, _ = ctx.shape
    const = lambda b: (0, 0)
    return pl.pallas_call(
        _in_ctx_kernel,
        grid=(batch,),
        in_specs=[
            pl.BlockSpec((1, n_ctx, D_MODEL), lambda b: (b, 0, 0)),
            pl.BlockSpec((1, 6, D_MODEL), lambda b: (batch, 0, 0)),
            pl.BlockSpec((1, D_MODEL), const),
            pl.BlockSpec(wkv.shape, const),
            pl.BlockSpec((1, KV_RANK), const),
            pl.BlockSpec(wkvb.shape, const),
        ],
        out_specs=[
            pl.BlockSpec((1, N_HEADS, n_ctx, QK_DIM), lambda b: (b, 0, 0, 0)),
            pl.BlockSpec((1, N_HEADS, n_ctx, HEAD), lambda b: (b, 0, 0, 0)),
        ],
        out_shape=[
            jax.ShapeDtypeStruct((batch, N_HEADS, n_ctx, QK_DIM), BF16),
            jax.ShapeDtypeStruct((batch, N_HEADS, n_ctx, HEAD), BF16),
        ],
        compiler_params=pltpu.CompilerParams(
            dimension_semantics=("parallel",), vmem_limit_bytes=VMEM_LIMIT),
        name="in_ctx",
    )(ctx, mod, gmix, wkv, gkv, wkvb)


UP_LANE_BLKS = N_SUB * 2 * BPS
DN_ROW_BLKS = N_SUB * BPS


def _up_src(c):
    c_in = jnp.minimum(c, UP_LANE_BLKS - 1)
    slab = (c_in % (2 * BPS)) // 2
    blk_in_tile = sum(jnp.where(slab == SLAB_OF_BLOCK[b], b, 0) for b in range(BPS))
    blk = (c_in // (2 * BPS)) * BPS + blk_in_tile
    src = (c_in % 2) * FFN_BLKS + jnp.minimum(blk, FFN_BLKS - 1)
    return src, (c < UP_LANE_BLKS) & (blk < FFN_BLKS)


def _attn_kernel(*refs, n_up, n_dn):
    q_ref, kl_ref, vl_ref, kc_ref, vc_ref = refs[:5]
    wu_refs = refs[5:5 + n_up]
    wd_refs = refs[5 + n_up:5 + n_up + n_dn]
    wo_ref = refs[5 + n_up + n_dn]
    o_ref, wup_ref, wdn_ref, wout_ref = refs[6 + n_up + n_dn:]

    step = ((pl.program_id(0) * pl.num_programs(1) + pl.program_id(1)) * pl.num_programs(2)
            + pl.program_id(2))
    for i in range(n_up):
        _, real = _up_src(step * n_up + i)
        wup_ref[:, i * LANES:(i + 1) * LANES] = jnp.where(real, wu_refs[i][...], 0.0).astype(BF16)
    dn_blk = jnp.minimum(step, DN_ROW_BLKS // n_dn - 1) * n_dn
    for i in range(n_dn):
        wdn_ref[i * LANES:(i + 1) * LANES, :] = jnp.where(dn_blk + i < FFN_BLKS, wd_refs[i][...], 0.0).astype(BF16)
    wout_ref[...] = wo_ref[...].astype(BF16)

    nt = (((1,), (1,)), ((), ()))

    def with_ones(v):
        return jnp.concatenate([v, jnp.ones_like(v)], axis=1)

    vl = with_ones(vl_ref[0, 0])
    vc = with_ones(vc_ref[0, 0])
    for j in range(TQ // TQ_SUB):
        rows = slice(j * TQ_SUB, (j + 1) * TQ_SUB)
        q = q_ref[0, 0, rows, :]
        s_l = lax.dot_general(q, kl_ref[0, 0], nt, preferred_element_type=F32)
        s_c = lax.dot_general(q, kc_ref[0, 0], nt, preferred_element_type=F32)
        m = jnp.maximum(jnp.max(s_l, axis=-1, keepdims=True), jnp.max(s_c, axis=-1, keepdims=True))
        p_l = jnp.exp2(s_l - m).astype(BF16)
        p_c = jnp.exp2(s_c - m).astype(BF16)
        o = _dot(p_l, vl) + _dot(p_c, vc)
        o_ref[0, rows, :] = o[:, :HEAD] / o[:, HEAD:]


def _attn(q, k_l, v_l, k_c, v_c, w_up, w_down, w_out):
    batch, heads, seq, _ = q.shape
    n_ctx = k_c.shape[2]
    nq = seq // TQ
    n_steps = batch * heads * nq
    n_up = -(-UP_LANE_BLKS // n_steps)
    n_dn = -(-DN_ROW_BLKS // n_steps)
    assert DN_ROW_BLKS % n_dn == 0 and D_MODEL % n_steps == 0
    wo_rows = D_MODEL // n_steps
    step = lambda b, h, i: (b * heads + h) * nq + i
    kv_map = lambda b, h, i: (b, h, 0, 0)
    dn_out = lambda b, h, i: jnp.minimum(step(b, h, i), DN_ROW_BLKS // n_dn - 1)
    up_specs = [pl.BlockSpec((D_MODEL, LANES),
                             functools.partial(lambda b, h, i, k: (0, _up_src(step(b, h, i) * n_up + k)[0]), k=k))
                for k in range(n_up)]
    dn_specs = [pl.BlockSpec((LANES, D_MODEL),
                             functools.partial(
                                 lambda b, h, i, k: (jnp.minimum(dn_out(b, h, i) * n_dn + k, FFN_BLKS - 1), 0), k=k))
                for k in range(n_dn)]
    return pl.pallas_call(
        functools.partial(_attn_kernel, n_up=n_up, n_dn=n_dn),
        grid=(batch, heads, nq),
        in_specs=[
            pl.BlockSpec((1, 1, TQ, QK_DIM), lambda b, h, i: (b, h, i, 0)),
            pl.BlockSpec((1, 1, seq, QK_DIM), kv_map),
            pl.BlockSpec((1, 1, seq, HEAD), kv_map),
            pl.BlockSpec((1, 1, n_ctx, QK_DIM), kv_map),
            pl.BlockSpec((1, 1, n_ctx, HEAD), kv_map),
            *up_specs,
            *dn_specs,
            pl.BlockSpec((wo_rows, D_MODEL), lambda b, h, i: (step(b, h, i), 0)),
        ],
        out_specs=[
            pl.BlockSpec((1, TQ, HEAD), lambda b, h, i: (b, i, h)),
            pl.BlockSpec((D_MODEL, n_up * LANES), lambda b, h, i: (0, step(b, h, i))),
            pl.BlockSpec((n_dn * LANES, D_MODEL), lambda b, h, i: (dn_out(b, h, i), 0)),
            pl.BlockSpec((wo_rows, D_MODEL), lambda b, h, i: (step(b, h, i), 0)),
        ],
        out_shape=[
            jax.ShapeDtypeStruct((batch, seq, heads * HEAD), F32),
            jax.ShapeDtypeStruct((D_MODEL, n_steps * n_up * LANES), BF16),
            jax.ShapeDtypeStruct((FFN_PAD, D_MODEL), BF16),
            jax.ShapeDtypeStruct((D_MODEL, D_MODEL), BF16),
        ],
        compiler_params=pltpu.CompilerParams(
            dimension_semantics=("arbitrary", "arbitrary", "arbitrary"),
            vmem_limit_bytes=VMEM_LIMIT),
        name="attn",
    )(q, k_l, v_l, k_c, v_c, *([w_up] * n_up), *([w_down] * n_dn), w_out)


def _out_kernel(att_ref, cb_ref, z_ref, zp_ref, zn_ref, cw_ref, cbias_ref, gmo_ref, wout_ref,
                x_ref, mod_ref, gffn_ref, x1_ref, hf_ref, mix_ref):
    t = pl.program_id(1)
    tm = z_ref.shape[0]
    z = z_ref[...]
    row = lax.broadcasted_iota(jnp.int32, z.shape, 0)
    z_before = jnp.where(t == 0, 0.0, zp_ref[0, 7:8, :])
    z_after = jnp.where(t == pl.num_programs(1) - 1, 0.0, zn_ref[0, 0:1, :])
    z_prev = jnp.where(row == 0, z_before, pltpu.roll(z, 1, 0))
    z_next = jnp.where(row == tm - 1, z_after, pltpu.roll(z, tm - 1, 0))
    conv = cb_ref[...] * (z_prev * cw_ref[0:1, :] + z * cw_ref[1:2, :] + z_next * cw_ref[2:3, :]
                          + cbias_ref[...])

    n_att = att_ref.shape[1] // HEAD
    gt_a = mod_ref[0, 2:3, :]
    ffn_gain = gffn_ref[...] * (1.0 + mod_ref[0, 4:5, :])
    ffn_shift = mod_ref[0, 3:4, :]
    for r in range(0, tm, NORM_ROWS):
        rows = slice(r, r + NORM_ROWS)
        for g in range(D_MODEL // HEAD):
            cols = slice(g * HEAD, (g + 1) * HEAD)
            blk = att_ref[rows, cols] if g < n_att else conv[rows, (g - n_att) * HEAD:(g - n_att + 1) * HEAD]
            mix_ref[rows, cols] = _rms(blk, gmo_ref[:, cols]).astype(BF16)
    x1 = x_ref[...] + gt_a * _dot(mix_ref[...], wout_ref[...])
    x1_ref[...] = x1
    hf_ref[...] = (_rms(x1, ffn_gain) + ffn_shift).astype(BF16)


def _out(att2, cb, z, cw, cbias, gmo, wout, x2, mod_l, gffn, batch, seq):
    tm = TM_OUT
    tpb = seq // tm
    n8 = batch * seq // 8
    z3 = z.reshape(n8, 8, CONV_W)
    const = lambda b, t: (0, 0)
    tok = lambda b, t: (b * tpb + t, 0)
    return pl.pallas_call(
        _out_kernel,
        grid=(batch, tpb),
        in_specs=[
            pl.BlockSpec((tm, att2.shape[1]), tok),
            pl.BlockSpec((tm, CONV_W), tok),
            pl.BlockSpec((tm, CONV_W), tok),
            pl.BlockSpec((1, 8, CONV_W), lambda b, t: (jnp.maximum((b * tpb + t) * (tm // 8) - 1, 0), 0, 0)),
            pl.BlockSpec((1, 8, CONV_W), lambda b, t: (jnp.minimum((b * tpb + t + 1) * (tm // 8), n8 - 1), 0, 0)),
            pl.BlockSpec((3, CONV_W), const),
            pl.BlockSpec((1, CONV_W), const),
            pl.BlockSpec((1, D_MODEL), const),
            pl.BlockSpec(wout.shape, const, pipeline_mode=pl.Buffered(1)),
            pl.BlockSpec((tm, D_MODEL), tok),
            pl.BlockSpec((1, 6, D_MODEL), lambda b, t: (b, 0, 0)),
            pl.BlockSpec((1, D_MODEL), const),
        ],
        out_specs=[
            pl.BlockSpec((tm, D_MODEL), tok),
            pl.BlockSpec((tm, D_MODEL), tok),
        ],
        out_shape=[
            jax.ShapeDtypeStruct((batch * seq, D_MODEL), F32),
            jax.ShapeDtypeStruct((batch * seq, D_MODEL), BF16),
        ],
        scratch_shapes=[pltpu.VMEM((tm, D_MODEL), BF16)],
        compiler_params=pltpu.CompilerParams(
            dimension_semantics=("parallel", "parallel"), vmem_limit_bytes=VMEM_LIMIT),
        name="out_proj",
    )(att2, cb, z, z3, z3, cw, cbias, gmo, wout, x2, mod_l, gffn)


def _ffn_kernel(h_ref, hp_ref, hn_ref, wup_ref, cw_ref, cbias_ref, wdn_ref, x1_ref, mod_ref, gfin_ref,
                o_ref, hbuf_ref, u_ref):
    t = pl.program_id(1)
    f = pl.program_id(2)
    nf = pl.num_programs(2)
    tm = h_ref.shape[0]

    def step(first, last):
        if first:
            hbuf_ref[HALO:HALO + tm, :] = h_ref[...]
            hbuf_ref[0:HALO, :] = jnp.where(t == 0, jnp.zeros_like(hp_ref[0]), hp_ref[0])
            hbuf_ref[HALO + tm:, :] = jnp.where(t == pl.num_programs(1) - 1, jnp.zeros_like(hn_ref[0]), hn_ref[0])
        u_ref[...] = _dot(hbuf_ref[...], wup_ref[...])
        cw = cw_ref[f]
        y = (u_ref[pl.ds(HALO - 1, tm), :] * cw[0:1, :] + u_ref[pl.ds(HALO, tm), :] * cw[1:2, :]
             + u_ref[pl.ds(HALO + 1, tm), :] * cw[2:3, :] + cbias_ref[f])
        act = jnp.concatenate(
            [y[:, 2 * j * LANES:(2 * j + 1) * LANES] * _silu(y[:, (2 * j + 1) * LANES:(2 * j + 2) * LANES])
             for j in SLAB_OF_BLOCK], axis=1).astype(BF16)
        half = TF_SUB // 2
        d = _dot(act[:, :half], wdn_ref[:half, :]) + _dot(act[:, half:], wdn_ref[half:, :])
        acc = d if first else o_ref[...] + d
        if last:
            acc = _rms(x1_ref[...] + mod_ref[0, 5:6, :] * acc, gfin_ref[...])
        o_ref[...] = acc

    pl.when(f == 0)(lambda: step(True, False))
    pl.when((f > 0) & (f < nf - 1))(lambda: step(False, False))
    pl.when(f == nf - 1)(lambda: step(False, True))


def _ffn(hf, wup_t, cw_t, cbias_t, wdn, x1, mod_l, gfin, batch, seq):
    tm = TM_FFN
    tpb = seq // tm
    nf = N_SUB // FFN_SUBS
    tf = FFN_SUBS * TF_SUB
    n_h = batch * seq // HALO
    hf3 = hf.reshape(n_h, HALO, D_MODEL)
    tok = lambda b, t, f: (b * tpb + t, 0)
    return pl.pallas_call(
        _ffn_kernel,
        grid=(batch, tpb, nf),
        in_specs=[
            pl.BlockSpec((tm, D_MODEL), tok),
            pl.BlockSpec((1, HALO, D_MODEL),
                         lambda b, t, f: (jnp.maximum((b * tpb + t) * (tm // HALO) - 1, 0), 0, 0)),
            pl.BlockSpec((1, HALO, D_MODEL),
                         lambda b, t, f: (jnp.minimum((b * tpb + t + 1) * (tm // HALO), n_h - 1), 0, 0)),
            pl.BlockSpec((D_MODEL, 2 * TF_SUB), lambda b, t, f: (0, f)),
            pl.BlockSpec((nf, 3, 2 * TF_SUB), lambda b, t, f: (0, 0, 0)),
            pl.BlockSpec((nf, 1, 2 * TF_SUB), lambda b, t, f: (0, 0, 0)),
            pl.BlockSpec((tf, D_MODEL), lambda b, t, f: (f, 0)),
            pl.BlockSpec((tm, D_MODEL), tok),
            pl.BlockSpec((1, 6, D_MODEL), lambda b, t, f: (b, 0, 0)),
            pl.BlockSpec((1, D_MODEL), lambda b, t, f: (0, 0)),
        ],
        out_specs=pl.BlockSpec((tm, D_MODEL), tok),
        out_shape=jax.ShapeDtypeStruct((batch * seq, D_MODEL), F32),
        scratch_shapes=[
            pltpu.VMEM((tm + 2 * HALO, D_MODEL), BF16),
            pltpu.VMEM((tm + 2 * HALO, 2 * TF_SUB), F32),
        ],
        compiler_params=pltpu.CompilerParams(
            dimension_semantics=("parallel", "parallel", "arbitrary"), vmem_limit_bytes=VMEM_LIMIT),
        name="ffn",
    )(hf, hf3, hf3, wup_t, cw_t, cbias_t, wdn, x1, mod_l, gfin)


def _rope_tables(seq):
    rows = seq // GRID_W
    row = np.repeat(np.arange(rows), GRID_W).astype(np.float64)
    col = np.tile(np.arange(GRID_W), rows).astype(np.float64)
    axis_dim = QK_ROPE // 2
    inv = ROPE_BASE ** (-np.arange(0, axis_dim, 2, dtype=np.float64) / axis_dim)
    ang = np.concatenate([row[:, None] * inv, col[:, None] * inv], axis=-1)
    cos, sin = np.cos(ang), np.sin(ang)
    c = np.repeat(cos, 2, axis=-1)
    s = np.stack([-sin, sin], axis=-1).reshape(seq, QK_ROPE)
    reps = (1, LANES // QK_ROPE)
    return jnp.asarray(np.tile(c, reps), F32), jnp.asarray(np.tile(s, reps), F32)


def _tile_ffn_cols(a):
    pad = ((0, 0), (0, FFN_PAD - FFN_DIM))
    val = jnp.pad(a[:, :FFN_DIM], pad).reshape(a.shape[0], N_SUB, BPS, 1, LANES)
    gate = jnp.pad(a[:, FFN_DIM:], pad).reshape(a.shape[0], N_SUB, BPS, 1, LANES)
    block_of_slab = [SLAB_OF_BLOCK.index(p) for p in range(BPS)]
    both = jnp.concatenate([val, gate], axis=3)[:, :, block_of_slab].reshape(a.shape[0], N_SUB, 2 * TF_SUB)
    return jnp.moveaxis(both, 1, 0)


def kernel(x, c, ctx, c_ctx, w_ada, b_ada, g_mix_norm, w_in, g_q_a, w_q_b, g_kv_a, w_kv_b, conv_w, conv_b, g_mix_out, w_out, g_ffn_norm, w_up, ffn_conv_w, ffn_conv_b, w_down, g_final):
    batch, seq, d = x.shape
    assert d == D_MODEL and w_ada.shape[0] == 1
    x2 = x.reshape(batch * seq, d)

    mod, wconv, wqkv, wqb, wkvb = _ada(c, c_ctx[None], w_ada[0], b_ada, w_in[0].T, w_q_b[0], w_kv_b[0])
    mod = mod.reshape(mod.shape[0], 6, d)

    cw_t = _tile_ffn_cols(ffn_conv_w[0])
    cbias_t = _tile_ffn_cols(ffn_conv_b)

    cq, sq = _rope_tables(seq)

    q, k_l, v_l, cb, z = _in_lat(x2, mod, g_mix_norm, wqkv, wconv, g_q_a, wqb, g_kv_a, wkvb,
                                 cq, sq, batch, seq)
    k_c, v_c = _in_ctx(ctx, mod, g_mix_norm, wqkv, g_kv_a, wkvb)
    att, wup_t, wdn, wout = _attn(q, k_l, v_l, k_c, v_c, w_up[0], w_down[0], w_out[0])
    x1, hf = _out(att.reshape(batch * seq, N_HEADS * HEAD), cb, z, conv_w[0], conv_b, g_mix_out,
                  wout, x2, mod, g_ffn_norm, batch, seq)
    y = _ffn(hf, wup_t, cw_t, cbias_t, wdn, x1, mod, g_final[None], batch, seq)
    return y.reshape(batch, seq, d)
```

```python
import functools

import jax
import jax.numpy as jnp
import numpy as np
from jax import lax
from jax.experimental import pallas as pl
from jax.experimental.pallas import tpu as pltpu

F32 = jnp.float32
BF16 = jnp.bfloat16

D_MODEL = 2048
N_HEADS = 8
HEAD = 128
QK_NOPE = 128
QK_ROPE = 64
QK_DIM = QK_NOPE + QK_ROPE
Q_RANK = 512
KV_RANK = 256
CONV_W = 1024
GRID_W = 64
ROPE_BASE = 10000.0
EPS = 1e-6
LOG2E = 1.4426950408889634
FFN_DIM = 5504

V7X_VMEM_BYTES = 64 * 1024 * 1024
VMEM_LIMIT = V7X_VMEM_BYTES - 8 * 1024 * 1024

TM_IN = 512
TQ = 2048
TQ_SUB = 256
ATTN_HEADS = 2
TM_OUT = 512
NORM_ROWS = 64
TM_FFN = 512
LANES = 128
TF_SUB = 512
FFN_SUBS = 1
HALO = 8
BPS = TF_SUB // LANES
FFN_BLKS = FFN_DIM // LANES
N_SUB = (FFN_DIM + TF_SUB - 1) // TF_SUB
FFN_PAD = N_SUB * TF_SUB
SLAB_OF_BLOCK = (0, 2, 1, 3)


def _rms(x, g):
    return x * lax.rsqrt(jnp.mean(x * x, axis=-1, keepdims=True) + EPS) * g


def _silu(x):
    h = 0.5 * x
    return h * (1.0 + jnp.tanh(h))


def _dot(a, b):
    return jnp.dot(a, b, preferred_element_type=F32)


def _rope(x, c, s):
    w = x.shape[-1]
    lane = lax.broadcasted_iota(jnp.int32, x.shape, 1)
    from_right = pltpu.roll(x, w - 1, 1)
    from_left = pltpu.roll(x, 1, 1)
    partner = jnp.where(lane % 2 == 0, from_right, from_left)
    return x * c + partner * s


N_QKV = Q_RANK + KV_RANK + QK_ROPE
PREP_ROWS = 512
N_QKV_BLKS = -(-N_QKV // PREP_ROWS)
N_CONV_BLKS = 3 * CONV_W // PREP_ROWS


def _ada_kernel(c_ref, cctx_ref, w_ref, b_ref, wi_ref, wq_ref, wkv_ref,
                o_ref, conv_ref, qkv_ref, wqb_ref, wkvb_ref, lhs_ref, prev_ref):
    j = pl.program_id(0)
    nb = c_ref.shape[0]
    lhs_ref[...] = jnp.zeros_like(lhs_ref)
    lhs_ref[0:nb, :] = _silu(c_ref[...])
    lhs_ref[nb:nb + 1, :] = _silu(cctx_ref[...])
    o_ref[...] = _dot(lhs_ref[...].astype(BF16), w_ref[...].astype(BF16)) + b_ref[...]

    @pl.when(j < N_QKV_BLKS)
    def _():
        qkv_ref[...] = wi_ref[...].T.astype(BF16)

    @pl.when((j >= N_QKV_BLKS) & (j < N_QKV_BLKS + N_CONV_BLKS))
    def _():
        off = N_QKV % PREP_ROWS
        conv_ref[...] = jnp.concatenate([prev_ref[off:, :], wi_ref[:off, :]], axis=0).T.astype(BF16)

    @pl.when(j < N_QKV_BLKS + N_CONV_BLKS - 1)
    def _():
        prev_ref[...] = wi_ref[...]

    @pl.when(j == N_QKV_BLKS + N_CONV_BLKS)
    def _():
        w = wq_ref[...]
        nope = [w[:, h * QK_DIM:h * QK_DIM + QK_NOPE] for h in range(N_HEADS)]
        rope = [w[:, h * QK_DIM + QK_NOPE:(h + 1) * QK_DIM] for h in range(N_HEADS)]
        wqb_ref[...] = jnp.concatenate(nope + rope, axis=1).astype(BF16)

    @pl.when(j == N_QKV_BLKS + N_CONV_BLKS + 1)
    def _():
        wkvb_ref[...] = wkv_ref[...].astype(BF16)


def _ada(c, c_ctx, w, b, w_in_t, w_q_b, w_kv_b):
    n = w.shape[1]
    tn = 1024
    n_blks = N_QKV_BLKS + N_CONV_BLKS
    assert n // tn >= n_blks + 2 and N_QKV_BLKS == N_QKV // PREP_ROWS + 1 and c.shape[0] + 1 <= 8
    conv_blk = lambda j: jnp.clip(j - N_QKV_BLKS, 0, N_CONV_BLKS - 1)
    const = lambda j: (0, 0)
    return pl.pallas_call(
        _ada_kernel,
        grid=(n // tn,),
        in_specs=[
            pl.BlockSpec(c.shape, const),
            pl.BlockSpec((1, D_MODEL), const),
            pl.BlockSpec((D_MODEL, tn), lambda j: (0, j)),
            pl.BlockSpec((1, tn), lambda j: (0, j)),
            pl.BlockSpec((PREP_ROWS, D_MODEL), lambda j: (jnp.minimum(j, n_blks - 1), 0)),
            pl.BlockSpec(w_q_b.shape, const),
            pl.BlockSpec(w_kv_b.shape, const),
        ],
        out_specs=[
            pl.BlockSpec((8, tn), lambda j: (0, j)),
            pl.BlockSpec((D_MODEL, PREP_ROWS), lambda j: (0, conv_blk(j))),
            pl.BlockSpec((D_MODEL, PREP_ROWS), lambda j: (0, jnp.minimum(j, N_QKV_BLKS - 1))),
            pl.BlockSpec(w_q_b.shape, const),
            pl.BlockSpec(w_kv_b.shape, const),
        ],
        out_shape=[
            jax.ShapeDtypeStruct((8, n), F32),
            jax.ShapeDtypeStruct((D_MODEL, N_CONV_BLKS * PREP_ROWS), BF16),
            jax.ShapeDtypeStruct((D_MODEL, N_QKV_BLKS * PREP_ROWS), BF16),
            jax.ShapeDtypeStruct(w_q_b.shape, BF16),
            jax.ShapeDtypeStruct(w_kv_b.shape, BF16),
        ],
        scratch_shapes=[pltpu.VMEM((8, D_MODEL), F32), pltpu.VMEM((PREP_ROWS, D_MODEL), F32)],
        compiler_params=pltpu.CompilerParams(
            dimension_semantics=("arbitrary",), vmem_limit_bytes=VMEM_LIMIT),
        name="ada",
    )(c, c_ctx, w, b, w_in_t, w_q_b, w_kv_b)


def _in_lat_kernel(x_ref, mod_ref, gmix_ref, wqkv_ref, wconv_ref, gq_ref, wqb_ref, gkv_ref,
                   wkvb_ref, cq_ref, sq_ref, q_ref, k_ref, v_ref, cb_ref, z_ref):
    sh = mod_ref[0, 0:1, :]
    sc = mod_ref[0, 1:2, :]
    h = (_rms(x_ref[...], gmix_ref[...]) * (1.0 + sc) + sh).astype(BF16)

    p2 = _dot(h, wconv_ref[...])
    cb_ref[...] = p2[:, :CONV_W]
    z_ref[...] = p2[:, CONV_W:2 * CONV_W] * p2[:, 2 * CONV_W:]

    p1 = _dot(h, wqkv_ref[...])
    qa = p1[:, :Q_RANK]
    kva = p1[:, Q_RANK:Q_RANK + KV_RANK]
    kr = p1[:, Q_RANK + KV_RANK:Q_RANK + KV_RANK + LANES]

    c1 = cq_ref[...]
    s1 = sq_ref[...]
    reps = N_HEADS * QK_ROPE // LANES
    cq = jnp.concatenate([c1] * reps, axis=1)
    sq = jnp.concatenate([s1] * reps, axis=1)
    scale = QK_DIM ** -0.5 * LOG2E
    q = _dot(_rms(qa, gq_ref[...]).astype(BF16), wqb_ref[...])
    qn = (q[:, :N_HEADS * QK_NOPE] * scale).astype(BF16)
    qr = (_rope(q[:, N_HEADS * QK_NOPE:], cq, sq) * scale).astype(BF16)

    kv = _dot(_rms(kva, gkv_ref[...]).astype(BF16), wkvb_ref[...]).astype(BF16)
    krr = _rope(kr, c1, s1)[:, :QK_ROPE].astype(BF16)
    for hd in range(N_HEADS):
        q_ref[0, hd, :, 0:QK_NOPE] = qn[:, hd * QK_NOPE:(hd + 1) * QK_NOPE]
        q_ref[0, hd, :, QK_NOPE:QK_DIM] = qr[:, hd * QK_ROPE:(hd + 1) * QK_ROPE]
        k_ref[0, hd, :, 0:QK_NOPE] = kv[:, hd * 2 * HEAD:hd * 2 * HEAD + QK_NOPE]
        k_ref[0, hd, :, QK_NOPE:QK_DIM] = krr
        v_ref[0, hd, :, :] = kv[:, hd * 2 * HEAD + QK_NOPE:(hd + 1) * 2 * HEAD]


def _in_lat(x2, mod_l, gmix, wqkv, wconv, gq, wqb, gkv, wkvb, cq, sq, batch, seq):
    tm = TM_IN
    tpb = seq // tm
    const = lambda b, t: (0, 0)
    one = pl.Buffered(1)
    tok = lambda b, t: (b * tpb + t, 0)
    hd4 = lambda b, t: (b, 0, t, 0)
    return pl.pallas_call(
        _in_lat_kernel,
        grid=(batch, tpb),
        in_specs=[
            pl.BlockSpec((tm, D_MODEL), tok),
            pl.BlockSpec((1, 6, D_MODEL), lambda b, t: (b, 0, 0)),
            pl.BlockSpec((1, D_MODEL), const),
            pl.BlockSpec(wqkv.shape, const, pipeline_mode=one),
            pl.BlockSpec(wconv.shape, const, pipeline_mode=one),
            pl.BlockSpec((1, Q_RANK), const),
            pl.BlockSpec(wqb.shape, const, pipeline_mode=one),
            pl.BlockSpec((1, KV_RANK), const),
            pl.BlockSpec(wkvb.shape, const, pipeline_mode=one),
            pl.BlockSpec((tm, LANES), lambda b, t: (t, 0)),
            pl.BlockSpec((tm, LANES), lambda b, t: (t, 0)),
        ],
        out_specs=[
            pl.BlockSpec((1, N_HEADS, tm, QK_DIM), hd4),
            pl.BlockSpec((1, N_HEADS, tm, QK_DIM), hd4),
            pl.BlockSpec((1, N_HEADS, tm, HEAD), hd4),
            pl.BlockSpec((tm, CONV_W), tok),
            pl.BlockSpec((tm, CONV_W), tok),
        ],
        out_shape=[
            jax.ShapeDtypeStruct((batch, N_HEADS, seq, QK_DIM), BF16),
            jax.ShapeDtypeStruct((batch, N_HEADS, seq, QK_DIM), BF16),
            jax.ShapeDtypeStruct((batch, N_HEADS, seq, HEAD), BF16),
            jax.ShapeDtypeStruct((batch * seq, CONV_W), F32),
            jax.ShapeDtypeStruct((batch * seq, CONV_W), F32),
        ],
        compiler_params=pltpu.CompilerParams(
            dimension_semantics=("parallel", "parallel"), vmem_limit_bytes=VMEM_LIMIT),
        name="in_lat",
    )(x2, mod_l, gmix, wqkv, wconv, gq, wqb, gkv, wkvb, cq, sq)


def _in_ctx_kernel(x_ref, mod_ref, gmix_ref, wkv_ref, gkv_ref, wkvb_ref, k_ref, v_ref):
    sh = mod_ref[0, 0:1, :]
    sc = mod_ref[0, 1:2, :]
    h = (_rms(x_ref[0], gmix_ref[...]) * (1.0 + sc) + sh).astype(BF16)
    p1 = _dot(h, wkv_ref[:, Q_RANK:Q_RANK + KV_RANK + LANES])
    kva = p1[:, :KV_RANK]
    kr = p1[:, KV_RANK:KV_RANK + QK_ROPE].astype(BF16)
    kv = _dot(_rms(kva, gkv_ref[...]).astype(BF16), wkvb_ref[...]).astype(BF16)
    for hd in range(N_HEADS):
        k_ref[0, hd, :, 0:QK_NOPE] = kv[:, hd * 2 * HEAD:hd * 2 * HEAD + QK_NOPE]
        k_ref[0, hd, :, QK_NOPE:QK_DIM] = kr
        v_ref[0, hd, :, :] = kv[:, hd * 2 * HEAD + QK_NOPE:(hd + 1) * 2 * HEAD]


def _in_ctx(ctx, mod, gmix, wkv, gkv, wkvb):
    batch, n_ctx, _ = ctx.shape
    const = lambda b: (0, 0)
    return pl.pallas_call(
        _in_ctx_kernel,
        grid=(batch,),
        in_specs=[
            pl.BlockSpec((1, n_ctx, D_MODEL), lambda b: (b, 0, 0)),
            pl.BlockSpec((1, 6, D_MODEL), lambda b: (batch, 0, 0)),
            pl.BlockSpec((1, D_MODEL), const),
            pl.BlockSpec(wkv.shape, const),
            pl.BlockSpec((1, KV_RANK), const),
            pl.BlockSpec(wkvb.shape, const),
        ],
        out_specs=[
            pl.BlockSpec((1, N_HEADS, n_ctx, QK_DIM), lambda b: (b, 0, 0, 0)),
            pl.BlockSpec((1, N_HEADS, n_ctx, HEAD), lambda b: (b, 0, 0, 0)),
        ],
        out_shape=[
            jax.ShapeDtypeStruct((batch, N_HEADS, n_ctx, QK_DIM), BF16),
            jax.ShapeDtypeStruct((batch, N_HEADS, n_ctx, HEAD), BF16),
        ],
        compiler_params=pltpu.CompilerParams(
            dimension_semantics=("parallel",), vmem_limit_bytes=VMEM_LIMIT),
        name="in_ctx",
    )(ctx, mod, gmix, wkv, gkv, wkvb)


UP_LANE_BLKS = N_SUB * 2 * BPS
DN_ROW_BLKS = N_SUB * BPS


def _up_src(c):
    c_in = jnp.minimum(c, UP_LANE_BLKS - 1)
    slab = (c_in % (2 * BPS)) // 2
    blk_in_tile = sum(jnp.where(slab == SLAB_OF_BLOCK[b], b, 0) for b in range(BPS))
    blk = (c_in // (2 * BPS)) * BPS + blk_in_tile
    src = (c_in % 2) * FFN_BLKS + jnp.minimum(blk, FFN_BLKS - 1)
    return src, (c < UP_LANE_BLKS) & (blk < FFN_BLKS)


def _attn_kernel(*refs, n_up, n_dn):
    q_ref, kl_ref, vl_ref, kc_ref, vc_ref = refs[:5]
    wu_refs = refs[5:5 + n_up]
    wd_refs = refs[5 + n_up:5 + n_up + n_dn]
    wo_ref = refs[5 + n_up + n_dn]
    o_ref, wup_ref, wdn_ref, wout_ref = refs[6 + n_up + n_dn:]

    step = ((pl.program_id(0) * pl.num_programs(1) + pl.program_id(1)) * pl.num_programs(2)
            + pl.program_id(2))
    for i in range(n_up):
        _, real = _up_src(step * n_up + i)
        wup_ref[:, i * LANES:(i + 1) * LANES] = jnp.where(real, wu_refs[i][...], 0.0).astype(BF16)
    dn_blk = jnp.minimum(step, DN_ROW_BLKS // n_dn - 1) * n_dn
    for i in range(n_dn):
        wdn_ref[i * LANES:(i + 1) * LANES, :] = jnp.where(dn_blk + i < FFN_BLKS, wd_refs[i][...], 0.0).astype(BF16)
    wout_ref[...] = wo_ref[...].astype(BF16)

    nt = (((1,), (1,)), ((), ()))

    def with_ones(v):
        return jnp.concatenate([v, jnp.ones_like(v)], axis=1)

    for hd in range(ATTN_HEADS):
        vl = with_ones(vl_ref[0, hd])
        vc = with_ones(vc_ref[0, hd])
        for j in range(TQ // TQ_SUB):
            rows = slice(j * TQ_SUB, (j + 1) * TQ_SUB)
            q = q_ref[0, hd, rows, :]
            s_l = lax.dot_general(q, kl_ref[0, hd], nt, preferred_element_type=F32)
            s_c = lax.dot_general(q, kc_ref[0, hd], nt, preferred_element_type=F32)
            m = jnp.maximum(jnp.max(s_l, axis=-1, keepdims=True), jnp.max(s_c, axis=-1, keepdims=True))
            p_l = jnp.exp2(s_l - m).astype(BF16)
            p_c = jnp.exp2(s_c - m).astype(BF16)
            o = _dot(p_l, vl) + _dot(p_c, vc)
            o_ref[0, rows, hd * HEAD:(hd + 1) * HEAD] = o[:, :HEAD] / o[:, HEAD:]


def _attn(q, k_l, v_l, k_c, v_c, w_up, w_down, w_out):
    batch, heads, seq, _ = q.shape
    n_ctx = k_c.shape[2]
    nq = seq // TQ
    hps = ATTN_HEADS
    n_hg = heads // hps
    n_steps = batch * n_hg * nq
    n_up = -(-UP_LANE_BLKS // n_steps)
    n_dn = next(k for k in range(-(-DN_ROW_BLKS // n_steps), DN_ROW_BLKS + 1) if DN_ROW_BLKS % k == 0)
    assert D_MODEL % n_steps == 0 and heads % hps == 0
    wo_rows = D_MODEL // n_steps
    step = lambda b, h, i: (b * n_hg + h) * nq + i
    kv_map = lambda b, h, i: (b, h, 0, 0)
    dn_out = lambda b, h, i: jnp.minimum(step(b, h, i), DN_ROW_BLKS // n_dn - 1)
    up_specs = [pl.BlockSpec((D_MODEL, LANES),
                             functools.partial(lambda b, h, i, k: (0, _up_src(step(b, h, i) * n_up + k)[0]), k=k))
                for k in range(n_up)]
    dn_specs = [pl.BlockSpec((LANES, D_MODEL),
                             functools.partial(
                                 lambda b, h, i, k: (jnp.minimum(dn_out(b, h, i) * n_dn + k, FFN_BLKS - 1), 0), k=k))
                for k in range(n_dn)]
    return pl.pallas_call(
        functools.partial(_attn_kernel, n_up=n_up, n_dn=n_dn),
        grid=(batch, n_hg, nq),
        in_specs=[
            pl.BlockSpec((1, hps, TQ, QK_DIM), lambda b, h, i: (b, h, i, 0)),
            pl.BlockSpec((1, hps, seq, QK_DIM), kv_map),
            pl.BlockSpec((1, hps, seq, HEAD), kv_map),
            pl.BlockSpec((1, hps, n_ctx, QK_DIM), kv_map),
            pl.BlockSpec((1, hps, n_ctx, HEAD), kv_map),
            *up_specs,
            *dn_specs,
            pl.BlockSpec((wo_rows, D_MODEL), lambda b, h, i: (step(b, h, i), 0)),
        ],
        out_specs=[
            pl.BlockSpec((1, TQ, hps * HEAD), lambda b, h, i: (b, i, h)),
            pl.BlockSpec((D_MODEL, n_up * LANES), lambda b, h, i: (0, step(b, h, i))),
            pl.BlockSpec((n_dn * LANES, D_MODEL), lambda b, h, i: (dn_out(b, h, i), 0)),
            pl.BlockSpec((wo_rows, D_MODEL), lambda b, h, i: (step(b, h, i), 0)),
        ],
        out_shape=[
            jax.ShapeDtypeStruct((batch, seq, heads * HEAD), F32),
            jax.ShapeDtypeStruct((D_MODEL, n_steps * n_up * LANES), BF16),
            jax.ShapeDtypeStruct((FFN_PAD, D_MODEL), BF16),
            jax.ShapeDtypeStruct((D_MODEL, D_MODEL), BF16),
        ],
        compiler_params=pltpu.CompilerParams(
            dimension_semantics=("arbitrary", "arbitrary", "arbitrary"),
            vmem_limit_bytes=VMEM_LIMIT),
        name="attn",
    )(q, k_l, v_l, k_c, v_c, *([w_up] * n_up), *([w_down] * n_dn), w_out)


def _out_kernel(att_ref, cb_ref, z_ref, zp_ref, zn_ref, cw_ref, cbias_ref, gmo_ref, wout_ref,
                x_ref, mod_ref, gffn_ref, x1_ref, hf_ref, mix_ref):
    t = pl.program_id(1)
    tm = z_ref.shape[0]
    z = z_ref[...]
    row = lax.broadcasted_iota(jnp.int32, z.shape, 0)
    z_before = jnp.where(t == 0, 0.0, zp_ref[0, 7:8, :])
    z_after = jnp.where(t == pl.num_programs(1) - 1, 0.0, zn_ref[0, 0:1, :])
    z_prev = jnp.where(row == 0, z_before, pltpu.roll(z, 1, 0))
    z_next = jnp.where(row == tm - 1, z_after, pltpu.roll(z, tm - 1, 0))
    conv = cb_ref[...] * (z_prev * cw_ref[0:1, :] + z * cw_ref[1:2, :] + z_next * cw_ref[2:3, :]
                          + cbias_ref[...])

    n_att = att_ref.shape[1] // HEAD
    gt_a = mod_ref[0, 2:3, :]
    ffn_gain = gffn_ref[...] * (1.0 + mod_ref[0, 4:5, :])
    ffn_shift = mod_ref[0, 3:4, :]
    for r in range(0, tm, NORM_ROWS):
        rows = slice(r, r + NORM_ROWS)
        for g in range(D_MODEL // HEAD):
            cols = slice(g * HEAD, (g + 1) * HEAD)
            blk = att_ref[rows, cols] if g < n_att else conv[rows, (g - n_att) * HEAD:(g - n_att + 1) * HEAD]
            mix_ref[rows, cols] = _rms(blk, gmo_ref[:, cols]).astype(BF16)
    x1 = x_ref[...] + gt_a * _dot(mix_ref[...], wout_ref[...])
    x1_ref[...] = x1
    hf_ref[...] = (_rms(x1, ffn_gain) + ffn_shift).astype(BF16)


def _out(att2, cb, z, cw, cbias, gmo, wout, x2, mod_l, gffn, batch, seq):
    tm = TM_OUT
    tpb = seq // tm
    n8 = batch * seq // 8
    z3 = z.reshape(n8, 8, CONV_W)
    const = lambda b, t: (0, 0)
    tok = lambda b, t: (b * tpb + t, 0)
    return pl.pallas_call(
        _out_kernel,
        grid=(batch, tpb),
        in_specs=[
            pl.BlockSpec((tm, att2.shape[1]), tok),
            pl.BlockSpec((tm, CONV_W), tok),
            pl.BlockSpec((tm, CONV_W), tok),
            pl.BlockSpec((1, 8, CONV_W), lambda b, t: (jnp.maximum((b * tpb + t) * (tm // 8) - 1, 0), 0, 0)),
            pl.BlockSpec((1, 8, CONV_W), lambda b, t: (jnp.minimum((b * tpb + t + 1) * (tm // 8), n8 - 1), 0, 0)),
            pl.BlockSpec((3, CONV_W), const),
            pl.BlockSpec((1, CONV_W), const),
            pl.BlockSpec((1, D_MODEL), const),
            pl.BlockSpec(wout.shape, const, pipeline_mode=pl.Buffered(1)),
            pl.BlockSpec((tm, D_MODEL), tok),
            pl.BlockSpec((1, 6, D_MODEL), lambda b, t: (b, 0, 0)),
            pl.BlockSpec((1, D_MODEL), const),
        ],
        out_specs=[
            pl.BlockSpec((tm, D_MODEL), tok),
            pl.BlockSpec((tm, D_MODEL), tok),
        ],
        out_shape=[
            jax.ShapeDtypeStruct((batch * seq, D_MODEL), F32),
            jax.ShapeDtypeStruct((batch * seq, D_MODEL), BF16),
        ],
        scratch_shapes=[pltpu.VMEM((tm, D_MODEL), BF16)],
        compiler_params=pltpu.CompilerParams(
            dimension_semantics=("parallel", "parallel"), vmem_limit_bytes=VMEM_LIMIT),
        name="out_proj",
    )(att2, cb, z, z3, z3, cw, cbias, gmo, wout, x2, mod_l, gffn)


def _ffn_kernel(h_ref, hp_ref, hn_ref, wup_ref, cw_ref, cbias_ref, wdn_ref, x1_ref, mod_ref, gfin_ref,
                o_ref, hbuf_ref, u_ref):
    t = pl.program_id(1)
    f = pl.program_id(2)
    nf = pl.num_programs(2)
    tm = h_ref.shape[0]

    def step(first, last):
        if first:
            hbuf_ref[HALO:HALO + tm, :] = h_ref[...]
            hbuf_ref[0:HALO, :] = jnp.where(t == 0, jnp.zeros_like(hp_ref[0]), hp_ref[0])
            hbuf_ref[HALO + tm:, :] = jnp.where(t == pl.num_programs(1) - 1, jnp.zeros_like(hn_ref[0]), hn_ref[0])
        u_ref[...] = _dot(hbuf_ref[...], wup_ref[...])
        cw = cw_ref[f]
        y = (u_ref[pl.ds(HALO - 1, tm), :] * cw[0:1, :] + u_ref[pl.ds(HALO, tm), :] * cw[1:2, :]
             + u_ref[pl.ds(HALO + 1, tm), :] * cw[2:3, :] + cbias_ref[f])
        act = jnp.concatenate(
            [y[:, 2 * j * LANES:(2 * j + 1) * LANES] * _silu(y[:, (2 * j + 1) * LANES:(2 * j + 2) * LANES])
             for j in SLAB_OF_BLOCK], axis=1).astype(BF16)
        half = TF_SUB // 2
        d = _dot(act[:, :half], wdn_ref[:half, :]) + _dot(act[:, half:], wdn_ref[half:, :])
        acc = d if first else o_ref[...] + d
        if last:
            acc = _rms(x1_ref[...] + mod_ref[0, 5:6, :] * acc, gfin_ref[...])
        o_ref[...] = acc

    pl.when(f == 0)(lambda: step(True, False))
    pl.when((f > 0) & (f < nf - 1))(lambda: step(False, False))
    pl.when(f == nf - 1)(lambda: step(False, True))


def _ffn(hf, wup_t, cw_t, cbias_t, wdn, x1, mod_l, gfin, batch, seq):
    tm = TM_FFN
    tpb = seq // tm
    nf = N_SUB // FFN_SUBS
    tf = FFN_SUBS * TF_SUB
    n_h = batch * seq // HALO
    hf3 = hf.reshape(n_h, HALO, D_MODEL)
    tok = lambda b, t, f: (b * tpb + t, 0)
    return pl.pallas_call(
        _ffn_kernel,
        grid=(batch, tpb, nf),
        in_specs=[
            pl.BlockSpec((tm, D_MODEL), tok),
            pl.BlockSpec((1, HALO, D_MODEL),
                         lambda b, t, f: (jnp.maximum((b * tpb + t) * (tm // HALO) - 1, 0), 0, 0)),
            pl.BlockSpec((1, HALO, D_MODEL),
                         lambda b, t, f: (jnp.minimum((b * tpb + t + 1) * (tm // HALO), n_h - 1), 0, 0)),
            pl.BlockSpec((D_MODEL, 2 * TF_SUB), lambda b, t, f: (0, f)),
            pl.BlockSpec((nf, 3, 2 * TF_SUB), lambda b, t, f: (0, 0, 0)),
            pl.BlockSpec((nf, 1, 2 * TF_SUB), lambda b, t, f: (0, 0, 0)),
            pl.BlockSpec((tf, D_MODEL), lambda b, t, f: (f, 0)),
            pl.BlockSpec((tm, D_MODEL), tok),
            pl.BlockSpec((1, 6, D_MODEL), lambda b, t, f: (b, 0, 0)),
            pl.BlockSpec((1, D_MODEL), lambda b, t, f: (0, 0)),
        ],
        out_specs=pl.BlockSpec((tm, D_MODEL), tok),
        out_shape=jax.ShapeDtypeStruct((batch * seq, D_MODEL), F32),
        scratch_shapes=[
            pltpu.VMEM((tm + 2 * HALO, D_MODEL), BF16),
            pltpu.VMEM((tm + 2 * HALO, 2 * TF_SUB), F32),
        ],
        compiler_params=pltpu.CompilerParams(
            dimension_semantics=("parallel", "parallel", "arbitrary"), vmem_limit_bytes=VMEM_LIMIT),
        name="ffn",
    )(hf, hf3, hf3, wup_t, cw_t, cbias_t, wdn, x1, mod_l, gfin)


def _rope_tables(seq):
    rows = seq // GRID_W
    row = np.repeat(np.arange(rows), GRID_W).astype(np.float64)
    col = np.tile(np.arange(GRID_W), rows).astype(np.float64)
    axis_dim = QK_ROPE // 2
    inv = ROPE_BASE ** (-np.arange(0, axis_dim, 2, dtype=np.float64) / axis_dim)
    ang = np.concatenate([row[:, None] * inv, col[:, None] * inv], axis=-1)
    cos, sin = np.cos(ang), np.sin(ang)
    c = np.repeat(cos, 2, axis=-1)
    s = np.stack([-sin, sin], axis=-1).reshape(seq, QK_ROPE)
    reps = (1, LANES // QK_ROPE)
    return jnp.asarray(np.tile(c, reps), F32), jnp.asarray(np.tile(s, reps), F32)


def _tile_ffn_cols(a):
    pad = ((0, 0), (0, FFN_PAD - FFN_DIM))
    val = jnp.pad(a[:, :FFN_DIM], pad).reshape(a.shape[0], N_SUB, BPS, 1, LANES)
    gate = jnp.pad(a[:, FFN_DIM:], pad).reshape(a.shape[0], N_SUB, BPS, 1, LANES)
    block_of_slab = [SLAB_OF_BLOCK.index(p) for p in range(BPS)]
    both = jnp.concatenate([val, gate], axis=3)[:, :, block_of_slab].reshape(a.shape[0], N_SUB, 2 * TF_SUB)
    return jnp.moveaxis(both, 1, 0)


def kernel(x, c, ctx, c_ctx, w_ada, b_ada, g_mix_norm, w_in, g_q_a, w_q_b, g_kv_a, w_kv_b, conv_w, conv_b, g_mix_out, w_out, g_ffn_norm, w_up, ffn_conv_w, ffn_conv_b, w_down, g_final):
    batch, seq, d = x.shape
    assert d == D_MODEL and w_ada.shape[0] == 1
    x2 = x.reshape(batch * seq, d)

    mod, wconv, wqkv, wqb, wkvb = _ada(c, c_ctx[None], w_ada[0], b_ada, w_in[0].T, w_q_b[0], w_kv_b[0])
    mod = mod.reshape(mod.shape[0], 6, d)

    cw_t = _tile_ffn_cols(ffn_conv_w[0])
    cbias_t = _tile_ffn_cols(ffn_conv_b)

    cq, sq = _rope_tables(seq)

    q, k_l, v_l, cb, z = _in_lat(x2, mod, g_mix_norm, wqkv, wconv, g_q_a, wqb, g_kv_a, wkvb,
                                 cq, sq, batch, seq)
    k_c, v_c = _in_ctx(ctx, mod, g_mix_norm, wqkv, g_kv_a, wkvb)
    att, wup_t, wdn, wout = _attn(q, k_l, v_l, k_c, v_c, w_up[0], w_down[0], w_out[0])
    x1, hf = _out(att.reshape(batch * seq, N_HEADS * HEAD), cb, z, conv_w[0], conv_b, g_mix_out,
                  wout, x2, mod, g_ffn_norm, batch, seq)
    y = _ffn(hf, wup_t, cw_t, cbias_t, wdn, x1, mod, g_final[None], batch, seq)
    return y.reshape(batch, seq, d)
```

```python
import functools

import jax
import jax.numpy as jnp
import numpy as np
from jax import lax
from jax.experimental import pallas as pl
from jax.experimental.pallas import tpu as pltpu

F32 = jnp.float32
BF16 = jnp.bfloat16

D_MODEL = 2048
N_HEADS = 8
HEAD = 128
QK_NOPE = 128
QK_ROPE = 64
QK_DIM = QK_NOPE + QK_ROPE
Q_RANK = 512
KV_RANK = 256
CONV_W = 1024
GRID_W = 64
ROPE_BASE = 10000.0
EPS = 1e-6
LOG2E = 1.4426950408889634
FFN_DIM = 5504

V7X_VMEM_BYTES = 64 * 1024 * 1024
VMEM_LIMIT = V7X_VMEM_BYTES - 8 * 1024 * 1024

TM_IN = 512
TQ = 2048
TQ_SUB = 256
ATTN_HEADS = 2
TM_OUT = 512
NORM_ROWS = 64
TM_FFN = 512
LANES = 128
SUBLANES = 8
TF_SUB = 512
HALO = SUBLANES
BPS = TF_SUB // LANES
FFN_BLKS = FFN_DIM // LANES
N_SUB = (FFN_DIM + TF_SUB - 1) // TF_SUB
FFN_PAD = N_SUB * TF_SUB
SLAB_OF_BLOCK = (0, 2, 1, 3)


def _rms(x, g):
    return x * lax.rsqrt(jnp.mean(x * x, axis=-1, keepdims=True) + EPS) * g


def _silu(x):
    h = 0.5 * x
    return h * (1.0 + jnp.tanh(h))


def _dot(a, b):
    return jnp.dot(a, b, preferred_element_type=F32)


def _rope(x, c, s):
    w = x.shape[-1]
    lane = lax.broadcasted_iota(jnp.int32, x.shape, 1)
    from_right = pltpu.roll(x, w - 1, 1)
    from_left = pltpu.roll(x, 1, 1)
    partner = jnp.where(lane % 2 == 0, from_right, from_left)
    return x * c + partner * s


N_QKV = Q_RANK + KV_RANK + QK_ROPE
PREP_ROWS = 512
N_QKV_BLKS = -(-N_QKV // PREP_ROWS)
N_CONV_BLKS = 3 * CONV_W // PREP_ROWS


def _ada_kernel(c_ref, cctx_ref, w_ref, b_ref, wi_ref, wq_ref, wkv_ref,
                o_ref, conv_ref, qkv_ref, wqb_ref, wkvb_ref, lhs_ref, prev_ref):
    j = pl.program_id(0)
    nb = c_ref.shape[0]
    lhs_ref[...] = jnp.zeros_like(lhs_ref)
    lhs_ref[0:nb, :] = _silu(c_ref[...])
    lhs_ref[nb:nb + 1, :] = _silu(cctx_ref[...])
    o_ref[...] = _dot(lhs_ref[...].astype(BF16), w_ref[...].astype(BF16)) + b_ref[...]

    @pl.when(j < N_QKV_BLKS)
    def _():
        qkv_ref[...] = wi_ref[...].T.astype(BF16)

    @pl.when((j >= N_QKV_BLKS) & (j < N_QKV_BLKS + N_CONV_BLKS))
    def _():
        off = N_QKV % PREP_ROWS
        conv_ref[...] = jnp.concatenate([prev_ref[off:, :], wi_ref[:off, :]], axis=0).T.astype(BF16)

    @pl.when(j < N_QKV_BLKS + N_CONV_BLKS - 1)
    def _():
        prev_ref[...] = wi_ref[...]

    @pl.when(j == N_QKV_BLKS + N_CONV_BLKS)
    def _():
        w = wq_ref[...]
        nope = [w[:, h * QK_DIM:h * QK_DIM + QK_NOPE] for h in range(N_HEADS)]
        rope = [w[:, h * QK_DIM + QK_NOPE:(h + 1) * QK_DIM] for h in range(N_HEADS)]
        wqb_ref[...] = jnp.concatenate(nope + rope, axis=1).astype(BF16)

    @pl.when(j == N_QKV_BLKS + N_CONV_BLKS + 1)
    def _():
        wkvb_ref[...] = wkv_ref[...].astype(BF16)


def _ada(c, c_ctx, w, b, w_in_t, w_q_b, w_kv_b):
    n = w.shape[1]
    tn = 1024
    n_blks = N_QKV_BLKS + N_CONV_BLKS
    assert n // tn >= n_blks + 2 and N_QKV_BLKS == N_QKV // PREP_ROWS + 1 and c.shape[0] + 1 <= SUBLANES
    conv_blk = lambda j: jnp.clip(j - N_QKV_BLKS, 0, N_CONV_BLKS - 1)
    const = lambda j: (0, 0)
    return pl.pallas_call(
        _ada_kernel,
        grid=(n // tn,),
        in_specs=[
            pl.BlockSpec(c.shape, const),
            pl.BlockSpec((1, D_MODEL), const),
            pl.BlockSpec((D_MODEL, tn), lambda j: (0, j)),
            pl.BlockSpec((1, tn), lambda j: (0, j)),
            pl.BlockSpec((PREP_ROWS, D_MODEL), lambda j: (jnp.minimum(j, n_blks - 1), 0)),
            pl.BlockSpec(w_q_b.shape, const),
            pl.BlockSpec(w_kv_b.shape, const),
        ],
        out_specs=[
            pl.BlockSpec((SUBLANES, tn), lambda j: (0, j)),
            pl.BlockSpec((D_MODEL, PREP_ROWS), lambda j: (0, conv_blk(j))),
            pl.BlockSpec((D_MODEL, PREP_ROWS), lambda j: (0, jnp.minimum(j, N_QKV_BLKS - 1))),
            pl.BlockSpec(w_q_b.shape, const),
            pl.BlockSpec(w_kv_b.shape, const),
        ],
        out_shape=[
            jax.ShapeDtypeStruct((SUBLANES, n), F32),
            jax.ShapeDtypeStruct((D_MODEL, N_CONV_BLKS * PREP_ROWS), BF16),
            jax.ShapeDtypeStruct((D_MODEL, N_QKV_BLKS * PREP_ROWS), BF16),
            jax.ShapeDtypeStruct(w_q_b.shape, BF16),
            jax.ShapeDtypeStruct(w_kv_b.shape, BF16),
        ],
        scratch_shapes=[pltpu.VMEM((SUBLANES, D_MODEL), F32), pltpu.VMEM((PREP_ROWS, D_MODEL), F32)],
        compiler_params=pltpu.CompilerParams(
            dimension_semantics=("arbitrary",), vmem_limit_bytes=VMEM_LIMIT),
        name="ada",
    )(c, c_ctx, w, b, w_in_t, w_q_b, w_kv_b)


def _in_lat_kernel(x_ref, mod_ref, gmix_ref, wqkv_ref, wconv_ref, gq_ref, wqb_ref, gkv_ref,
                   wkvb_ref, cq_ref, sq_ref, q_ref, k_ref, v_ref, cb_ref, z_ref):
    sh = mod_ref[0, 0:1, :]
    sc = mod_ref[0, 1:2, :]
    h = (_rms(x_ref[...], gmix_ref[...]) * (1.0 + sc) + sh).astype(BF16)

    p2 = _dot(h, wconv_ref[...])
    cb_ref[...] = p2[:, :CONV_W]
    z_ref[...] = p2[:, CONV_W:2 * CONV_W] * p2[:, 2 * CONV_W:]

    p1 = _dot(h, wqkv_ref[...])
    qa = p1[:, :Q_RANK]
    kva = p1[:, Q_RANK:Q_RANK + KV_RANK]
    kr = p1[:, Q_RANK + KV_RANK:Q_RANK + KV_RANK + LANES]

    c1 = cq_ref[...]
    s1 = sq_ref[...]
    reps = N_HEADS * QK_ROPE // LANES
    cq = jnp.concatenate([c1] * reps, axis=1)
    sq = jnp.concatenate([s1] * reps, axis=1)
    scale = QK_DIM ** -0.5 * LOG2E
    q = _dot(_rms(qa, gq_ref[...]).astype(BF16), wqb_ref[...])
    qn = (q[:, :N_HEADS * QK_NOPE] * scale).astype(BF16)
    qr = (_rope(q[:, N_HEADS * QK_NOPE:], cq, sq) * scale).astype(BF16)

    kv = _dot(_rms(kva, gkv_ref[...]).astype(BF16), wkvb_ref[...]).astype(BF16)
    krr = _rope(kr, c1, s1)[:, :QK_ROPE].astype(BF16)
    for hd in range(N_HEADS):
        q_ref[0, hd, :, 0:QK_NOPE] = qn[:, hd * QK_NOPE:(hd + 1) * QK_NOPE]
        q_ref[0, hd, :, QK_NOPE:QK_DIM] = qr[:, hd * QK_ROPE:(hd + 1) * QK_ROPE]
        k_ref[0, hd, :, 0:QK_NOPE] = kv[:, hd * 2 * HEAD:hd * 2 * HEAD + QK_NOPE]
        k_ref[0, hd, :, QK_NOPE:QK_DIM] = krr
        v_ref[0, hd, :, :] = kv[:, hd * 2 * HEAD + QK_NOPE:(hd + 1) * 2 * HEAD]


def _in_lat(x2, mod_l, gmix, wqkv, wconv, gq, wqb, gkv, wkvb, cq, sq, batch, seq):
    tm = TM_IN
    tpb = seq // tm
    const = lambda b, t: (0, 0)
    one = pl.Buffered(1)
    tok = lambda b, t: (b * tpb + t, 0)
    hd4 = lambda b, t: (b, 0, t, 0)
    return pl.pallas_call(
        _in_lat_kernel,
        grid=(batch, tpb),
        in_specs=[
            pl.BlockSpec((tm, D_MODEL), tok),
            pl.BlockSpec((1, 6, D_MODEL), lambda b, t: (b, 0, 0)),
            pl.BlockSpec((1, D_MODEL), const),
            pl.BlockSpec(wqkv.shape, const, pipeline_mode=one),
            pl.BlockSpec(wconv.shape, const, pipeline_mode=one),
            pl.BlockSpec((1, Q_RANK), const),
            pl.BlockSpec(wqb.shape, const, pipeline_mode=one),
            pl.BlockSpec((1, KV_RANK), const),
            pl.BlockSpec(wkvb.shape, const, pipeline_mode=one),
            pl.BlockSpec((tm, LANES), lambda b, t: (t, 0)),
            pl.BlockSpec((tm, LANES), lambda b, t: (t, 0)),
        ],
        out_specs=[
            pl.BlockSpec((1, N_HEADS, tm, QK_DIM), hd4),
            pl.BlockSpec((1, N_HEADS, tm, QK_DIM), hd4),
            pl.BlockSpec((1, N_HEADS, tm, HEAD), hd4),
            pl.BlockSpec((tm, CONV_W), tok),
            pl.BlockSpec((tm, CONV_W), tok),
        ],
        out_shape=[
            jax.ShapeDtypeStruct((batch, N_HEADS, seq, QK_DIM), BF16),
            jax.ShapeDtypeStruct((batch, N_HEADS, seq, QK_DIM), BF16),
            jax.ShapeDtypeStruct((batch, N_HEADS, seq, HEAD), BF16),
            jax.ShapeDtypeStruct((batch * seq, CONV_W), F32),
            jax.ShapeDtypeStruct((batch * seq, CONV_W), F32),
        ],
        compiler_params=pltpu.CompilerParams(
            dimension_semantics=("parallel", "parallel"), vmem_limit_bytes=VMEM_LIMIT),
        name="in_lat",
    )(x2, mod_l, gmix, wqkv, wconv, gq, wqb, gkv, wkvb, cq, sq)


def _in_ctx_kernel(x_ref, mod_ref, gmix_ref, wkv_ref, gkv_ref, wkvb_ref, k_ref, v_ref):
    sh = mod_ref[0, 0:1, :]
    sc = mod_ref[0, 1:2, :]
    h = (_rms(x_ref[0], gmix_ref[...]) * (1.0 + sc) + sh).astype(BF16)
    p1 = _dot(h, wkv_ref[:, Q_RANK:Q_RANK + KV_RANK + LANES])
    kva = p1[:, :KV_RANK]
    kr = p1[:, KV_RANK:KV_RANK + QK_ROPE].astype(BF16)
    kv = _dot(_rms(kva, gkv_ref[...]).astype(BF16), wkvb_ref[...]).astype(BF16)
    for hd in range(N_HEADS):
        k_ref[0, hd, :, 0:QK_NOPE] = kv[:, hd * 2 * HEAD:hd * 2 * HEAD + QK_NOPE]
        k_ref[0, hd, :, QK_NOPE:QK_DIM] = kr
        v_ref[0, hd, :, :] = kv[:, hd * 2 * HEAD + QK_NOPE:(hd + 1) * 2 * HEAD]


def _in_ctx(ctx, mod, gmix, wkv, gkv, wkvb):
    batch, n_ctx, _ = ctx.shape
    const = lambda b: (0, 0)
    return pl.pallas_call(
        _in_ctx_kernel,
        grid=(batch,),
        in_specs=[
            pl.BlockSpec((1, n_ctx, D_MODEL), lambda b: (b, 0, 0)),
            pl.BlockSpec((1, 6, D_MODEL), lambda b: (batch, 0, 0)),
            pl.BlockSpec((1, D_MODEL), const),
            pl.BlockSpec(wkv.shape, const),
            pl.BlockSpec((1, KV_RANK), const),
            pl.BlockSpec(wkvb.shape, const),
        ],
        out_specs=[
            pl.BlockSpec((1, N_HEADS, n_ctx, QK_DIM), lambda b: (b, 0, 0, 0)),
            pl.BlockSpec((1, N_HEADS, n_ctx, HEAD), lambda b: (b, 0, 0, 0)),
        ],
        out_shape=[
            jax.ShapeDtypeStruct((batch, N_HEADS, n_ctx, QK_DIM), BF16),
            jax.ShapeDtypeStruct((batch, N_HEADS, n_ctx, HEAD), BF16),
        ],
        compiler_params=pltpu.CompilerParams(
            dimension_semantics=("parallel",), vmem_limit_bytes=VMEM_LIMIT),
        name="in_ctx",
    )(ctx, mod, gmix, wkv, gkv, wkvb)


UP_LANE_BLKS = N_SUB * 2 * BPS
DN_ROW_BLKS = N_SUB * BPS


def _up_src(c):
    c_in = jnp.minimum(c, UP_LANE_BLKS - 1)
    slab = (c_in % (2 * BPS)) // 2
    blk_in_tile = sum(jnp.where(slab == SLAB_OF_BLOCK[b], b, 0) for b in range(BPS))
    blk = (c_in // (2 * BPS)) * BPS + blk_in_tile
    src = (c_in % 2) * FFN_BLKS + jnp.minimum(blk, FFN_BLKS - 1)
    return src, (c < UP_LANE_BLKS) & (blk < FFN_BLKS)


def _attn_kernel(*refs, n_up, n_dn):
    q_ref, kl_ref, vl_ref, kc_ref, vc_ref = refs[:5]
    wu_refs = refs[5:5 + n_up]
    wd_refs = refs[5 + n_up:5 + n_up + n_dn]
    wo_ref = refs[5 + n_up + n_dn]
    o_ref, wup_ref, wdn_ref, wout_ref = refs[6 + n_up + n_dn:]

    step = ((pl.program_id(0) * pl.num_programs(1) + pl.program_id(1)) * pl.num_programs(2)
            + pl.program_id(2))
    for i in range(n_up):
        _, real = _up_src(step * n_up + i)
        wup_ref[:, i * LANES:(i + 1) * LANES] = jnp.where(real, wu_refs[i][...], 0.0).astype(BF16)
    dn_blk = jnp.minimum(step, DN_ROW_BLKS // n_dn - 1) * n_dn
    for i in range(n_dn):
        wdn_ref[i * LANES:(i + 1) * LANES, :] = jnp.where(dn_blk + i < FFN_BLKS, wd_refs[i][...], 0.0).astype(BF16)
    wout_ref[...] = wo_ref[...].astype(BF16)

    nt = (((1,), (1,)), ((), ()))

    def with_ones(v):
        return jnp.concatenate([v, jnp.ones_like(v)], axis=1)

    for hd in range(ATTN_HEADS):
        vl = with_ones(vl_ref[0, hd])
        vc = with_ones(vc_ref[0, hd])
        for j in range(TQ // TQ_SUB):
            rows = slice(j * TQ_SUB, (j + 1) * TQ_SUB)
            q = q_ref[0, hd, rows, :]
            s_l = lax.dot_general(q, kl_ref[0, hd], nt, preferred_element_type=F32)
            s_c = lax.dot_general(q, kc_ref[0, hd], nt, preferred_element_type=F32)
            m = jnp.maximum(jnp.max(s_l, axis=-1, keepdims=True), jnp.max(s_c, axis=-1, keepdims=True))
            p_l = jnp.exp2(s_l - m).astype(BF16)
            p_c = jnp.exp2(s_c - m).astype(BF16)
            o = _dot(p_l, vl) + _dot(p_c, vc)
            o_ref[0, rows, hd * HEAD:(hd + 1) * HEAD] = o[:, :HEAD] / o[:, HEAD:]


def _attn(q, k_l, v_l, k_c, v_c, w_up, w_down, w_out):
    batch, heads, seq, _ = q.shape
    n_ctx = k_c.shape[2]
    nq = seq // TQ
    hps = ATTN_HEADS
    n_hg = heads // hps
    n_steps = batch * n_hg * nq
    n_up = -(-UP_LANE_BLKS // n_steps)
    n_dn = next(k for k in range(-(-DN_ROW_BLKS // n_steps), DN_ROW_BLKS + 1) if DN_ROW_BLKS % k == 0)
    assert D_MODEL % n_steps == 0 and heads % hps == 0
    wo_rows = D_MODEL // n_steps
    step = lambda b, h, i: (b * n_hg + h) * nq + i
    kv_map = lambda b, h, i: (b, h, 0, 0)
    dn_out = lambda b, h, i: jnp.minimum(step(b, h, i), DN_ROW_BLKS // n_dn - 1)
    up_specs = [pl.BlockSpec((D_MODEL, LANES),
                             functools.partial(lambda b, h, i, k: (0, _up_src(step(b, h, i) * n_up + k)[0]), k=k))
                for k in range(n_up)]
    dn_specs = [pl.BlockSpec((LANES, D_MODEL),
                             functools.partial(
                                 lambda b, h, i, k: (jnp.minimum(dn_out(b, h, i) * n_dn + k, FFN_BLKS - 1), 0), k=k))
                for k in range(n_dn)]
    return pl.pallas_call(
        functools.partial(_attn_kernel, n_up=n_up, n_dn=n_dn),
        grid=(batch, n_hg, nq),
        in_specs=[
            pl.BlockSpec((1, hps, TQ, QK_DIM), lambda b, h, i: (b, h, i, 0)),
            pl.BlockSpec((1, hps, seq, QK_DIM), kv_map),
            pl.BlockSpec((1, hps, seq, HEAD), kv_map),
            pl.BlockSpec((1, hps, n_ctx, QK_DIM), kv_map),
            pl.BlockSpec((1, hps, n_ctx, HEAD), kv_map),
            *up_specs,
            *dn_specs,
            pl.BlockSpec((wo_rows, D_MODEL), lambda b, h, i: (step(b, h, i), 0)),
        ],
        out_specs=[
            pl.BlockSpec((1, TQ, hps * HEAD), lambda b, h, i: (b, i, h)),
            pl.BlockSpec((D_MODEL, n_up * LANES), lambda b, h, i: (0, step(b, h, i))),
            pl.BlockSpec((n_dn * LANES, D_MODEL), lambda b, h, i: (dn_out(b, h, i), 0)),
            pl.BlockSpec((wo_rows, D_MODEL), lambda b, h, i: (step(b, h, i), 0)),
        ],
        out_shape=[
            jax.ShapeDtypeStruct((batch, seq, heads * HEAD), F32),
            jax.ShapeDtypeStruct((D_MODEL, n_steps * n_up * LANES), BF16),
            jax.ShapeDtypeStruct((FFN_PAD, D_MODEL), BF16),
            jax.ShapeDtypeStruct((D_MODEL, D_MODEL), BF16),
        ],
        compiler_params=pltpu.CompilerParams(
            dimension_semantics=("arbitrary", "arbitrary", "arbitrary"),
            vmem_limit_bytes=VMEM_LIMIT),
        name="attn",
    )(q, k_l, v_l, k_c, v_c, *([w_up] * n_up), *([w_down] * n_dn), w_out)


def _out_kernel(att_ref, cb_ref, z_ref, zp_ref, zn_ref, cw_ref, cbias_ref, gmo_ref, wout_ref,
                x_ref, mod_ref, gffn_ref, x1_ref, hf_ref, mix_ref):
    t = pl.program_id(1)
    tm = z_ref.shape[0]
    z = z_ref[...]
    row = lax.broadcasted_iota(jnp.int32, z.shape, 0)
    z_before = jnp.where(t == 0, 0.0, zp_ref[0, SUBLANES - 1:SUBLANES, :])
    z_after = jnp.where(t == pl.num_programs(1) - 1, 0.0, zn_ref[0, 0:1, :])
    z_prev = jnp.where(row == 0, z_before, pltpu.roll(z, 1, 0))
    z_next = jnp.where(row == tm - 1, z_after, pltpu.roll(z, tm - 1, 0))
    conv = cb_ref[...] * (z_prev * cw_ref[0:1, :] + z * cw_ref[1:2, :] + z_next * cw_ref[2:3, :]
                          + cbias_ref[...])

    n_att = att_ref.shape[1] // HEAD
    gt_a = mod_ref[0, 2:3, :]
    ffn_gain = gffn_ref[...] * (1.0 + mod_ref[0, 4:5, :])
    ffn_shift = mod_ref[0, 3:4, :]
    for r in range(0, tm, NORM_ROWS):
        rows = slice(r, r + NORM_ROWS)
        for g in range(D_MODEL // HEAD):
            cols = slice(g * HEAD, (g + 1) * HEAD)
            blk = att_ref[rows, cols] if g < n_att else conv[rows, (g - n_att) * HEAD:(g - n_att + 1) * HEAD]
            mix_ref[rows, cols] = _rms(blk, gmo_ref[:, cols]).astype(BF16)
    x1 = x_ref[...] + gt_a * _dot(mix_ref[...], wout_ref[...])
    x1_ref[...] = x1
    hf_ref[...] = (_rms(x1, ffn_gain) + ffn_shift).astype(BF16)


def _out(att2, cb, z, cw, cbias, gmo, wout, x2, mod_l, gffn, batch, seq):
    tm = TM_OUT
    tpb = seq // tm
    n8 = batch * seq // SUBLANES
    z3 = z.reshape(n8, SUBLANES, CONV_W)
    const = lambda b, t: (0, 0)
    tok = lambda b, t: (b * tpb + t, 0)
    return pl.pallas_call(
        _out_kernel,
        grid=(batch, tpb),
        in_specs=[
            pl.BlockSpec((tm, att2.shape[1]), tok),
            pl.BlockSpec((tm, CONV_W), tok),
            pl.BlockSpec((tm, CONV_W), tok),
            pl.BlockSpec((1, SUBLANES, CONV_W),
                         lambda b, t: (jnp.maximum((b * tpb + t) * (tm // SUBLANES) - 1, 0), 0, 0)),
            pl.BlockSpec((1, SUBLANES, CONV_W),
                         lambda b, t: (jnp.minimum((b * tpb + t + 1) * (tm // SUBLANES), n8 - 1), 0, 0)),
            pl.BlockSpec((3, CONV_W), const),
            pl.BlockSpec((1, CONV_W), const),
            pl.BlockSpec((1, D_MODEL), const),
            pl.BlockSpec(wout.shape, const, pipeline_mode=pl.Buffered(1)),
            pl.BlockSpec((tm, D_MODEL), tok),
            pl.BlockSpec((1, 6, D_MODEL), lambda b, t: (b, 0, 0)),
            pl.BlockSpec((1, D_MODEL), const),
        ],
        out_specs=[
            pl.BlockSpec((tm, D_MODEL), tok),
            pl.BlockSpec((tm, D_MODEL), tok),
        ],
        out_shape=[
            jax.ShapeDtypeStruct((batch * seq, D_MODEL), F32),
            jax.ShapeDtypeStruct((batch * seq, D_MODEL), BF16),
        ],
        scratch_shapes=[pltpu.VMEM((tm, D_MODEL), BF16)],
        compiler_params=pltpu.CompilerParams(
            dimension_semantics=("parallel", "parallel"), vmem_limit_bytes=VMEM_LIMIT),
        name="out_proj",
    )(att2, cb, z, z3, z3, cw, cbias, gmo, wout, x2, mod_l, gffn)


def _ffn_kernel(h_ref, hp_ref, hn_ref, wup_ref, cw_ref, cbias_ref, wdn_ref, x1_ref, mod_ref, gfin_ref,
                o_ref, hbuf_ref, u_ref):
    t = pl.program_id(1)
    f = pl.program_id(2)
    nf = pl.num_programs(2)
    tm = h_ref.shape[0]

    def step(first, last):
        if first:
            hbuf_ref[HALO:HALO + tm, :] = h_ref[...]
            hbuf_ref[0:HALO, :] = jnp.where(t == 0, jnp.zeros_like(hp_ref[0]), hp_ref[0])
            hbuf_ref[HALO + tm:, :] = jnp.where(t == pl.num_programs(1) - 1, jnp.zeros_like(hn_ref[0]), hn_ref[0])
        u_ref[...] = _dot(hbuf_ref[...], wup_ref[...])
        cw = cw_ref[f]
        y = (u_ref[pl.ds(HALO - 1, tm), :] * cw[0:1, :] + u_ref[pl.ds(HALO, tm), :] * cw[1:2, :]
             + u_ref[pl.ds(HALO + 1, tm), :] * cw[2:3, :] + cbias_ref[f])
        act = jnp.concatenate(
            [y[:, 2 * j * LANES:(2 * j + 1) * LANES] * _silu(y[:, (2 * j + 1) * LANES:(2 * j + 2) * LANES])
             for j in SLAB_OF_BLOCK], axis=1).astype(BF16)
        half = TF_SUB // 2
        for rows in ((slice(0, tm // 2), slice(tm // 2, tm)) if last else (slice(0, tm),)):
            d = _dot(act[rows, :half], wdn_ref[:half, :]) + _dot(act[rows, half:], wdn_ref[half:, :])
            acc = d if first else o_ref[rows, :] + d
            if last:
                acc = _rms(x1_ref[rows, :] + mod_ref[0, 5:6, :] * acc, gfin_ref[...])
            o_ref[rows, :] = acc

    pl.when(f == 0)(lambda: step(True, False))
    pl.when((f > 0) & (f < nf - 1))(lambda: step(False, False))
    pl.when(f == nf - 1)(lambda: step(False, True))


def _ffn(hf, wup_t, cw_t, cbias_t, wdn, x1, mod_l, gfin, batch, seq):
    tm = TM_FFN
    tpb = seq // tm
    nf = N_SUB
    tf = TF_SUB
    n_h = batch * seq // HALO
    hf3 = hf.reshape(n_h, HALO, D_MODEL)
    tok = lambda b, t, f: (b * tpb + t, 0)
    return pl.pallas_call(
        _ffn_kernel,
        grid=(batch, tpb, nf),
        in_specs=[
            pl.BlockSpec((tm, D_MODEL), tok),
            pl.BlockSpec((1, HALO, D_MODEL),
                         lambda b, t, f: (jnp.maximum((b * tpb + t) * (tm // HALO) - 1, 0), 0, 0)),
            pl.BlockSpec((1, HALO, D_MODEL),
                         lambda b, t, f: (jnp.minimum((b * tpb + t + 1) * (tm // HALO), n_h - 1), 0, 0)),
            pl.BlockSpec((D_MODEL, 2 * TF_SUB), lambda b, t, f: (0, f)),
            pl.BlockSpec((nf, 3, 2 * TF_SUB), lambda b, t, f: (0, 0, 0)),
            pl.BlockSpec((nf, 1, 2 * TF_SUB), lambda b, t, f: (0, 0, 0)),
            pl.BlockSpec((tf, D_MODEL), lambda b, t, f: (f, 0)),
            pl.BlockSpec((tm, D_MODEL), tok),
            pl.BlockSpec((1, 6, D_MODEL), lambda b, t, f: (b, 0, 0)),
            pl.BlockSpec((1, D_MODEL), lambda b, t, f: (0, 0)),
        ],
        out_specs=pl.BlockSpec((tm, D_MODEL), tok),
        out_shape=jax.ShapeDtypeStruct((batch * seq, D_MODEL), F32),
        scratch_shapes=[
            pltpu.VMEM((tm + 2 * HALO, D_MODEL), BF16),
            pltpu.VMEM((tm + 2 * HALO, 2 * TF_SUB), F32),
        ],
        compiler_params=pltpu.CompilerParams(
            dimension_semantics=("parallel", "parallel", "arbitrary"), vmem_limit_bytes=VMEM_LIMIT),
        name="ffn",
    )(hf, hf3, hf3, wup_t, cw_t, cbias_t, wdn, x1, mod_l, gfin)


def _rope_tables(seq):
    rows = seq // GRID_W
    row = np.repeat(np.arange(rows), GRID_W).astype(np.float64)
    col = np.tile(np.arange(GRID_W), rows).astype(np.float64)
    axis_dim = QK_ROPE // 2
    inv = ROPE_BASE ** (-np.arange(0, axis_dim, 2, dtype=np.float64) / axis_dim)
    ang = np.concatenate([row[:, None] * inv, col[:, None] * inv], axis=-1)
    cos, sin = np.cos(ang), np.sin(ang)
    c = np.repeat(cos, 2, axis=-1)
    s = np.stack([-sin, sin], axis=-1).reshape(seq, QK_ROPE)
    reps = (1, LANES // QK_ROPE)
    return jnp.asarray(np.tile(c, reps), F32), jnp.asarray(np.tile(s, reps), F32)


def _tile_ffn_cols(a):
    pad = ((0, 0), (0, FFN_PAD - FFN_DIM))
    val = jnp.pad(a[:, :FFN_DIM], pad).reshape(a.shape[0], N_SUB, BPS, 1, LANES)
    gate = jnp.pad(a[:, FFN_DIM:], pad).reshape(a.shape[0], N_SUB, BPS, 1, LANES)
    block_of_slab = [SLAB_OF_BLOCK.index(p) for p in range(BPS)]
    both = jnp.concatenate([val, gate], axis=3)[:, :, block_of_slab].reshape(a.shape[0], N_SUB, 2 * TF_SUB)
    return jnp.moveaxis(both, 1, 0)


def kernel(x, c, ctx, c_ctx, w_ada, b_ada, g_mix_norm, w_in, g_q_a, w_q_b, g_kv_a, w_kv_b, conv_w, conv_b, g_mix_out, w_out, g_ffn_norm, w_up, ffn_conv_w, ffn_conv_b, w_down, g_final):
    batch, seq, d = x.shape
    assert d == D_MODEL and w_ada.shape[0] == 1
    x2 = x.reshape(batch * seq, d)

    mod, wconv, wqkv, wqb, wkvb = _ada(c, c_ctx[None], w_ada[0], b_ada, w_in[0].T, w_q_b[0], w_kv_b[0])
    mod = mod.reshape(mod.shape[0], 6, d)

    cw_t = _tile_ffn_cols(ffn_conv_w[0])
    cbias_t = _tile_ffn_cols(ffn_conv_b)

    cq, sq = _rope_tables(seq)

    q, k_l, v_l, cb, z = _in_lat(x2, mod, g_mix_norm, wqkv, wconv, g_q_a, wqb, g_kv_a, wkvb,
                                 cq, sq, batch, seq)
    k_c, v_c = _in_ctx(ctx, mod, g_mix_norm, wqkv, g_kv_a, wkvb)
    att, wup_t, wdn, wout = _attn(q, k_l, v_l, k_c, v_c, w_up[0], w_down[0], w_out[0])
    x1, hf = _out(att.reshape(batch * seq, N_HEADS * HEAD), cb, z, conv_w[0], conv_b, g_mix_out,
                  wout, x2, mod, g_ffn_norm, batch, seq)
    y = _ffn(hf, wup_t, cw_t, cbias_t, wdn, x1, mod, g_final[None], batch, seq)
    return y.reshape(batch, seq, d)
```

```python
import functools

import jax
import jax.numpy as jnp
import numpy as np
from jax import lax
from jax.experimental import pallas as pl
from jax.experimental.pallas import tpu as pltpu

F32 = jnp.float32
BF16 = jnp.bfloat16

D_MODEL = 2048
N_HEADS = 8
HEAD = 128
QK_NOPE = 128
QK_ROPE = 64
QK_DIM = QK_NOPE + QK_ROPE
Q_RANK = 512
KV_RANK = 256
CONV_W = 1024
GRID_W = 64
ROPE_BASE = 10000.0
EPS = 1e-6
LOG2E = 1.4426950408889634
FFN_DIM = 5504

V7X_VMEM_BYTES = 64 * 1024 * 1024
VMEM_LIMIT = V7X_VMEM_BYTES - 8 * 1024 * 1024

TM_IN = 512
TQ = 2048
TQ_SUB = 256
ATTN_HEADS = 2
TM_OUT = 512
NORM_ROWS = 64
TM_FFN = 512
LANES = 128
SUBLANES = 8
TF_SUB = 512
HALO = SUBLANES
BPS = TF_SUB // LANES
FFN_BLKS = FFN_DIM // LANES
N_SUB = (FFN_DIM + TF_SUB - 1) // TF_SUB
FFN_PAD = N_SUB * TF_SUB
DOWN_ROW_PARTS = 2
SLAB_OF_BLOCK = (0, 2, 1, 3)


def _rms(x, g):
    return x * lax.rsqrt(jnp.mean(x * x, axis=-1, keepdims=True) + EPS) * g


def _silu(x):
    h = 0.5 * x
    return h * (1.0 + jnp.tanh(h))


def _dot(a, b):
    return jnp.dot(a, b, preferred_element_type=F32)


def _rope(x, c, s):
    w = x.shape[-1]
    lane = lax.broadcasted_iota(jnp.int32, x.shape, 1)
    from_right = pltpu.roll(x, w - 1, 1)
    from_left = pltpu.roll(x, 1, 1)
    partner = jnp.where(lane % 2 == 0, from_right, from_left)
    return x * c + partner * s


N_QKV = Q_RANK + KV_RANK + QK_ROPE
PREP_ROWS = 512
N_QKV_BLKS = -(-N_QKV // PREP_ROWS)
N_CONV_BLKS = 3 * CONV_W // PREP_ROWS


def _ada_kernel(c_ref, cctx_ref, w_ref, b_ref, wi_ref, wq_ref, wkv_ref,
                o_ref, conv_ref, qkv_ref, wqb_ref, wkvb_ref, lhs_ref, prev_ref):
    j = pl.program_id(0)
    nb = c_ref.shape[0]
    lhs_ref[...] = jnp.zeros_like(lhs_ref)
    lhs_ref[0:nb, :] = _silu(c_ref[...])
    lhs_ref[nb:nb + 1, :] = _silu(cctx_ref[...])
    o_ref[...] = _dot(lhs_ref[...].astype(BF16), w_ref[...].astype(BF16)) + b_ref[...]

    @pl.when(j < N_QKV_BLKS)
    def _():
        qkv_ref[...] = wi_ref[...].T.astype(BF16)

    @pl.when((j >= N_QKV_BLKS) & (j < N_QKV_BLKS + N_CONV_BLKS))
    def _():
        off = N_QKV % PREP_ROWS
        conv_ref[...] = jnp.concatenate([prev_ref[off:, :], wi_ref[:off, :]], axis=0).T.astype(BF16)

    @pl.when(j < N_QKV_BLKS + N_CONV_BLKS - 1)
    def _():
        prev_ref[...] = wi_ref[...]

    @pl.when(j == N_QKV_BLKS + N_CONV_BLKS)
    def _():
        w = wq_ref[...]
        nope = [w[:, h * QK_DIM:h * QK_DIM + QK_NOPE] for h in range(N_HEADS)]
        rope = [w[:, h * QK_DIM + QK_NOPE:(h + 1) * QK_DIM] for h in range(N_HEADS)]
        wqb_ref[...] = jnp.concatenate(nope + rope, axis=1).astype(BF16)

    @pl.when(j == N_QKV_BLKS + N_CONV_BLKS + 1)
    def _():
        wkvb_ref[...] = wkv_ref[...].astype(BF16)


def _ada(c, c_ctx, w, b, w_in_t, w_q_b, w_kv_b):
    n = w.shape[1]
    tn = 1024
    n_blks = N_QKV_BLKS + N_CONV_BLKS
    assert n // tn >= n_blks + 2 and N_QKV_BLKS == N_QKV // PREP_ROWS + 1 and c.shape[0] + 1 <= SUBLANES
    conv_blk = lambda j: jnp.clip(j - N_QKV_BLKS, 0, N_CONV_BLKS - 1)
    const = lambda j: (0, 0)
    return pl.pallas_call(
        _ada_kernel,
        grid=(n // tn,),
        in_specs=[
            pl.BlockSpec(c.shape, const),
            pl.BlockSpec((1, D_MODEL), const),
            pl.BlockSpec((D_MODEL, tn), lambda j: (0, j)),
            pl.BlockSpec((1, tn), lambda j: (0, j)),
            pl.BlockSpec((PREP_ROWS, D_MODEL), lambda j: (jnp.minimum(j, n_blks - 1), 0)),
            pl.BlockSpec(w_q_b.shape, const),
            pl.BlockSpec(w_kv_b.shape, const),
        ],
        out_specs=[
            pl.BlockSpec((SUBLANES, tn), lambda j: (0, j)),
            pl.BlockSpec((D_MODEL, PREP_ROWS), lambda j: (0, conv_blk(j))),
            pl.BlockSpec((D_MODEL, PREP_ROWS), lambda j: (0, jnp.minimum(j, N_QKV_BLKS - 1))),
            pl.BlockSpec(w_q_b.shape, const),
            pl.BlockSpec(w_kv_b.shape, const),
        ],
        out_shape=[
            jax.ShapeDtypeStruct((SUBLANES, n), F32),
            jax.ShapeDtypeStruct((D_MODEL, N_CONV_BLKS * PREP_ROWS), BF16),
            jax.ShapeDtypeStruct((D_MODEL, N_QKV_BLKS * PREP_ROWS), BF16),
            jax.ShapeDtypeStruct(w_q_b.shape, BF16),
            jax.ShapeDtypeStruct(w_kv_b.shape, BF16),
        ],
        scratch_shapes=[pltpu.VMEM((SUBLANES, D_MODEL), F32), pltpu.VMEM((PREP_ROWS, D_MODEL), F32)],
        compiler_params=pltpu.CompilerParams(
            dimension_semantics=("arbitrary",), vmem_limit_bytes=VMEM_LIMIT),
        name="ada",
    )(c, c_ctx, w, b, w_in_t, w_q_b, w_kv_b)


def _in_lat_kernel(x_ref, mod_ref, gmix_ref, wqkv_ref, wconv_ref, gq_ref, wqb_ref, gkv_ref,
                   wkvb_ref, cq_ref, sq_ref, q_ref, k_ref, v_ref, cb_ref, z_ref):
    sh = mod_ref[0, 0:1, :]
    sc = mod_ref[0, 1:2, :]
    h = (_rms(x_ref[...], gmix_ref[...]) * (1.0 + sc) + sh).astype(BF16)

    p2 = _dot(h, wconv_ref[...])
    cb_ref[...] = p2[:, :CONV_W]
    z_ref[...] = p2[:, CONV_W:2 * CONV_W] * p2[:, 2 * CONV_W:]

    p1 = _dot(h, wqkv_ref[...])
    qa = p1[:, :Q_RANK]
    kva = p1[:, Q_RANK:Q_RANK + KV_RANK]
    kr = p1[:, Q_RANK + KV_RANK:Q_RANK + KV_RANK + LANES]

    c1 = cq_ref[...]
    s1 = sq_ref[...]
    reps = N_HEADS * QK_ROPE // LANES
    cq = jnp.concatenate([c1] * reps, axis=1)
    sq = jnp.concatenate([s1] * reps, axis=1)
    scale = QK_DIM ** -0.5 * LOG2E
    q = _dot(_rms(qa, gq_ref[...]).astype(BF16), wqb_ref[...])
    qn = (q[:, :N_HEADS * QK_NOPE] * scale).astype(BF16)
    qr = (_rope(q[:, N_HEADS * QK_NOPE:], cq, sq) * scale).astype(BF16)

    kv = _dot(_rms(kva, gkv_ref[...]).astype(BF16), wkvb_ref[...]).astype(BF16)
    krr = _rope(kr, c1, s1)[:, :QK_ROPE].astype(BF16)
    for hd in range(N_HEADS):
        q_ref[0, hd, :, 0:QK_NOPE] = qn[:, hd * QK_NOPE:(hd + 1) * QK_NOPE]
        q_ref[0, hd, :, QK_NOPE:QK_DIM] = qr[:, hd * QK_ROPE:(hd + 1) * QK_ROPE]
        k_ref[0, hd, :, 0:QK_NOPE] = kv[:, hd * 2 * HEAD:hd * 2 * HEAD + QK_NOPE]
        k_ref[0, hd, :, QK_NOPE:QK_DIM] = krr
        v_ref[0, hd, :, :] = kv[:, hd * 2 * HEAD + QK_NOPE:(hd + 1) * 2 * HEAD]


def _in_lat(x2, mod_l, gmix, wqkv, wconv, gq, wqb, gkv, wkvb, cq, sq, batch, seq):
    tm = TM_IN
    tpb = seq // tm
    const = lambda b, t: (0, 0)
    one = pl.Buffered(1)
    tok = lambda b, t: (b * tpb + t, 0)
    hd4 = lambda b, t: (b, 0, t, 0)
    return pl.pallas_call(
        _in_lat_kernel,
        grid=(batch, tpb),
        in_specs=[
            pl.BlockSpec((tm, D_MODEL), tok),
            pl.BlockSpec((1, 6, D_MODEL), lambda b, t: (b, 0, 0)),
            pl.BlockSpec((1, D_MODEL), const),
            pl.BlockSpec(wqkv.shape, const, pipeline_mode=one),
            pl.BlockSpec(wconv.shape, const, pipeline_mode=one),
            pl.BlockSpec((1, Q_RANK), const),
            pl.BlockSpec(wqb.shape, const, pipeline_mode=one),
            pl.BlockSpec((1, KV_RANK), const),
            pl.BlockSpec(wkvb.shape, const, pipeline_mode=one),
            pl.BlockSpec((tm, LANES), lambda b, t: (t, 0)),
            pl.BlockSpec((tm, LANES), lambda b, t: (t, 0)),
        ],
        out_specs=[
            pl.BlockSpec((1, N_HEADS, tm, QK_DIM), hd4),
            pl.BlockSpec((1, N_HEADS, tm, QK_DIM), hd4),
            pl.BlockSpec((1, N_HEADS, tm, HEAD), hd4),
            pl.BlockSpec((tm, CONV_W), tok),
            pl.BlockSpec((tm, CONV_W), tok),
        ],
        out_shape=[
            jax.ShapeDtypeStruct((batch, N_HEADS, seq, QK_DIM), BF16),
            jax.ShapeDtypeStruct((batch, N_HEADS, seq, QK_DIM), BF16),
            jax.ShapeDtypeStruct((batch, N_HEADS, seq, HEAD), BF16),
            jax.ShapeDtypeStruct((batch * seq, CONV_W), F32),
            jax.ShapeDtypeStruct((batch * seq, CONV_W), F32),
        ],
        compiler_params=pltpu.CompilerParams(
            dimension_semantics=("parallel", "parallel"), vmem_limit_bytes=VMEM_LIMIT),
        name="in_lat",
    )(x2, mod_l, gmix, wqkv, wconv, gq, wqb, gkv, wkvb, cq, sq)


def _in_ctx_kernel(x_ref, mod_ref, gmix_ref, wkv_ref, gkv_ref, wkvb_ref, k_ref, v_ref):
    sh = mod_ref[0, 0:1, :]
    sc = mod_ref[0, 1:2, :]
    h = (_rms(x_ref[0], gmix_ref[...]) * (1.0 + sc) + sh).astype(BF16)
    p1 = _dot(h, wkv_ref[:, Q_RANK:Q_RANK + KV_RANK + LANES])
    kva = p1[:, :KV_RANK]
    kr = p1[:, KV_RANK:KV_RANK + QK_ROPE].astype(BF16)
    kv = _dot(_rms(kva, gkv_ref[...]).astype(BF16), wkvb_ref[...]).astype(BF16)
    for hd in range(N_HEADS):
        k_ref[0, hd, :, 0:QK_NOPE] = kv[:, hd * 2 * HEAD:hd * 2 * HEAD + QK_NOPE]
        k_ref[0, hd, :, QK_NOPE:QK_DIM] = kr
        v_ref[0, hd, :, :] = kv[:, hd * 2 * HEAD + QK_NOPE:(hd + 1) * 2 * HEAD]


def _in_ctx(ctx, mod, gmix, wkv, gkv, wkvb):
    batch, n_ctx, _ = ctx.shape
    const = lambda b: (0, 0)
    return pl.pallas_call(
        _in_ctx_kernel,
        grid=(batch,),
        in_specs=[
            pl.BlockSpec((1, n_ctx, D_MODEL), lambda b: (b, 0, 0)),
            pl.BlockSpec((1, 6, D_MODEL), lambda b: (batch, 0, 0)),
            pl.BlockSpec((1, D_MODEL), const),
            pl.BlockSpec(wkv.shape, const),
            pl.BlockSpec((1, KV_RANK), const),
            pl.BlockSpec(wkvb.shape, const),
        ],
        out_specs=[
            pl.BlockSpec((1, N_HEADS, n_ctx, QK_DIM), lambda b: (b, 0, 0, 0)),
            pl.BlockSpec((1, N_HEADS, n_ctx, HEAD), lambda b: (b, 0, 0, 0)),
        ],
        out_shape=[
            jax.ShapeDtypeStruct((batch, N_HEADS, n_ctx, QK_DIM), BF16),
            jax.ShapeDtypeStruct((batch, N_HEADS, n_ctx, HEAD), BF16),
        ],
        compiler_params=pltpu.CompilerParams(
            dimension_semantics=("parallel",), vmem_limit_bytes=VMEM_LIMIT),
        name="in_ctx",
    )(ctx, mod, gmix, wkv, gkv, wkvb)


UP_LANE_BLKS = N_SUB * 2 * BPS
DN_ROW_BLKS = N_SUB * BPS


def _up_src(c):
    c_in = jnp.minimum(c, UP_LANE_BLKS - 1)
    slab = (c_in % (2 * BPS)) // 2
    blk_in_tile = sum(jnp.where(slab == SLAB_OF_BLOCK[b], b, 0) for b in range(BPS))
    blk = (c_in // (2 * BPS)) * BPS + blk_in_tile
    src = (c_in % 2) * FFN_BLKS + jnp.minimum(blk, FFN_BLKS - 1)
    return src, (c < UP_LANE_BLKS) & (blk < FFN_BLKS)


def _attn_kernel(*refs, n_up, n_dn):
    q_ref, kl_ref, vl_ref, kc_ref, vc_ref = refs[:5]
    wu_refs = refs[5:5 + n_up]
    wd_refs = refs[5 + n_up:5 + n_up + n_dn]
    wo_ref = refs[5 + n_up + n_dn]
    o_ref, wup_ref, wdn_ref, wout_ref = refs[6 + n_up + n_dn:]

    step = ((pl.program_id(0) * pl.num_programs(1) + pl.program_id(1)) * pl.num_programs(2)
            + pl.program_id(2))
    for i in range(n_up):
        _, real = _up_src(step * n_up + i)
        wup_ref[:, i * LANES:(i + 1) * LANES] = jnp.where(real, wu_refs[i][...], 0.0).astype(BF16)
    dn_blk = jnp.minimum(step, DN_ROW_BLKS // n_dn - 1) * n_dn
    for i in range(n_dn):
        wdn_ref[i * LANES:(i + 1) * LANES, :] = jnp.where(dn_blk + i < FFN_BLKS, wd_refs[i][...], 0.0).astype(BF16)
    wout_ref[...] = wo_ref[...].astype(BF16)

    nt = (((1,), (1,)), ((), ()))

    def with_ones(v):
        return jnp.concatenate([v, jnp.ones_like(v)], axis=1)

    for hd in range(ATTN_HEADS):
        vl = with_ones(vl_ref[0, hd])
        vc = with_ones(vc_ref[0, hd])
        for j in range(TQ // TQ_SUB):
            rows = slice(j * TQ_SUB, (j + 1) * TQ_SUB)
            q = q_ref[0, hd, rows, :]
            s_l = lax.dot_general(q, kl_ref[0, hd], nt, preferred_element_type=F32)
            s_c = lax.dot_general(q, kc_ref[0, hd], nt, preferred_element_type=F32)
            m = jnp.maximum(jnp.max(s_l, axis=-1, keepdims=True), jnp.max(s_c, axis=-1, keepdims=True))
            p_l = jnp.exp2(s_l - m).astype(BF16)
            p_c = jnp.exp2(s_c - m).astype(BF16)
            o = _dot(p_l, vl) + _dot(p_c, vc)
            o_ref[0, rows, hd * HEAD:(hd + 1) * HEAD] = o[:, :HEAD] / o[:, HEAD:]


def _attn(q, k_l, v_l, k_c, v_c, w_up, w_down, w_out):
    batch, heads, seq, _ = q.shape
    n_ctx = k_c.shape[2]
    nq = seq // TQ
    hps = ATTN_HEADS
    n_hg = heads // hps
    n_steps = batch * n_hg * nq
    n_up = -(-UP_LANE_BLKS // n_steps)
    n_dn = next(k for k in range(-(-DN_ROW_BLKS // n_steps), DN_ROW_BLKS + 1) if DN_ROW_BLKS % k == 0)
    assert D_MODEL % n_steps == 0 and heads % hps == 0
    wo_rows = D_MODEL // n_steps
    step = lambda b, h, i: (b * n_hg + h) * nq + i
    kv_map = lambda b, h, i: (b, h, 0, 0)
    dn_out = lambda b, h, i: jnp.minimum(step(b, h, i), DN_ROW_BLKS // n_dn - 1)
    up_specs = [pl.BlockSpec((D_MODEL, LANES),
                             functools.partial(lambda b, h, i, k: (0, _up_src(step(b, h, i) * n_up + k)[0]), k=k))
                for k in range(n_up)]
    dn_specs = [pl.BlockSpec((LANES, D_MODEL),
                             functools.partial(
                                 lambda b, h, i, k: (jnp.minimum(dn_out(b, h, i) * n_dn + k, FFN_BLKS - 1), 0), k=k))
                for k in range(n_dn)]
    return pl.pallas_call(
        functools.partial(_attn_kernel, n_up=n_up, n_dn=n_dn),
        grid=(batch, n_hg, nq),
        in_specs=[
            pl.BlockSpec((1, hps, TQ, QK_DIM), lambda b, h, i: (b, h, i, 0)),
            pl.BlockSpec((1, hps, seq, QK_DIM), kv_map),
            pl.BlockSpec((1, hps, seq, HEAD), kv_map),
            pl.BlockSpec((1, hps, n_ctx, QK_DIM), kv_map),
            pl.BlockSpec((1, hps, n_ctx, HEAD), kv_map),
            *up_specs,
            *dn_specs,
            pl.BlockSpec((wo_rows, D_MODEL), lambda b, h, i: (step(b, h, i), 0)),
        ],
        out_specs=[
            pl.BlockSpec((1, TQ, hps * HEAD), lambda b, h, i: (b, i, h)),
            pl.BlockSpec((D_MODEL, n_up * LANES), lambda b, h, i: (0, step(b, h, i))),
            pl.BlockSpec((n_dn * LANES, D_MODEL), lambda b, h, i: (dn_out(b, h, i), 0)),
            pl.BlockSpec((wo_rows, D_MODEL), lambda b, h, i: (step(b, h, i), 0)),
        ],
        out_shape=[
            jax.ShapeDtypeStruct((batch, seq, heads * HEAD), F32),
            jax.ShapeDtypeStruct((D_MODEL, n_steps * n_up * LANES), BF16),
            jax.ShapeDtypeStruct((FFN_PAD, D_MODEL), BF16),
            jax.ShapeDtypeStruct((D_MODEL, D_MODEL), BF16),
        ],
        compiler_params=pltpu.CompilerParams(
            dimension_semantics=("arbitrary", "arbitrary", "arbitrary"),
            vmem_limit_bytes=VMEM_LIMIT),
        name="attn",
    )(q, k_l, v_l, k_c, v_c, *([w_up] * n_up), *([w_down] * n_dn), w_out)


def _out_kernel(att_ref, cb_ref, z_ref, zp_ref, zn_ref, cw_ref, cbias_ref, gmo_ref, wout_ref,
                x_ref, mod_ref, gffn_ref, x1_ref, hf_ref, mix_ref):
    t = pl.program_id(1)
    tm = z_ref.shape[0]
    z = z_ref[...]
    row = lax.broadcasted_iota(jnp.int32, z.shape, 0)
    z_before = jnp.where(t == 0, 0.0, zp_ref[0, SUBLANES - 1:SUBLANES, :])
    z_after = jnp.where(t == pl.num_programs(1) - 1, 0.0, zn_ref[0, 0:1, :])
    z_prev = jnp.where(row == 0, z_before, pltpu.roll(z, 1, 0))
    z_next = jnp.where(row == tm - 1, z_after, pltpu.roll(z, tm - 1, 0))
    conv = cb_ref[...] * (z_prev * cw_ref[0:1, :] + z * cw_ref[1:2, :] + z_next * cw_ref[2:3, :]
                          + cbias_ref[...])

    n_att = att_ref.shape[1] // HEAD
    gt_a = mod_ref[0, 2:3, :]
    ffn_gain = gffn_ref[...] * (1.0 + mod_ref[0, 4:5, :])
    ffn_shift = mod_ref[0, 3:4, :]
    for r in range(0, tm, NORM_ROWS):
        rows = slice(r, r + NORM_ROWS)
        for g in range(D_MODEL // HEAD):
            cols = slice(g * HEAD, (g + 1) * HEAD)
            blk = att_ref[rows, cols] if g < n_att else conv[rows, (g - n_att) * HEAD:(g - n_att + 1) * HEAD]
            mix_ref[rows, cols] = _rms(blk, gmo_ref[:, cols]).astype(BF16)
    x1 = x_ref[...] + gt_a * _dot(mix_ref[...], wout_ref[...])
    x1_ref[...] = x1
    hf_ref[...] = (_rms(x1, ffn_gain) + ffn_shift).astype(BF16)


def _out(att2, cb, z, cw, cbias, gmo, wout, x2, mod_l, gffn, batch, seq):
    tm = TM_OUT
    tpb = seq // tm
    n8 = batch * seq // SUBLANES
    z3 = z.reshape(n8, SUBLANES, CONV_W)
    const = lambda b, t: (0, 0)
    tok = lambda b, t: (b * tpb + t, 0)
    return pl.pallas_call(
        _out_kernel,
        grid=(batch, tpb),
        in_specs=[
            pl.BlockSpec((tm, att2.shape[1]), tok),
            pl.BlockSpec((tm, CONV_W), tok),
            pl.BlockSpec((tm, CONV_W), tok),
            pl.BlockSpec((1, SUBLANES, CONV_W),
                         lambda b, t: (jnp.maximum((b * tpb + t) * (tm // SUBLANES) - 1, 0), 0, 0)),
            pl.BlockSpec((1, SUBLANES, CONV_W),
                         lambda b, t: (jnp.minimum((b * tpb + t + 1) * (tm // SUBLANES), n8 - 1), 0, 0)),
            pl.BlockSpec((3, CONV_W), const),
            pl.BlockSpec((1, CONV_W), const),
            pl.BlockSpec((1, D_MODEL), const),
            pl.BlockSpec(wout.shape, const, pipeline_mode=pl.Buffered(1)),
            pl.BlockSpec((tm, D_MODEL), tok),
            pl.BlockSpec((1, 6, D_MODEL), lambda b, t: (b, 0, 0)),
            pl.BlockSpec((1, D_MODEL), const),
        ],
        out_specs=[
            pl.BlockSpec((tm, D_MODEL), tok),
            pl.BlockSpec((tm, D_MODEL), tok),
        ],
        out_shape=[
            jax.ShapeDtypeStruct((batch * seq, D_MODEL), F32),
            jax.ShapeDtypeStruct((batch * seq, D_MODEL), BF16),
        ],
        scratch_shapes=[pltpu.VMEM((tm, D_MODEL), BF16)],
        compiler_params=pltpu.CompilerParams(
            dimension_semantics=("parallel", "parallel"), vmem_limit_bytes=VMEM_LIMIT),
        name="out_proj",
    )(att2, cb, z, z3, z3, cw, cbias, gmo, wout, x2, mod_l, gffn)


def _ffn_kernel(h_ref, hp_ref, hn_ref, wup_ref, cw_ref, cbias_ref, wdn_ref, x1_ref, mod_ref, gfin_ref,
                o_ref, hbuf_ref, u_ref):
    t = pl.program_id(1)
    f = pl.program_id(2)
    nf = pl.num_programs(2)
    tm = h_ref.shape[0]

    def step(first, last):
        if first:
            hbuf_ref[HALO:HALO + tm, :] = h_ref[...]
            hbuf_ref[0:HALO, :] = jnp.where(t == 0, jnp.zeros_like(hp_ref[0]), hp_ref[0])
            hbuf_ref[HALO + tm:, :] = jnp.where(t == pl.num_programs(1) - 1, jnp.zeros_like(hn_ref[0]), hn_ref[0])
        split = tm // 2 + 2 * HALO
        u_ref[0:split, :] = _dot(hbuf_ref[0:split, :], wup_ref[...])
        u_ref[split:, :] = _dot(hbuf_ref[split:, :], wup_ref[...])
        cw = cw_ref[f]
        y = (u_ref[pl.ds(HALO - 1, tm), :] * cw[0:1, :] + u_ref[pl.ds(HALO, tm), :] * cw[1:2, :]
             + u_ref[pl.ds(HALO + 1, tm), :] * cw[2:3, :] + cbias_ref[f])
        act = jnp.concatenate(
            [y[:, 2 * j * LANES:(2 * j + 1) * LANES] * _silu(y[:, (2 * j + 1) * LANES:(2 * j + 2) * LANES])
             for j in SLAB_OF_BLOCK], axis=1).astype(BF16)
        half = TF_SUB // 2
        for rows in [slice(r, r + tm // DOWN_ROW_PARTS) for r in range(0, tm, tm // DOWN_ROW_PARTS)]:
            d = _dot(act[rows, :half], wdn_ref[:half, :]) + _dot(act[rows, half:], wdn_ref[half:, :])
            acc = d if first else o_ref[rows, :] + d
            if last:
                acc = _rms(x1_ref[rows, :] + mod_ref[0, 5:6, :] * acc, gfin_ref[...])
            o_ref[rows, :] = acc

    pl.when(f == 0)(lambda: step(True, False))
    pl.when((f > 0) & (f < nf - 1))(lambda: step(False, False))
    pl.when(f == nf - 1)(lambda: step(False, True))


def _ffn(hf, wup_t, cw_t, cbias_t, wdn, x1, mod_l, gfin, batch, seq):
    tm = TM_FFN
    tpb = seq // tm
    nf = N_SUB
    tf = TF_SUB
    n_h = batch * seq // HALO
    hf3 = hf.reshape(n_h, HALO, D_MODEL)
    tok = lambda b, t, f: (b * tpb + t, 0)
    return pl.pallas_call(
        _ffn_kernel,
        grid=(batch, tpb, nf),
        in_specs=[
            pl.BlockSpec((tm, D_MODEL), tok),
            pl.BlockSpec((1, HALO, D_MODEL),
                         lambda b, t, f: (jnp.maximum((b * tpb + t) * (tm // HALO) - 1, 0), 0, 0)),
            pl.BlockSpec((1, HALO, D_MODEL),
                         lambda b, t, f: (jnp.minimum((b * tpb + t + 1) * (tm // HALO), n_h - 1), 0, 0)),
            pl.BlockSpec((D_MODEL, 2 * TF_SUB), lambda b, t, f: (0, f)),
            pl.BlockSpec((nf, 3, 2 * TF_SUB), lambda b, t, f: (0, 0, 0)),
            pl.BlockSpec((nf, 1, 2 * TF_SUB), lambda b, t, f: (0, 0, 0)),
            pl.BlockSpec((tf, D_MODEL), lambda b, t, f: (f, 0)),
            pl.BlockSpec((tm, D_MODEL), tok),
            pl.BlockSpec((1, 6, D_MODEL), lambda b, t, f: (b, 0, 0)),
            pl.BlockSpec((1, D_MODEL), lambda b, t, f: (0, 0)),
        ],
        out_specs=pl.BlockSpec((tm, D_MODEL), tok),
        out_shape=jax.ShapeDtypeStruct((batch * seq, D_MODEL), F32),
        scratch_shapes=[
            pltpu.VMEM((tm + 2 * HALO, D_MODEL), BF16),
            pltpu.VMEM((tm + 2 * HALO, 2 * TF_SUB), F32),
        ],
        compiler_params=pltpu.CompilerParams(
            dimension_semantics=("parallel", "parallel", "arbitrary"), vmem_limit_bytes=VMEM_LIMIT),
        name="ffn",
    )(hf, hf3, hf3, wup_t, cw_t, cbias_t, wdn, x1, mod_l, gfin)


def _rope_tables(seq):
    rows = seq // GRID_W
    row = np.repeat(np.arange(rows), GRID_W).astype(np.float64)
    col = np.tile(np.arange(GRID_W), rows).astype(np.float64)
    axis_dim = QK_ROPE // 2
    inv = ROPE_BASE ** (-np.arange(0, axis_dim, 2, dtype=np.float64) / axis_dim)
    ang = np.concatenate([row[:, None] * inv, col[:, None] * inv], axis=-1)
    cos, sin = np.cos(ang), np.sin(ang)
    c = np.repeat(cos, 2, axis=-1)
    s = np.stack([-sin, sin], axis=-1).reshape(seq, QK_ROPE)
    reps = (1, LANES // QK_ROPE)
    return jnp.asarray(np.tile(c, reps), F32), jnp.asarray(np.tile(s, reps), F32)


def _tile_ffn_cols(a):
    pad = ((0, 0), (0, FFN_PAD - FFN_DIM))
    val = jnp.pad(a[:, :FFN_DIM], pad).reshape(a.shape[0], N_SUB, BPS, 1, LANES)
    gate = jnp.pad(a[:, FFN_DIM:], pad).reshape(a.shape[0], N_SUB, BPS, 1, LANES)
    block_of_slab = [SLAB_OF_BLOCK.index(p) for p in range(BPS)]
    both = jnp.concatenate([val, gate], axis=3)[:, :, block_of_slab].reshape(a.shape[0], N_SUB, 2 * TF_SUB)
    return jnp.moveaxis(both, 1, 0)


def kernel(x, c, ctx, c_ctx, w_ada, b_ada, g_mix_norm, w_in, g_q_a, w_q_b, g_kv_a, w_kv_b, conv_w, conv_b, g_mix_out, w_out, g_ffn_norm, w_up, ffn_conv_w, ffn_conv_b, w_down, g_final):
    batch, seq, d = x.shape
    assert d == D_MODEL and w_ada.shape[0] == 1
    x2 = x.reshape(batch * seq, d)

    mod, wconv, wqkv, wqb, wkvb = _ada(c, c_ctx[None], w_ada[0], b_ada, w_in[0].T, w_q_b[0], w_kv_b[0])
    mod = mod.reshape(mod.shape[0], 6, d)

    cw_t = _tile_ffn_cols(ffn_conv_w[0])
    cbias_t = _tile_ffn_cols(ffn_conv_b)

    cq, sq = _rope_tables(seq)

    q, k_l, v_l, cb, z = _in_lat(x2, mod, g_mix_norm, wqkv, wconv, g_q_a, wqb, g_kv_a, wkvb,
                                 cq, sq, batch, seq)
    k_c, v_c = _in_ctx(ctx, mod, g_mix_norm, wqkv, g_kv_a, wkvb)
    att, wup_t, wdn, wout = _attn(q, k_l, v_l, k_c, v_c, w_up[0], w_down[0], w_out[0])
    x1, hf = _out(att.reshape(batch * seq, N_HEADS * HEAD), cb, z, conv_w[0], conv_b, g_mix_out,
                  wout, x2, mod, g_ffn_norm, batch, seq)
    y = _ffn(hf, wup_t, cw_t, cbias_t, wdn, x1, mod, g_final[None], batch, seq)
    return y.reshape(batch, seq, d)
```

```python
import functools

import jax
import jax.numpy as jnp
import numpy as np
from jax import lax
from jax.experimental import pallas as pl
from jax.experimental.pallas import tpu as pltpu

F32 = jnp.float32
BF16 = jnp.bfloat16

D_MODEL = 2048
N_HEADS = 8
HEAD = 128
QK_NOPE = 128
QK_ROPE = 64
QK_DIM = QK_NOPE + QK_ROPE
Q_RANK = 512
KV_RANK = 256
CONV_W = 1024
GRID_W = 64
ROPE_BASE = 10000.0
EPS = 1e-6
LOG2E = 1.4426950408889634
FFN_DIM = 5504

V7X_VMEM_BYTES = 64 * 1024 * 1024
VMEM_LIMIT = V7X_VMEM_BYTES - 8 * 1024 * 1024

TM_IN = 512
TQ = 2048
TQ_SUB = 512
ATTN_HEADS = 2
TM_OUT = 512
NORM_ROWS = 64
TM_FFN = 512
LANES = 128
SUBLANES = 8
TF_SUB = 512
HALO = SUBLANES
BPS = TF_SUB // LANES
FFN_BLKS = FFN_DIM // LANES
N_SUB = (FFN_DIM + TF_SUB - 1) // TF_SUB
FFN_PAD = N_SUB * TF_SUB
DOWN_ROW_PARTS = 2
SLAB_OF_BLOCK = (0, 2, 1, 3)


def _rms(x, g):
    return x * lax.rsqrt(jnp.mean(x * x, axis=-1, keepdims=True) + EPS) * g


def _silu(x):
    h = 0.5 * x
    return h * (1.0 + jnp.tanh(h))


def _dot(a, b):
    return jnp.dot(a, b, preferred_element_type=F32)


def _rope(x, c, s):
    w = x.shape[-1]
    lane = lax.broadcasted_iota(jnp.int32, x.shape, 1)
    from_right = pltpu.roll(x, w - 1, 1)
    from_left = pltpu.roll(x, 1, 1)
    partner = jnp.where(lane % 2 == 0, from_right, from_left)
    return x * c + partner * s


N_QKV = Q_RANK + KV_RANK + QK_ROPE
PREP_ROWS = 512
N_QKV_BLKS = -(-N_QKV // PREP_ROWS)
N_CONV_BLKS = 3 * CONV_W // PREP_ROWS


def _ada_kernel(c_ref, cctx_ref, w_ref, b_ref, wi_ref, wq_ref, wkv_ref,
                o_ref, conv_ref, qkv_ref, wqb_ref, wkvb_ref, lhs_ref, prev_ref):
    j = pl.program_id(0)
    nb = c_ref.shape[0]
    lhs_ref[...] = jnp.zeros_like(lhs_ref)
    lhs_ref[0:nb, :] = _silu(c_ref[...])
    lhs_ref[nb:nb + 1, :] = _silu(cctx_ref[...])
    o_ref[...] = _dot(lhs_ref[...].astype(BF16), w_ref[...].astype(BF16)) + b_ref[...]

    @pl.when(j < N_QKV_BLKS)
    def _():
        qkv_ref[...] = wi_ref[...].T.astype(BF16)

    @pl.when((j >= N_QKV_BLKS) & (j < N_QKV_BLKS + N_CONV_BLKS))
    def _():
        off = N_QKV % PREP_ROWS
        conv_ref[...] = jnp.concatenate([prev_ref[off:, :], wi_ref[:off, :]], axis=0).T.astype(BF16)

    @pl.when(j < N_QKV_BLKS + N_CONV_BLKS - 1)
    def _():
        prev_ref[...] = wi_ref[...]

    @pl.when(j == N_QKV_BLKS + N_CONV_BLKS)
    def _():
        w = wq_ref[...]
        nope = [w[:, h * QK_DIM:h * QK_DIM + QK_NOPE] for h in range(N_HEADS)]
        rope = [w[:, h * QK_DIM + QK_NOPE:(h + 1) * QK_DIM] for h in range(N_HEADS)]
        wqb_ref[...] = jnp.concatenate(nope + rope, axis=1).astype(BF16)

    @pl.when(j == N_QKV_BLKS + N_CONV_BLKS + 1)
    def _():
        wkvb_ref[...] = wkv_ref[...].astype(BF16)


def _ada(c, c_ctx, w, b, w_in_t, w_q_b, w_kv_b):
    n = w.shape[1]
    tn = 1024
    n_blks = N_QKV_BLKS + N_CONV_BLKS
    assert n // tn >= n_blks + 2 and N_QKV_BLKS == N_QKV // PREP_ROWS + 1 and c.shape[0] + 1 <= SUBLANES
    conv_blk = lambda j: jnp.clip(j - N_QKV_BLKS, 0, N_CONV_BLKS - 1)
    const = lambda j: (0, 0)
    return pl.pallas_call(
        _ada_kernel,
        grid=(n // tn,),
        in_specs=[
            pl.BlockSpec(c.shape, const),
            pl.BlockSpec((1, D_MODEL), const),
            pl.BlockSpec((D_MODEL, tn), lambda j: (0, j)),
            pl.BlockSpec((1, tn), lambda j: (0, j)),
            pl.BlockSpec((PREP_ROWS, D_MODEL), lambda j: (jnp.minimum(j, n_blks - 1), 0)),
            pl.BlockSpec(w_q_b.shape, const),
            pl.BlockSpec(w_kv_b.shape, const),
        ],
        out_specs=[
            pl.BlockSpec((SUBLANES, tn), lambda j: (0, j)),
            pl.BlockSpec((D_MODEL, PREP_ROWS), lambda j: (0, conv_blk(j))),
            pl.BlockSpec((D_MODEL, PREP_ROWS), lambda j: (0, jnp.minimum(j, N_QKV_BLKS - 1))),
            pl.BlockSpec(w_q_b.shape, const),
            pl.BlockSpec(w_kv_b.shape, const),
        ],
        out_shape=[
            jax.ShapeDtypeStruct((SUBLANES, n), F32),
            jax.ShapeDtypeStruct((D_MODEL, N_CONV_BLKS * PREP_ROWS), BF16),
            jax.ShapeDtypeStruct((D_MODEL, N_QKV_BLKS * PREP_ROWS), BF16),
            jax.ShapeDtypeStruct(w_q_b.shape, BF16),
            jax.ShapeDtypeStruct(w_kv_b.shape, BF16),
        ],
        scratch_shapes=[pltpu.VMEM((SUBLANES, D_MODEL), F32), pltpu.VMEM((PREP_ROWS, D_MODEL), F32)],
        compiler_params=pltpu.CompilerParams(
            dimension_semantics=("arbitrary",), vmem_limit_bytes=VMEM_LIMIT),
        name="ada",
    )(c, c_ctx, w, b, w_in_t, w_q_b, w_kv_b)


def _in_lat_kernel(x_ref, mod_ref, gmix_ref, wqkv_ref, wconv_ref, gq_ref, wqb_ref, gkv_ref,
                   wkvb_ref, cq_ref, sq_ref, q_ref, k_ref, v_ref, cb_ref, z_ref):
    sh = mod_ref[0, 0:1, :]
    sc = mod_ref[0, 1:2, :]
    h = (_rms(x_ref[...], gmix_ref[...]) * (1.0 + sc) + sh).astype(BF16)

    p2 = _dot(h, wconv_ref[...])
    cb_ref[...] = p2[:, :CONV_W]
    z_ref[...] = p2[:, CONV_W:2 * CONV_W] * p2[:, 2 * CONV_W:]

    p1 = _dot(h, wqkv_ref[...])
    qa = p1[:, :Q_RANK]
    kva = p1[:, Q_RANK:Q_RANK + KV_RANK]
    kr = p1[:, Q_RANK + KV_RANK:Q_RANK + KV_RANK + LANES]

    c1 = cq_ref[...]
    s1 = sq_ref[...]
    reps = N_HEADS * QK_ROPE // LANES
    cq = jnp.concatenate([c1] * reps, axis=1)
    sq = jnp.concatenate([s1] * reps, axis=1)
    scale = QK_DIM ** -0.5 * LOG2E
    q = _dot(_rms(qa, gq_ref[...]).astype(BF16), wqb_ref[...])
    qn = (q[:, :N_HEADS * QK_NOPE] * scale).astype(BF16)
    qr = (_rope(q[:, N_HEADS * QK_NOPE:], cq, sq) * scale).astype(BF16)

    kv = _dot(_rms(kva, gkv_ref[...]).astype(BF16), wkvb_ref[...]).astype(BF16)
    krr = _rope(kr, c1, s1)[:, :QK_ROPE].astype(BF16)
    for hd in range(N_HEADS):
        q_ref[0, hd, :, 0:QK_NOPE] = qn[:, hd * QK_NOPE:(hd + 1) * QK_NOPE]
        q_ref[0, hd, :, QK_NOPE:QK_DIM] = qr[:, hd * QK_ROPE:(hd + 1) * QK_ROPE]
        k_ref[0, hd, :, 0:QK_NOPE] = kv[:, hd * 2 * HEAD:hd * 2 * HEAD + QK_NOPE]
        k_ref[0, hd, :, QK_NOPE:QK_DIM] = krr
        v_ref[0, hd, :, :] = kv[:, hd * 2 * HEAD + QK_NOPE:(hd + 1) * 2 * HEAD]


def _in_lat(x2, mod_l, gmix, wqkv, wconv, gq, wqb, gkv, wkvb, cq, sq, batch, seq):
    tm = TM_IN
    tpb = seq // tm
    const = lambda b, t: (0, 0)
    one = pl.Buffered(1)
    tok = lambda b, t: (b * tpb + t, 0)
    hd4 = lambda b, t: (b, 0, t, 0)
    return pl.pallas_call(
        _in_lat_kernel,
        grid=(batch, tpb),
        in_specs=[
            pl.BlockSpec((tm, D_MODEL), tok),
            pl.BlockSpec((1, 6, D_MODEL), lambda b, t: (b, 0, 0)),
            pl.BlockSpec((1, D_MODEL), const),
            pl.BlockSpec(wqkv.shape, const, pipeline_mode=one),
            pl.BlockSpec(wconv.shape, const, pipeline_mode=one),
            pl.BlockSpec((1, Q_RANK), const),
            pl.BlockSpec(wqb.shape, const, pipeline_mode=one),
            pl.BlockSpec((1, KV_RANK), const),
            pl.BlockSpec(wkvb.shape, const, pipeline_mode=one),
            pl.BlockSpec((tm, LANES), lambda b, t: (t, 0)),
            pl.BlockSpec((tm, LANES), lambda b, t: (t, 0)),
        ],
        out_specs=[
            pl.BlockSpec((1, N_HEADS, tm, QK_DIM), hd4),
            pl.BlockSpec((1, N_HEADS, tm, QK_DIM), hd4),
            pl.BlockSpec((1, N_HEADS, tm, HEAD), hd4),
            pl.BlockSpec((tm, CONV_W), tok),
            pl.BlockSpec((tm, CONV_W), tok),
        ],
        out_shape=[
            jax.ShapeDtypeStruct((batch, N_HEADS, seq, QK_DIM), BF16),
            jax.ShapeDtypeStruct((batch, N_HEADS, seq, QK_DIM), BF16),
            jax.ShapeDtypeStruct((batch, N_HEADS, seq, HEAD), BF16),
            jax.ShapeDtypeStruct((batch * seq, CONV_W), F32),
            jax.ShapeDtypeStruct((batch * seq, CONV_W), F32),
        ],
        compiler_params=pltpu.CompilerParams(
            dimension_semantics=("parallel", "parallel"), vmem_limit_bytes=VMEM_LIMIT),
        name="in_lat",
    )(x2, mod_l, gmix, wqkv, wconv, gq, wqb, gkv, wkvb, cq, sq)


def _in_ctx_kernel(x_ref, mod_ref, gmix_ref, wkv_ref, gkv_ref, wkvb_ref, k_ref, v_ref):
    sh = mod_ref[0, 0:1, :]
    sc = mod_ref[0, 1:2, :]
    h = (_rms(x_ref[0], gmix_ref[...]) * (1.0 + sc) + sh).astype(BF16)
    p1 = _dot(h, wkv_ref[:, Q_RANK:Q_RANK + KV_RANK + LANES])
    kva = p1[:, :KV_RANK]
    kr = p1[:, KV_RANK:KV_RANK + QK_ROPE].astype(BF16)
    kv = _dot(_rms(kva, gkv_ref[...]).astype(BF16), wkvb_ref[...]).astype(BF16)
    for hd in range(N_HEADS):
        k_ref[0, hd, :, 0:QK_NOPE] = kv[:, hd * 2 * HEAD:hd * 2 * HEAD + QK_NOPE]
        k_ref[0, hd, :, QK_NOPE:QK_DIM] = kr
        v_ref[0, hd, :, :] = kv[:, hd * 2 * HEAD + QK_NOPE:(hd + 1) * 2 * HEAD]


def _in_ctx(ctx, mod, gmix, wkv, gkv, wkvb):
    batch, n_ctx, _ = ctx.shape
    const = lambda b: (0, 0)
    return pl.pallas_call(
        _in_ctx_kernel,
        grid=(batch,),
        in_specs=[
            pl.BlockSpec((1, n_ctx, D_MODEL), lambda b: (b, 0, 0)),
            pl.BlockSpec((1, 6, D_MODEL), lambda b: (batch, 0, 0)),
            pl.BlockSpec((1, D_MODEL), const),
            pl.BlockSpec(wkv.shape, const),
            pl.BlockSpec((1, KV_RANK), const),
            pl.BlockSpec(wkvb.shape, const),
        ],
        out_specs=[
            pl.BlockSpec((1, N_HEADS, n_ctx, QK_DIM), lambda b: (b, 0, 0, 0)),
            pl.BlockSpec((1, N_HEADS, n_ctx, HEAD), lambda b: (b, 0, 0, 0)),
        ],
        out_shape=[
            jax.ShapeDtypeStruct((batch, N_HEADS, n_ctx, QK_DIM), BF16),
            jax.ShapeDtypeStruct((batch, N_HEADS, n_ctx, HEAD), BF16),
        ],
        compiler_params=pltpu.CompilerParams(
            dimension_semantics=("parallel",), vmem_limit_bytes=VMEM_LIMIT),
        name="in_ctx",
    )(ctx, mod, gmix, wkv, gkv, wkvb)


UP_LANE_BLKS = N_SUB * 2 * BPS
DN_ROW_BLKS = N_SUB * BPS


def _up_src(c):
    c_in = jnp.minimum(c, UP_LANE_BLKS - 1)
    slab = (c_in % (2 * BPS)) // 2
    blk_in_tile = sum(jnp.where(slab == SLAB_OF_BLOCK[b], b, 0) for b in range(BPS))
    blk = (c_in // (2 * BPS)) * BPS + blk_in_tile
    src = (c_in % 2) * FFN_BLKS + jnp.minimum(blk, FFN_BLKS - 1)
    return src, (c < UP_LANE_BLKS) & (blk < FFN_BLKS)


def _attn_kernel(*refs, n_up, n_dn):
    q_ref, kl_ref, vl_ref, kc_ref, vc_ref = refs[:5]
    wu_refs = refs[5:5 + n_up]
    wd_refs = refs[5 + n_up:5 + n_up + n_dn]
    wo_ref = refs[5 + n_up + n_dn]
    o_ref, wup_ref, wdn_ref, wout_ref = refs[6 + n_up + n_dn:]

    step = ((pl.program_id(0) * pl.num_programs(1) + pl.program_id(1)) * pl.num_programs(2)
            + pl.program_id(2))
    for i in range(n_up):
        _, real = _up_src(step * n_up + i)
        wup_ref[:, i * LANES:(i + 1) * LANES] = jnp.where(real, wu_refs[i][...], 0.0).astype(BF16)
    dn_blk = jnp.minimum(step, DN_ROW_BLKS // n_dn - 1) * n_dn
    for i in range(n_dn):
        wdn_ref[i * LANES:(i + 1) * LANES, :] = jnp.where(dn_blk + i < FFN_BLKS, wd_refs[i][...], 0.0).astype(BF16)
    wout_ref[...] = wo_ref[...].astype(BF16)

    nt = (((1,), (1,)), ((), ()))

    def with_ones(v):
        return jnp.concatenate([v, jnp.ones_like(v)], axis=1)

    for hd in range(ATTN_HEADS):
        vl = with_ones(vl_ref[0, hd])
        vc = with_ones(vc_ref[0, hd])
        for j in range(TQ // TQ_SUB):
            rows = slice(j * TQ_SUB, (j + 1) * TQ_SUB)
            q = q_ref[0, hd, rows, :]
            s_l = lax.dot_general(q, kl_ref[0, hd], nt, preferred_element_type=F32)
            s_c = lax.dot_general(q, kc_ref[0, hd], nt, preferred_element_type=F32)
            m = jnp.maximum(jnp.max(s_l, axis=-1, keepdims=True), jnp.max(s_c, axis=-1, keepdims=True))
            p_l = jnp.exp2(s_l - m).astype(BF16)
            p_c = jnp.exp2(s_c - m).astype(BF16)
            o = _dot(p_l, vl) + _dot(p_c, vc)
            o_ref[0, rows, hd * HEAD:(hd + 1) * HEAD] = o[:, :HEAD] / o[:, HEAD:]


def _attn(q, k_l, v_l, k_c, v_c, w_up, w_down, w_out):
    batch, heads, seq, _ = q.shape
    n_ctx = k_c.shape[2]
    nq = seq // TQ
    hps = ATTN_HEADS
    n_hg = heads // hps
    n_steps = batch * n_hg * nq
    n_up = -(-UP_LANE_BLKS // n_steps)
    n_dn = next(k for k in range(-(-DN_ROW_BLKS // n_steps), DN_ROW_BLKS + 1) if DN_ROW_BLKS % k == 0)
    assert D_MODEL % n_steps == 0 and heads % hps == 0
    wo_rows = D_MODEL // n_steps
    step = lambda b, h, i: (b * n_hg + h) * nq + i
    kv_map = lambda b, h, i: (b, h, 0, 0)
    dn_out = lambda b, h, i: jnp.minimum(step(b, h, i), DN_ROW_BLKS // n_dn - 1)
    up_specs = [pl.BlockSpec((D_MODEL, LANES),
                             functools.partial(lambda b, h, i, k: (0, _up_src(step(b, h, i) * n_up + k)[0]), k=k))
                for k in range(n_up)]
    dn_specs = [pl.BlockSpec((LANES, D_MODEL),
                             functools.partial(
                                 lambda b, h, i, k: (jnp.minimum(dn_out(b, h, i) * n_dn + k, FFN_BLKS - 1), 0), k=k))
                for k in range(n_dn)]
    return pl.pallas_call(
        functools.partial(_attn_kernel, n_up=n_up, n_dn=n_dn),
        grid=(batch, n_hg, nq),
        in_specs=[
            pl.BlockSpec((1, hps, TQ, QK_DIM), lambda b, h, i: (b, h, i, 0)),
            pl.BlockSpec((1, hps, seq, QK_DIM), kv_map),
            pl.BlockSpec((1, hps, seq, HEAD), kv_map),
            pl.BlockSpec((1, hps, n_ctx, QK_DIM), kv_map),
            pl.BlockSpec((1, hps, n_ctx, HEAD), kv_map),
            *up_specs,
            *dn_specs,
            pl.BlockSpec((wo_rows, D_MODEL), lambda b, h, i: (step(b, h, i), 0)),
        ],
        out_specs=[
            pl.BlockSpec((1, TQ, hps * HEAD), lambda b, h, i: (b, i, h)),
            pl.BlockSpec((D_MODEL, n_up * LANES), lambda b, h, i: (0, step(b, h, i))),
            pl.BlockSpec((n_dn * LANES, D_MODEL), lambda b, h, i: (dn_out(b, h, i), 0)),
            pl.BlockSpec((wo_rows, D_MODEL), lambda b, h, i: (step(b, h, i), 0)),
        ],
        out_shape=[
            jax.ShapeDtypeStruct((batch, seq, heads * HEAD), F32),
            jax.ShapeDtypeStruct((D_MODEL, n_steps * n_up * LANES), BF16),
            jax.ShapeDtypeStruct((FFN_PAD, D_MODEL), BF16),
            jax.ShapeDtypeStruct((D_MODEL, D_MODEL), BF16),
        ],
        compiler_params=pltpu.CompilerParams(
            dimension_semantics=("arbitrary", "arbitrary", "arbitrary"),
            vmem_limit_bytes=VMEM_LIMIT),
        name="attn",
    )(q, k_l, v_l, k_c, v_c, *([w_up] * n_up), *([w_down] * n_dn), w_out)


def _out_kernel(att_ref, cb_ref, z_ref, zp_ref, zn_ref, cw_ref, cbias_ref, gmo_ref, wout_ref,
                x_ref, mod_ref, gffn_ref, x1_ref, hf_ref, mix_ref):
    t = pl.program_id(1)
    tm = z_ref.shape[0]
    z = z_ref[...]
    row = lax.broadcasted_iota(jnp.int32, z.shape, 0)
    z_before = jnp.where(t == 0, 0.0, zp_ref[0, SUBLANES - 1:SUBLANES, :])
    z_after = jnp.where(t == pl.num_programs(1) - 1, 0.0, zn_ref[0, 0:1, :])
    z_prev = jnp.where(row == 0, z_before, pltpu.roll(z, 1, 0))
    z_next = jnp.where(row == tm - 1, z_after, pltpu.roll(z, tm - 1, 0))
    conv = cb_ref[...] * (z_prev * cw_ref[0:1, :] + z * cw_ref[1:2, :] + z_next * cw_ref[2:3, :]
                          + cbias_ref[...])

    n_att = att_ref.shape[1] // HEAD
    gt_a = mod_ref[0, 2:3, :]
    ffn_gain = gffn_ref[...] * (1.0 + mod_ref[0, 4:5, :])
    ffn_shift = mod_ref[0, 3:4, :]
    for r in range(0, tm, NORM_ROWS):
        rows = slice(r, r + NORM_ROWS)
        for g in range(D_MODEL // HEAD):
            cols = slice(g * HEAD, (g + 1) * HEAD)
            blk = att_ref[rows, cols] if g < n_att else conv[rows, (g - n_att) * HEAD:(g - n_att + 1) * HEAD]
            mix_ref[rows, cols] = _rms(blk, gmo_ref[:, cols]).astype(BF16)
    x1 = x_ref[...] + gt_a * _dot(mix_ref[...], wout_ref[...])
    x1_ref[...] = x1
    hf_ref[...] = (_rms(x1, ffn_gain) + ffn_shift).astype(BF16)


def _out(att2, cb, z, cw, cbias, gmo, wout, x2, mod_l, gffn, batch, seq):
    tm = TM_OUT
    tpb = seq // tm
    n8 = batch * seq // SUBLANES
    z3 = z.reshape(n8, SUBLANES, CONV_W)
    const = lambda b, t: (0, 0)
    tok = lambda b, t: (b * tpb + t, 0)
    return pl.pallas_call(
        _out_kernel,
        grid=(batch, tpb),
        in_specs=[
            pl.BlockSpec((tm, att2.shape[1]), tok),
            pl.BlockSpec((tm, CONV_W), tok),
            pl.BlockSpec((tm, CONV_W), tok),
            pl.BlockSpec((1, SUBLANES, CONV_W),
                         lambda b, t: (jnp.maximum((b * tpb + t) * (tm // SUBLANES) - 1, 0), 0, 0)),
            pl.BlockSpec((1, SUBLANES, CONV_W),
                         lambda b, t: (jnp.minimum((b * tpb + t + 1) * (tm // SUBLANES), n8 - 1), 0, 0)),
            pl.BlockSpec((3, CONV_W), const),
            pl.BlockSpec((1, CONV_W), const),
            pl.BlockSpec((1, D_MODEL), const),
            pl.BlockSpec(wout.shape, const, pipeline_mode=pl.Buffered(1)),
            pl.BlockSpec((tm, D_MODEL), tok),
            pl.BlockSpec((1, 6, D_MODEL), lambda b, t: (b, 0, 0)),
            pl.BlockSpec((1, D_MODEL), const),
        ],
        out_specs=[
            pl.BlockSpec((tm, D_MODEL), tok),
            pl.BlockSpec((tm, D_MODEL), tok),
        ],
        out_shape=[
            jax.ShapeDtypeStruct((batch * seq, D_MODEL), F32),
            jax.ShapeDtypeStruct((batch * seq, D_MODEL), BF16),
        ],
        scratch_shapes=[pltpu.VMEM((tm, D_MODEL), BF16)],
        compiler_params=pltpu.CompilerParams(
            dimension_semantics=("parallel", "parallel"), vmem_limit_bytes=VMEM_LIMIT),
        name="out_proj",
    )(att2, cb, z, z3, z3, cw, cbias, gmo, wout, x2, mod_l, gffn)


def _ffn_kernel(h_ref, hp_ref, hn_ref, wup_ref, cw_ref, cbias_ref, wdn_ref, x1_ref, mod_ref, gfin_ref,
                o_ref, hbuf_ref, u_ref):
    t = pl.program_id(1)
    f = pl.program_id(2)
    nf = pl.num_programs(2)
    tm = h_ref.shape[0]

    def step(first, last):
        if first:
            hbuf_ref[HALO:HALO + tm, :] = h_ref[...]
            hbuf_ref[0:HALO, :] = jnp.where(t == 0, jnp.zeros_like(hp_ref[0]), hp_ref[0])
            hbuf_ref[HALO + tm:, :] = jnp.where(t == pl.num_programs(1) - 1, jnp.zeros_like(hn_ref[0]), hn_ref[0])
        u_ref[...] = _dot(hbuf_ref[...], wup_ref[...])
        cw = cw_ref[f]
        y = (u_ref[pl.ds(HALO - 1, tm), :] * cw[0:1, :] + u_ref[pl.ds(HALO, tm), :] * cw[1:2, :]
             + u_ref[pl.ds(HALO + 1, tm), :] * cw[2:3, :] + cbias_ref[f])
        act = jnp.concatenate(
            [y[:, 2 * j * LANES:(2 * j + 1) * LANES] * _silu(y[:, (2 * j + 1) * LANES:(2 * j + 2) * LANES])
             for j in SLAB_OF_BLOCK], axis=1).astype(BF16)
        half = TF_SUB // 2
        for rows in [slice(r, r + tm // DOWN_ROW_PARTS) for r in range(0, tm, tm // DOWN_ROW_PARTS)]:
            d = _dot(act[rows, :half], wdn_ref[:half, :]) + _dot(act[rows, half:], wdn_ref[half:, :])
            acc = d if first else o_ref[rows, :] + d
            if last:
                acc = _rms(x1_ref[rows, :] + mod_ref[0, 5:6, :] * acc, gfin_ref[...])
            o_ref[rows, :] = acc

    pl.when(f == 0)(lambda: step(True, False))
    pl.when((f > 0) & (f < nf - 1))(lambda: step(False, False))
    pl.when(f == nf - 1)(lambda: step(False, True))


def _ffn(hf, wup_t, cw_t, cbias_t, wdn, x1, mod_l, gfin, batch, seq):
    tm = TM_FFN
    tpb = seq // tm
    nf = N_SUB
    tf = TF_SUB
    n_h = batch * seq // HALO
    hf3 = hf.reshape(n_h, HALO, D_MODEL)
    tok = lambda b, t, f: (b * tpb + t, 0)
    return pl.pallas_call(
        _ffn_kernel,
        grid=(batch, tpb, nf),
        in_specs=[
            pl.BlockSpec((tm, D_MODEL), tok),
            pl.BlockSpec((1, HALO, D_MODEL),
                         lambda b, t, f: (jnp.maximum((b * tpb + t) * (tm // HALO) - 1, 0), 0, 0)),
            pl.BlockSpec((1, HALO, D_MODEL),
                         lambda b, t, f: (jnp.minimum((b * tpb + t + 1) * (tm // HALO), n_h - 1), 0, 0)),
            pl.BlockSpec((D_MODEL, 2 * TF_SUB), lambda b, t, f: (0, f)),
            pl.BlockSpec((nf, 3, 2 * TF_SUB), lambda b, t, f: (0, 0, 0)),
            pl.BlockSpec((nf, 1, 2 * TF_SUB), lambda b, t, f: (0, 0, 0)),
            pl.BlockSpec((tf, D_MODEL), lambda b, t, f: (f, 0)),
            pl.BlockSpec((tm, D_MODEL), tok),
            pl.BlockSpec((1, 6, D_MODEL), lambda b, t, f: (b, 0, 0)),
            pl.BlockSpec((1, D_MODEL), lambda b, t, f: (0, 0)),
        ],
        out_specs=pl.BlockSpec((tm, D_MODEL), tok),
        out_shape=jax.ShapeDtypeStruct((batch * seq, D_MODEL), F32),
        scratch_shapes=[
            pltpu.VMEM((tm + 2 * HALO, D_MODEL), BF16),
            pltpu.VMEM((tm + 2 * HALO, 2 * TF_SUB), F32),
        ],
        compiler_params=pltpu.CompilerParams(
            dimension_semantics=("parallel", "parallel", "arbitrary"), vmem_limit_bytes=VMEM_LIMIT),
        name="ffn",
    )(hf, hf3, hf3, wup_t, cw_t, cbias_t, wdn, x1, mod_l, gfin)


def _rope_tables(seq):
    rows = seq // GRID_W
    row = np.repeat(np.arange(rows), GRID_W).astype(np.float64)
    col = np.tile(np.arange(GRID_W), rows).astype(np.float64)
    axis_dim = QK_ROPE // 2
    inv = ROPE_BASE ** (-np.arange(0, axis_dim, 2, dtype=np.float64) / axis_dim)
    ang = np.concatenate([row[:, None] * inv, col[:, None] * inv], axis=-1)
    cos, sin = np.cos(ang), np.sin(ang)
    c = np.repeat(cos, 2, axis=-1)
    s = np.stack([-sin, sin], axis=-1).reshape(seq, QK_ROPE)
    reps = (1, LANES // QK_ROPE)
    return jnp.asarray(np.tile(c, reps), F32), jnp.asarray(np.tile(s, reps), F32)


def _tile_ffn_cols(a):
    pad = ((0, 0), (0, FFN_PAD - FFN_DIM))
    val = jnp.pad(a[:, :FFN_DIM], pad).reshape(a.shape[0], N_SUB, BPS, 1, LANES)
    gate = jnp.pad(a[:, FFN_DIM:], pad).reshape(a.shape[0], N_SUB, BPS, 1, LANES)
    block_of_slab = [SLAB_OF_BLOCK.index(p) for p in range(BPS)]
    both = jnp.concatenate([val, gate], axis=3)[:, :, block_of_slab].reshape(a.shape[0], N_SUB, 2 * TF_SUB)
    return jnp.moveaxis(both, 1, 0)


def kernel(x, c, ctx, c_ctx, w_ada, b_ada, g_mix_norm, w_in, g_q_a, w_q_b, g_kv_a, w_kv_b, conv_w, conv_b, g_mix_out, w_out, g_ffn_norm, w_up, ffn_conv_w, ffn_conv_b, w_down, g_final):
    batch, seq, d = x.shape
    assert d == D_MODEL and w_ada.shape[0] == 1
    x2 = x.reshape(batch * seq, d)

    mod, wconv, wqkv, wqb, wkvb = _ada(c, c_ctx[None], w_ada[0], b_ada, w_in[0].T, w_q_b[0], w_kv_b[0])
    mod = mod.reshape(mod.shape[0], 6, d)

    cw_t = _tile_ffn_cols(ffn_conv_w[0])
    cbias_t = _tile_ffn_cols(ffn_conv_b)

    cq, sq = _rope_tables(seq)

    q, k_l, v_l, cb, z = _in_lat(x2, mod, g_mix_norm, wqkv, wconv, g_q_a, wqb, g_kv_a, wkvb,
                                 cq, sq, batch, seq)
    k_c, v_c = _in_ctx(ctx, mod, g_mix_norm, wqkv, g_kv_a, wkvb)
    att, wup_t, wdn, wout = _attn(q, k_l, v_l, k_c, v_c, w_up[0], w_down[0], w_out[0])
    x1, hf = _out(att.reshape(batch * seq, N_HEADS * HEAD), cb, z, conv_w[0], conv_b, g_mix_out,
                  wout, x2, mod, g_ffn_norm, batch, seq)
    y = _ffn(hf, wup_t, cw_t, cbias_t, wdn, x1, mod, g_final[None], batch, seq)
    return y.reshape(batch, seq, d)
```

```python
import functools

import jax
import jax.numpy as jnp
import numpy as np
from jax import lax
from jax.experimental import pallas as pl
from jax.experimental.pallas import tpu as pltpu

F32 = jnp.float32
BF16 = jnp.bfloat16

D_MODEL = 2048
N_HEADS = 8
HEAD = 128
QK_NOPE = 128
QK_ROPE = 64
QK_DIM = QK_NOPE + QK_ROPE
Q_RANK = 512
KV_RANK = 256
CONV_W = 1024
GRID_W = 64
ROPE_BASE = 10000.0
EPS = 1e-6
LOG2E = 1.4426950408889634
FFN_DIM = 5504

V7X_VMEM_BYTES = 64 * 1024 * 1024
VMEM_LIMIT = V7X_VMEM_BYTES - 8 * 1024 * 1024

TM_IN = 512
TQ = 2048
TQ_SUB = 256
ATTN_HEADS = 2
TM_OUT = 512
NORM_ROWS = 64
TM_FFN = 512
LANES = 128
SUBLANES = 8
TF_SUB = 512
HALO = SUBLANES
BPS = TF_SUB // LANES
FFN_BLKS = FFN_DIM // LANES
N_SUB = (FFN_DIM + TF_SUB - 1) // TF_SUB
FFN_PAD = N_SUB * TF_SUB
DOWN_ROW_PARTS = 2
SLAB_OF_BLOCK = (0, 2, 1, 3)


def _rms(x, g):
    return x * lax.rsqrt(jnp.mean(x * x, axis=-1, keepdims=True) + EPS) * g


def _silu(x):
    h = 0.5 * x
    return h * (1.0 + jnp.tanh(h))


def _dot(a, b):
    return jnp.dot(a, b, preferred_element_type=F32)


def _rope(x, c, s):
    w = x.shape[-1]
    lane = lax.broadcasted_iota(jnp.int32, x.shape, 1)
    from_right = pltpu.roll(x, w - 1, 1)
    from_left = pltpu.roll(x, 1, 1)
    partner = jnp.where(lane % 2 == 0, from_right, from_left)
    return x * c + partner * s


N_QKV = Q_RANK + KV_RANK + QK_ROPE
PREP_ROWS = 512
N_QKV_BLKS = -(-N_QKV // PREP_ROWS)
N_CONV_BLKS = 3 * CONV_W // PREP_ROWS


def _ada_kernel(c_ref, cctx_ref, w_ref, b_ref, wi_ref, wq_ref, wkv_ref, ctx_ref, gmix_ref, gkv_ref,
                o_ref, conv_ref, qkv_ref, wqb_ref, wkvb_ref, kc_ref, vc_ref, lhs_ref, prev_ref, modc_ref):
    j = pl.program_id(0)
    nb = c_ref.shape[0]
    tn = w_ref.shape[1]
    per_chunk = D_MODEL // tn
    lhs_ref[...] = jnp.zeros_like(lhs_ref)
    lhs_ref[0:nb, :] = _silu(c_ref[...])
    lhs_ref[nb:nb + 1, :] = _silu(cctx_ref[...])
    mod_blk = _dot(lhs_ref[...].astype(BF16), w_ref[...].astype(BF16)) + b_ref[...]
    o_ref[...] = mod_blk
    for step in range(2 * per_chunk):
        @pl.when(j == step)
        def _(step=step):
            lo = (step % per_chunk) * tn
            modc_ref[step // per_chunk:step // per_chunk + 1, lo:lo + tn] = mod_blk[nb:nb + 1, :]

    @pl.when(j < N_QKV_BLKS)
    def _():
        qkv_ref[...] = wi_ref[...].T.astype(BF16)

    @pl.when((j >= N_QKV_BLKS) & (j < N_QKV_BLKS + N_CONV_BLKS))
    def _():
        off = N_QKV % PREP_ROWS
        conv_ref[...] = jnp.concatenate([prev_ref[off:, :], wi_ref[:off, :]], axis=0).T.astype(BF16)

    @pl.when(j < N_QKV_BLKS + N_CONV_BLKS - 1)
    def _():
        prev_ref[...] = wi_ref[...]

    @pl.when(j == N_QKV_BLKS + N_CONV_BLKS)
    def _():
        w = wq_ref[...]
        nope = [w[:, h * QK_DIM:h * QK_DIM + QK_NOPE] for h in range(N_HEADS)]
        rope = [w[:, h * QK_DIM + QK_NOPE:(h + 1) * QK_DIM] for h in range(N_HEADS)]
        wqb_ref[...] = jnp.concatenate(nope + rope, axis=1).astype(BF16)

    @pl.when(j == 0)
    def _():
        wkvb_ref[...] = wkv_ref[...].astype(BF16)

    ctx0 = 2 * per_chunk

    @pl.when((j >= ctx0) & (j < ctx0 + nb))
    def _():
        sh = modc_ref[0:1, :]
        sc = modc_ref[1:2, :]
        h = (_rms(ctx_ref[0], gmix_ref[...]) * (1.0 + sc) + sh).astype(BF16)
        p1 = _dot(h, qkv_ref[:, 0:KV_RANK + LANES])
        kva = p1[:, :KV_RANK]
        kr = p1[:, KV_RANK:KV_RANK + QK_ROPE].astype(BF16)
        kv = _dot(_rms(kva, gkv_ref[...]).astype(BF16), wkvb_ref[...]).astype(BF16)
        for hd in range(N_HEADS):
            kc_ref[0, hd, :, 0:QK_NOPE] = kv[:, hd * 2 * HEAD:hd * 2 * HEAD + QK_NOPE]
            kc_ref[0, hd, :, QK_NOPE:QK_DIM] = kr
            vc_ref[0, hd, :, :] = kv[:, hd * 2 * HEAD + QK_NOPE:(hd + 1) * 2 * HEAD]


def _ada(c, c_ctx, w, b, w_in_t, w_q_b, w_kv_b, ctx, gmix, gkv):
    n = w.shape[1]
    tn = 1024
    n_blks = N_QKV_BLKS + N_CONV_BLKS
    batch, n_ctx, _ = ctx.shape
    ctx0 = 2 * (D_MODEL // tn)
    assert n // tn >= max(n_blks + 1, ctx0 + batch) and c.shape[0] == batch and batch + 1 <= SUBLANES
    assert N_QKV_BLKS == N_QKV // PREP_ROWS + 1 and Q_RANK == PREP_ROWS * (N_QKV_BLKS - 1) and ctx0 >= N_QKV_BLKS
    ctx_blk = lambda j: jnp.clip(j - ctx0, 0, batch - 1)
    conv_blk = lambda j: jnp.clip(j - N_QKV_BLKS, 0, N_CONV_BLKS - 1)
    const = lambda j: (0, 0)
    return pl.pallas_call(
        _ada_kernel,
        grid=(n // tn,),
        in_specs=[
            pl.BlockSpec(c.shape, const),
            pl.BlockSpec((1, D_MODEL), const),
            pl.BlockSpec((D_MODEL, tn), lambda j: (0, j)),
            pl.BlockSpec((1, tn), lambda j: (0, j)),
            pl.BlockSpec((PREP_ROWS, D_MODEL), lambda j: (jnp.minimum(j, n_blks - 1), 0)),
            pl.BlockSpec(w_q_b.shape, const),
            pl.BlockSpec(w_kv_b.shape, const),
            pl.BlockSpec((1, n_ctx, D_MODEL), lambda j: (ctx_blk(j), 0, 0)),
            pl.BlockSpec((1, D_MODEL), const),
            pl.BlockSpec((1, KV_RANK), const),
        ],
        out_specs=[
            pl.BlockSpec((SUBLANES, tn), lambda j: (0, j)),
            pl.BlockSpec((D_MODEL, PREP_ROWS), lambda j: (0, conv_blk(j))),
            pl.BlockSpec((D_MODEL, PREP_ROWS), lambda j: (0, jnp.minimum(j, N_QKV_BLKS - 1))),
            pl.BlockSpec(w_q_b.shape, const),
            pl.BlockSpec(w_kv_b.shape, const),
            pl.BlockSpec((1, N_HEADS, n_ctx, QK_DIM), lambda j: (ctx_blk(j), 0, 0, 0)),
            pl.BlockSpec((1, N_HEADS, n_ctx, HEAD), lambda j: (ctx_blk(j), 0, 0, 0)),
        ],
        out_shape=[
            jax.ShapeDtypeStruct((SUBLANES, n), F32),
            jax.ShapeDtypeStruct((D_MODEL, N_CONV_BLKS * PREP_ROWS), BF16),
            jax.ShapeDtypeStruct((D_MODEL, N_QKV_BLKS * PREP_ROWS), BF16),
            jax.ShapeDtypeStruct(w_q_b.shape, BF16),
            jax.ShapeDtypeStruct(w_kv_b.shape, BF16),
            jax.ShapeDtypeStruct((batch, N_HEADS, n_ctx, QK_DIM), BF16),
            jax.ShapeDtypeStruct((batch, N_HEADS, n_ctx, HEAD), BF16),
        ],
        scratch_shapes=[pltpu.VMEM((SUBLANES, D_MODEL), F32), pltpu.VMEM((PREP_ROWS, D_MODEL), F32),
                        pltpu.VMEM((2, D_MODEL), F32)],
        compiler_params=pltpu.CompilerParams(
            dimension_semantics=("arbitrary",), vmem_limit_bytes=VMEM_LIMIT),
        name="ada",
    )(c, c_ctx, w, b, w_in_t, w_q_b, w_kv_b, ctx, gmix, gkv)


def _in_lat_kernel(x_ref, mod_ref, gmix_ref, wqkv_ref, wconv_ref, gq_ref, wqb_ref, gkv_ref,
                   wkvb_ref, cq_ref, sq_ref, q_ref, k_ref, v_ref, cb_ref, z_ref):
    sh = mod_ref[0, 0:1, :]
    sc = mod_ref[0, 1:2, :]
    h = (_rms(x_ref[...], gmix_ref[...]) * (1.0 + sc) + sh).astype(BF16)

    p2 = _dot(h, wconv_ref[...])
    cb_ref[...] = p2[:, :CONV_W]
    z_ref[...] = p2[:, CONV_W:2 * CONV_W] * p2[:, 2 * CONV_W:]

    p1 = _dot(h, wqkv_ref[...])
    qa = p1[:, :Q_RANK]
    kva = p1[:, Q_RANK:Q_RANK + KV_RANK]
    kr = p1[:, Q_RANK + KV_RANK:Q_RANK + KV_RANK + LANES]

    c1 = cq_ref[...]
    s1 = sq_ref[...]
    reps = N_HEADS * QK_ROPE // LANES
    cq = jnp.concatenate([c1] * reps, axis=1)
    sq = jnp.concatenate([s1] * reps, axis=1)
    scale = QK_DIM ** -0.5 * LOG2E
    q = _dot(_rms(qa, gq_ref[...]).astype(BF16), wqb_ref[...])
    qn = (q[:, :N_HEADS * QK_NOPE] * scale).astype(BF16)
    qr = (_rope(q[:, N_HEADS * QK_NOPE:], cq, sq) * scale).astype(BF16)

    kv = _dot(_rms(kva, gkv_ref[...]).astype(BF16), wkvb_ref[...]).astype(BF16)
    krr = _rope(kr, c1, s1)[:, :QK_ROPE].astype(BF16)
    for hd in range(N_HEADS):
        q_ref[0, hd, :, 0:QK_NOPE] = qn[:, hd * QK_NOPE:(hd + 1) * QK_NOPE]
        q_ref[0, hd, :, QK_NOPE:QK_DIM] = qr[:, hd * QK_ROPE:(hd + 1) * QK_ROPE]
        k_ref[0, hd, :, 0:QK_NOPE] = kv[:, hd * 2 * HEAD:hd * 2 * HEAD + QK_NOPE]
        k_ref[0, hd, :, QK_NOPE:QK_DIM] = krr
        v_ref[0, hd, :, :] = kv[:, hd * 2 * HEAD + QK_NOPE:(hd + 1) * 2 * HEAD]


def _in_lat(x2, mod_l, gmix, wqkv, wconv, gq, wqb, gkv, wkvb, cq, sq, batch, seq):
    tm = TM_IN
    tpb = seq // tm
    const = lambda b, t: (0, 0)
    one = pl.Buffered(1)
    tok = lambda b, t: (b * tpb + t, 0)
    hd4 = lambda b, t: (b, 0, t, 0)
    return pl.pallas_call(
        _in_lat_kernel,
        grid=(batch, tpb),
        in_specs=[
            pl.BlockSpec((tm, D_MODEL), tok),
            pl.BlockSpec((1, 6, D_MODEL), lambda b, t: (b, 0, 0)),
            pl.BlockSpec((1, D_MODEL), const),
            pl.BlockSpec(wqkv.shape, const, pipeline_mode=one),
            pl.BlockSpec(wconv.shape, const, pipeline_mode=one),
            pl.BlockSpec((1, Q_RANK), const),
            pl.BlockSpec(wqb.shape, const, pipeline_mode=one),
            pl.BlockSpec((1, KV_RANK), const),
            pl.BlockSpec(wkvb.shape, const, pipeline_mode=one),
            pl.BlockSpec((tm, LANES), lambda b, t: (t, 0)),
            pl.BlockSpec((tm, LANES), lambda b, t: (t, 0)),
        ],
        out_specs=[
            pl.BlockSpec((1, N_HEADS, tm, QK_DIM), hd4),
            pl.BlockSpec((1, N_HEADS, tm, QK_DIM), hd4),
            pl.BlockSpec((1, N_HEADS, tm, HEAD), hd4),
            pl.BlockSpec((tm, CONV_W), tok),
            pl.BlockSpec((tm, CONV_W), tok),
        ],
        out_shape=[
            jax.ShapeDtypeStruct((batch, N_HEADS, seq, QK_DIM), BF16),
            jax.ShapeDtypeStruct((batch, N_HEADS, seq, QK_DIM), BF16),
            jax.ShapeDtypeStruct((batch, N_HEADS, seq, HEAD), BF16),
            jax.ShapeDtypeStruct((batch * seq, CONV_W), F32),
            jax.ShapeDtypeStruct((batch * seq, CONV_W), F32),
        ],
        compiler_params=pltpu.CompilerParams(
            dimension_semantics=("parallel", "parallel"), vmem_limit_bytes=VMEM_LIMIT),
        name="in_lat",
    )(x2, mod_l, gmix, wqkv, wconv, gq, wqb, gkv, wkvb, cq, sq)


UP_LANE_BLKS = N_SUB * 2 * BPS
DN_ROW_BLKS = N_SUB * BPS


def _up_src(c):
    c_in = jnp.minimum(c, UP_LANE_BLKS - 1)
    slab = (c_in % (2 * BPS)) // 2
    blk_in_tile = sum(jnp.where(slab == SLAB_OF_BLOCK[b], b, 0) for b in range(BPS))
    blk = (c_in // (2 * BPS)) * BPS + blk_in_tile
    src = (c_in % 2) * FFN_BLKS + jnp.minimum(blk, FFN_BLKS - 1)
    return src, (c < UP_LANE_BLKS) & (blk < FFN_BLKS)


def _attn_kernel(*refs, n_up, n_dn):
    q_ref, kl_ref, vl_ref, kc_ref, vc_ref = refs[:5]
    wu_refs = refs[5:5 + n_up]
    wd_refs = refs[5 + n_up:5 + n_up + n_dn]
    wo_ref = refs[5 + n_up + n_dn]
    o_ref, wup_ref, wdn_ref, wout_ref = refs[6 + n_up + n_dn:]

    step = ((pl.program_id(0) * pl.num_programs(1) + pl.program_id(1)) * pl.num_programs(2)
            + pl.program_id(2))
    for i in range(n_up):
        _, real = _up_src(step * n_up + i)
        wup_ref[:, i * LANES:(i + 1) * LANES] = jnp.where(real, wu_refs[i][...], 0.0).astype(BF16)
    dn_blk = jnp.minimum(step, DN_ROW_BLKS // n_dn - 1) * n_dn
    for i in range(n_dn):
        wdn_ref[i * LANES:(i + 1) * LANES, :] = jnp.where(dn_blk + i < FFN_BLKS, wd_refs[i][...], 0.0).astype(BF16)
    wout_ref[...] = wo_ref[...].astype(BF16)

    nt = (((1,), (1,)), ((), ()))

    def with_ones(v):
        return jnp.concatenate([v, jnp.ones_like(v)], axis=1)

    for hd in range(ATTN_HEADS):
        vl = with_ones(vl_ref[0, hd])
        vc = with_ones(vc_ref[0, hd])
        for j in range(TQ // TQ_SUB):
            rows = slice(j * TQ_SUB, (j + 1) * TQ_SUB)
            q = q_ref[0, hd, rows, :]
            s_l = lax.dot_general(q, kl_ref[0, hd], nt, preferred_element_type=F32)
            s_c = lax.dot_general(q, kc_ref[0, hd], nt, preferred_element_type=F32)
            m = jnp.maximum(jnp.max(s_l, axis=-1, keepdims=True), jnp.max(s_c, axis=-1, keepdims=True))
            p_l = jnp.exp2(s_l - m).astype(BF16)
            p_c = jnp.exp2(s_c - m).astype(BF16)
            o = _dot(p_l, vl) + _dot(p_c, vc)
            o_ref[0, rows, hd * HEAD:(hd + 1) * HEAD] = o[:, :HEAD] / o[:, HEAD:]


def _attn(q, k_l, v_l, k_c, v_c, w_up, w_down, w_out):
    batch, heads, seq, _ = q.shape
    n_ctx = k_c.shape[2]
    nq = seq // TQ
    hps = ATTN_HEADS
    n_hg = heads // hps
    n_steps = batch * n_hg * nq
    n_up = -(-UP_LANE_BLKS // n_steps)
    n_dn = next(k for k in range(-(-DN_ROW_BLKS // n_steps), DN_ROW_BLKS + 1) if DN_ROW_BLKS % k == 0)
    assert D_MODEL % n_steps == 0 and heads % hps == 0
    wo_rows = D_MODEL // n_steps
    step = lambda b, h, i: (b * n_hg + h) * nq + i
    kv_map = lambda b, h, i: (b, h, 0, 0)
    dn_out = lambda b, h, i: jnp.minimum(step(b, h, i), DN_ROW_BLKS // n_dn - 1)
    up_specs = [pl.BlockSpec((D_MODEL, LANES),
                             functools.partial(lambda b, h, i, k: (0, _up_src(step(b, h, i) * n_up + k)[0]), k=k))
                for k in range(n_up)]
    dn_specs = [pl.BlockSpec((LANES, D_MODEL),
                             functools.partial(
                                 lambda b, h, i, k: (jnp.minimum(dn_out(b, h, i) * n_dn + k, FFN_BLKS - 1), 0), k=k))
                for k in range(n_dn)]
    return pl.pallas_call(
        functools.partial(_attn_kernel, n_up=n_up, n_dn=n_dn),
        grid=(batch, n_hg, nq),
        in_specs=[
            pl.BlockSpec((1, hps, TQ, QK_DIM), lambda b, h, i: (b, h, i, 0)),
            pl.BlockSpec((1, hps, seq, QK_DIM), kv_map),
            pl.BlockSpec((1, hps, seq, HEAD), kv_map),
            pl.BlockSpec((1, hps, n_ctx, QK_DIM), kv_map),
            pl.BlockSpec((1, hps, n_ctx, HEAD), kv_map),
            *up_specs,
            *dn_specs,
            pl.BlockSpec((wo_rows, D_MODEL), lambda b, h, i: (step(b, h, i), 0)),
        ],
        out_specs=[
            pl.BlockSpec((1, TQ, hps * HEAD), lambda b, h, i: (b, i, h)),
            pl.BlockSpec((D_MODEL, n_up * LANES), lambda b, h, i: (0, step(b, h, i))),
            pl.BlockSpec((n_dn * LANES, D_MODEL), lambda b, h, i: (dn_out(b, h, i), 0)),
            pl.BlockSpec((wo_rows, D_MODEL), lambda b, h, i: (step(b, h, i), 0)),
        ],
        out_shape=[
            jax.ShapeDtypeStruct((batch, seq, heads * HEAD), F32),
            jax.ShapeDtypeStruct((D_MODEL, n_steps * n_up * LANES), BF16),
            jax.ShapeDtypeStruct((FFN_PAD, D_MODEL), BF16),
            jax.ShapeDtypeStruct((D_MODEL, D_MODEL), BF16),
        ],
        compiler_params=pltpu.CompilerParams(
            dimension_semantics=("arbitrary", "arbitrary", "arbitrary"),
            vmem_limit_bytes=VMEM_LIMIT),
        name="attn",
    )(q, k_l, v_l, k_c, v_c, *([w_up] * n_up), *([w_down] * n_dn), w_out)


def _out_kernel(att_ref, cb_ref, z_ref, zp_ref, zn_ref, cw_ref, cbias_ref, gmo_ref, wout_ref,
                x_ref, mod_ref, gffn_ref, x1_ref, hf_ref, mix_ref):
    t = pl.program_id(1)
    tm = z_ref.shape[0]
    z = z_ref[...]
    row = lax.broadcasted_iota(jnp.int32, z.shape, 0)
    z_before = jnp.where(t == 0, 0.0, zp_ref[0, SUBLANES - 1:SUBLANES, :])
    z_after = jnp.where(t == pl.num_programs(1) - 1, 0.0, zn_ref[0, 0:1, :])
    z_prev = jnp.where(row == 0, z_before, pltpu.roll(z, 1, 0))
    z_next = jnp.where(row == tm - 1, z_after, pltpu.roll(z, tm - 1, 0))
    conv = cb_ref[...] * (z_prev * cw_ref[0:1, :] + z * cw_ref[1:2, :] + z_next * cw_ref[2:3, :]
                          + cbias_ref[...])

    n_att = att_ref.shape[1] // HEAD
    gt_a = mod_ref[0, 2:3, :]
    ffn_gain = gffn_ref[...] * (1.0 + mod_ref[0, 4:5, :])
    ffn_shift = mod_ref[0, 3:4, :]
    for r in range(0, tm, NORM_ROWS):
        rows = slice(r, r + NORM_ROWS)
        for g in range(D_MODEL // HEAD):
            cols = slice(g * HEAD, (g + 1) * HEAD)
            blk = att_ref[rows, cols] if g < n_att else conv[rows, (g - n_att) * HEAD:(g - n_att + 1) * HEAD]
            mix_ref[rows, cols] = _rms(blk, gmo_ref[:, cols]).astype(BF16)
    x1 = x_ref[...] + gt_a * _dot(mix_ref[...], wout_ref[...])
    x1_ref[...] = x1
    hf_ref[...] = (_rms(x1, ffn_gain) + ffn_shift).astype(BF16)


def _out(att2, cb, z, cw, cbias, gmo, wout, x2, mod_l, gffn, batch, seq):
    tm = TM_OUT
    tpb = seq // tm
    n8 = batch * seq // SUBLANES
    z3 = z.reshape(n8, SUBLANES, CONV_W)
    const = lambda b, t: (0, 0)
    tok = lambda b, t: (b * tpb + t, 0)
    return pl.pallas_call(
        _out_kernel,
        grid=(batch, tpb),
        in_specs=[
            pl.BlockSpec((tm, att2.shape[1]), tok),
            pl.BlockSpec((tm, CONV_W), tok),
            pl.BlockSpec((tm, CONV_W), tok),
            pl.BlockSpec((1, SUBLANES, CONV_W),
                         lambda b, t: (jnp.maximum((b * tpb + t) * (tm // SUBLANES) - 1, 0), 0, 0)),
            pl.BlockSpec((1, SUBLANES, CONV_W),
                         lambda b, t: (jnp.minimum((b * tpb + t + 1) * (tm // SUBLANES), n8 - 1), 0, 0)),
            pl.BlockSpec((3, CONV_W), const),
            pl.BlockSpec((1, CONV_W), const),
            pl.BlockSpec((1, D_MODEL), const),
            pl.BlockSpec(wout.shape, const, pipeline_mode=pl.Buffered(1)),
            pl.BlockSpec((tm, D_MODEL), tok),
            pl.BlockSpec((1, 6, D_MODEL), lambda b, t: (b, 0, 0)),
            pl.BlockSpec((1, D_MODEL), const),
        ],
        out_specs=[
            pl.BlockSpec((tm, D_MODEL), tok),
            pl.BlockSpec((tm, D_MODEL), tok),
        ],
        out_shape=[
            jax.ShapeDtypeStruct((batch * seq, D_MODEL), F32),
            jax.ShapeDtypeStruct((batch * seq, D_MODEL), BF16),
        ],
        scratch_shapes=[pltpu.VMEM((tm, D_MODEL), BF16)],
        compiler_params=pltpu.CompilerParams(
            dimension_semantics=("parallel", "parallel"), vmem_limit_bytes=VMEM_LIMIT),
        name="out_proj",
    )(att2, cb, z, z3, z3, cw, cbias, gmo, wout, x2, mod_l, gffn)


def _ffn_kernel(h_ref, hp_ref, hn_ref, wup_ref, cw_ref, cbias_ref, wdn_ref, x1_ref, mod_ref, gfin_ref,
                o_ref, hbuf_ref, u_ref):
    t = pl.program_id(1)
    f = pl.program_id(2)
    nf = pl.num_programs(2)
    tm = h_ref.shape[0]

    def step(first, last):
        if first:
            hbuf_ref[HALO:HALO + tm, :] = h_ref[...]
            hbuf_ref[0:HALO, :] = jnp.where(t == 0, jnp.zeros_like(hp_ref[0]), hp_ref[0])
            hbuf_ref[HALO + tm:, :] = jnp.where(t == pl.num_programs(1) - 1, jnp.zeros_like(hn_ref[0]), hn_ref[0])
        u_ref[...] = _dot(hbuf_ref[...], wup_ref[...])
        cw = cw_ref[f]
        y = (u_ref[pl.ds(HALO - 1, tm), :] * cw[0:1, :] + u_ref[pl.ds(HALO, tm), :] * cw[1:2, :]
             + u_ref[pl.ds(HALO + 1, tm), :] * cw[2:3, :] + cbias_ref[f])
        act = jnp.concatenate(
            [y[:, 2 * j * LANES:(2 * j + 1) * LANES] * _silu(y[:, (2 * j + 1) * LANES:(2 * j + 2) * LANES])
             for j in SLAB_OF_BLOCK], axis=1).astype(BF16)
        half = TF_SUB // 2
        for rows in [slice(r, r + tm // DOWN_ROW_PARTS) for r in range(0, tm, tm // DOWN_ROW_PARTS)]:
            d = _dot(act[rows, :half], wdn_ref[:half, :]) + _dot(act[rows, half:], wdn_ref[half:, :])
            acc = d if first else o_ref[rows, :] + d
            if last:
                acc = _rms(x1_ref[rows, :] + mod_ref[0, 5:6, :] * acc, gfin_ref[...])
            o_ref[rows, :] = acc

    pl.when(f == 0)(lambda: step(True, False))
    pl.when((f > 0) & (f < nf - 1))(lambda: step(False, False))
    pl.when(f == nf - 1)(lambda: step(False, True))


def _ffn(hf, wup_t, cw_t, cbias_t, wdn, x1, mod_l, gfin, batch, seq):
    tm = TM_FFN
    tpb = seq // tm
    nf = N_SUB
    tf = TF_SUB
    n_h = batch * seq // HALO
    hf3 = hf.reshape(n_h, HALO, D_MODEL)
    tok = lambda b, t, f: (b * tpb + t, 0)
    return pl.pallas_call(
        _ffn_kernel,
        grid=(batch, tpb, nf),
        in_specs=[
            pl.BlockSpec((tm, D_MODEL), tok),
            pl.BlockSpec((1, HALO, D_MODEL),
                         lambda b, t, f: (jnp.maximum((b * tpb + t) * (tm // HALO) - 1, 0), 0, 0)),
            pl.BlockSpec((1, HALO, D_MODEL),
                         lambda b, t, f: (jnp.minimum((b * tpb + t + 1) * (tm // HALO), n_h - 1), 0, 0)),
            pl.BlockSpec((D_MODEL, 2 * TF_SUB), lambda b, t, f: (0, f)),
            pl.BlockSpec((nf, 3, 2 * TF_SUB), lambda b, t, f: (0, 0, 0)),
            pl.BlockSpec((nf, 1, 2 * TF_SUB), lambda b, t, f: (0, 0, 0)),
            pl.BlockSpec((tf, D_MODEL), lambda b, t, f: (f, 0)),
            pl.BlockSpec((tm, D_MODEL), tok),
            pl.BlockSpec((1, 6, D_MODEL), lambda b, t, f: (b, 0, 0)),
            pl.BlockSpec((1, D_MODEL), lambda b, t, f: (0, 0)),
        ],
        out_specs=pl.BlockSpec((tm, D_MODEL), tok),
        out_shape=jax.ShapeDtypeStruct((batch * seq, D_MODEL), F32),
        scratch_shapes=[
            pltpu.VMEM((tm + 2 * HALO, D_MODEL), BF16),
            pltpu.VMEM((tm + 2 * HALO, 2 * TF_SUB), F32),
        ],
        compiler_params=pltpu.CompilerParams(
            dimension_semantics=("parallel", "parallel", "arbitrary"), vmem_limit_bytes=VMEM_LIMIT),
        name="ffn",
    )(hf, hf3, hf3, wup_t, cw_t, cbias_t, wdn, x1, mod_l, gfin)


def _rope_tables(seq):
    rows = seq // GRID_W
    row = np.repeat(np.arange(rows), GRID_W).astype(np.float64)
    col = np.tile(np.arange(GRID_W), rows).astype(np.float64)
    axis_dim = QK_ROPE // 2
    inv = ROPE_BASE ** (-np.arange(0, axis_dim, 2, dtype=np.float64) / axis_dim)
    ang = np.concatenate([row[:, None] * inv, col[:, None] * inv], axis=-1)
    cos, sin = np.cos(ang), np.sin(ang)
    c = np.repeat(cos, 2, axis=-1)
    s = np.stack([-sin, sin], axis=-1).reshape(seq, QK_ROPE)
    reps = (1, LANES // QK_ROPE)
    return jnp.asarray(np.tile(c, reps), F32), jnp.asarray(np.tile(s, reps), F32)


def _tile_ffn_cols(a):
    pad = ((0, 0), (0, FFN_PAD - FFN_DIM))
    val = jnp.pad(a[:, :FFN_DIM], pad).reshape(a.shape[0], N_SUB, BPS, 1, LANES)
    gate = jnp.pad(a[:, FFN_DIM:], pad).reshape(a.shape[0], N_SUB, BPS, 1, LANES)
    block_of_slab = [SLAB_OF_BLOCK.index(p) for p in range(BPS)]
    both = jnp.concatenate([val, gate], axis=3)[:, :, block_of_slab].reshape(a.shape[0], N_SUB, 2 * TF_SUB)
    return jnp.moveaxis(both, 1, 0)


def kernel(x, c, ctx, c_ctx, w_ada, b_ada, g_mix_norm, w_in, g_q_a, w_q_b, g_kv_a, w_kv_b, conv_w, conv_b, g_mix_out, w_out, g_ffn_norm, w_up, ffn_conv_w, ffn_conv_b, w_down, g_final):
    batch, seq, d = x.shape
    assert d == D_MODEL and w_ada.shape[0] == 1
    x2 = x.reshape(batch * seq, d)

    mod, wconv, wqkv, wqb, wkvb, k_c, v_c = _ada(c, c_ctx[None], w_ada[0], b_ada, w_in[0].T, w_q_b[0], w_kv_b[0],
                                                 ctx, g_mix_norm, g_kv_a)
    mod = mod.reshape(mod.shape[0], 6, d)

    cw_t = _tile_ffn_cols(ffn_conv_w[0])
    cbias_t = _tile_ffn_cols(ffn_conv_b)

    cq, sq = _rope_tables(seq)

    q, k_l, v_l, cb, z = _in_lat(x2, mod, g_mix_norm, wqkv, wconv, g_q_a, wqb, g_kv_a, wkvb,
                                 cq, sq, batch, seq)
    att, wup_t, wdn, wout = _attn(q, k_l, v_l, k_c, v_c, w_up[0], w_down[0], w_out[0])
    x1, hf = _out(att.reshape(batch * seq, N_HEADS * HEAD), cb, z, conv_w[0], conv_b, g_mix_out,
                  wout, x2, mod, g_ffn_norm, batch, seq)
    y = _ffn(hf, wup_t, cw_t, cbias_t, wdn, x1, mod, g_final[None], batch, seq)
    return y.reshape(batch, seq, d)
```
